```python
import math
import jax, jax.numpy as jnp
from jax import lax
import numpy as np

D_MODEL = 1024
BATCH = 2
SEQ = 8192
DEPTH = 1

D_MIX = D_MODEL
D_MLSTM = D_MIX // 2
D_S5 = D_MIX - D_MLSTM
MLSTM_HEADS = 4
MLSTM_HEAD_DIM = D_MLSTM // MLSTM_HEADS
MLSTM_CHUNK = 64
CONV_WIDTH = 4
S5_GROUP_CH = 16
S5_GROUPS = D_S5 // S5_GROUP_CH
S5_STATE = 64
D_IN_PROJ = 4 * D_MLSTM + 2 * MLSTM_HEADS + D_S5
N_EXPERT_GROUPS = 4
EXPERTS_PER_GROUP = 8
N_EXPERTS = N_EXPERT_GROUPS * EXPERTS_PER_GROUP
TOP_K_IN_GROUP = 2
D_EXPERT = 512
EPS = 1e-6
LAMBDA_RE_MAX = -1e-4

kernel_name = "hymba_mlstm_s5_hiermoe_block"


def rmsnorm(x, g):
    xf = x.astype(jnp.float32)
    y = xf * lax.rsqrt(jnp.mean(xf * xf, axis=-1, keepdims=True) + EPS) * g.astype(jnp.float32)
    return y.astype(x.dtype)


def causal_depthwise_conv(x, w, b):
    L = x.shape[1]
    W = w.shape[0]
    xp = jnp.pad(x, ((0, 0), (W - 1, 0), (0, 0)))
    out = b
    for j in range(W):
        out = out + xp[:, j:j + L, :] * w[j]
    return out


def mlstm_chunkwise(q, k, v, log_i, log_f):
    Bsz, H, L, d = q.shape
    nc = L // MLSTM_CHUNK

    def chunks(a):
        a = a.reshape(a.shape[:2] + (nc, MLSTM_CHUNK) + a.shape[3:])
        return jnp.moveaxis(a, 2, 0)

    mask = jnp.tril(jnp.ones((MLSTM_CHUNK, MLSTM_CHUNK), dtype=bool))

    def step(carry, inp):
        C, n, m = carry
        qc, kc, vc, li, lf = inp
        b = jnp.cumsum(lf, axis=-1)
        dmat = b[..., :, None] - b[..., None, :] + li[..., None, :]
        dmat = jnp.where(mask, dmat, -jnp.inf)
        inter = b + m[..., None]
        m_row = jnp.maximum(inter, jnp.max(dmat, axis=-1))
        w_intra = jnp.exp(dmat - m_row[..., None])
        w_inter = jnp.exp(inter - m_row)
        s = jnp.einsum('bhjd,bhsd->bhjs', qc, kc) * w_intra
        num = jnp.einsum('bhjs,bhsd->bhjd', s, vc) + w_inter[..., None] * jnp.einsum('bhvk,bhjk->bhjv', C, qc)
        den = jnp.sum(s, axis=-1) + w_inter * jnp.einsum('bhk,bhjk->bhj', n, qc)
        h = num / jnp.maximum(jnp.abs(den), jnp.exp(-m_row))[..., None]
        b_last = b[..., -1]
        d_state = b_last[..., None] - b + li
        m_new = jnp.maximum(b_last + m, jnp.max(d_state, axis=-1))
        w_s = jnp.exp(d_state - m_new[..., None])
        w_c = jnp.exp(b_last + m - m_new)
        C_new = w_c[..., None, None] * C + jnp.einsum('bhs,bhsv,bhsk->bhvk', w_s, vc, kc)
        n_new = w_c[..., None] * n + jnp.einsum('bhs,bhsk->bhk', w_s, kc)
        return (C_new, n_new, m_new), h

    init = (jnp.zeros((Bsz, H, d, d), jnp.float32),
            jnp.zeros((Bsz, H, d), jnp.float32),
            jnp.zeros((Bsz, H), jnp.float32))
    _, h = lax.scan(step, init, (chunks(q), chunks(k), chunks(v), chunks(log_i), chunks(log_f)))
    h = jnp.moveaxis(h, 0, 2)
    return h.reshape(Bsz, H, L, d)


def s5_combine(e1, e2):
    a1r, a1i, b1r, b1i = e1
    a2r, a2i, b2r, b2i = e2
    return (a2r * a1r - a2i * a1i,
            a2r * a1i + a2i * a1r,
            a2r * b1r - a2i * b1i + b2r,
            a2r * b1i + a2i * b1r + b2i)


def s5_group_mixer(u, lam_re, lam_im, log_dt, b_re, b_im, c_re, c_im, d_skip, glu_w, glu_b):
    f32 = jnp.float32
    lr = jnp.minimum(lam_re.astype(f32), LAMBDA_RE_MAX)
    li = lam_im.astype(f32)
    dt = jnp.exp(log_dt.astype(f32))[:, None]
    mag = jnp.exp(lr * dt)
    ar = mag * jnp.cos(li * dt)
    ai = mag * jnp.sin(li * dt)
    den = lr * lr + li * li
    cr = ((ar - 1.0) * lr + ai * li) / den
    ci = (ai * lr - (ar - 1.0) * li) / den
    br, bi = b_re.astype(f32), b_im.astype(f32)
    bbar_r = cr[..., None] * br - ci[..., None] * bi
    bbar_i = cr[..., None] * bi + ci[..., None] * br
    bu_r = jnp.einsum('blgh,gnh->blgn', u, bbar_r)
    bu_i = jnp.einsum('blgh,gnh->blgn', u, bbar_i)
    a_r = jnp.broadcast_to(ar, bu_r.shape)
    a_i = jnp.broadcast_to(ai, bu_i.shape)
    _, _, s_r, s_i = lax.associative_scan(s5_combine, (a_r, a_i, bu_r, bu_i), axis=1)
    y = (jnp.einsum('blgn,ghn->blgh', s_r, c_re.astype(f32))
         - jnp.einsum('blgn,ghn->blgh', s_i, c_im.astype(f32))
         + d_skip.astype(f32) * u)
    y = jax.nn.gelu(y)
    z = jnp.einsum('blgh,ghc->blgc', y, glu_w.astype(f32)) + glu_b.astype(f32)
    return z[..., :S5_GROUP_CH] * jax.nn.sigmoid(z[..., S5_GROUP_CH:])


def hybrid_mixer(h, w_in, conv_w, conv_b, i_bias, f_bias, mlstm_norm_g,
                 lam_re, lam_im, log_dt, b_re, b_im, c_re, c_im, d_skip, glu_w, glu_b, w_out):
    f32 = jnp.float32
    Bsz, L, _ = h.shape
    H, d = MLSTM_HEADS, MLSTM_HEAD_DIM
    proj = (h @ w_in).astype(f32)
    qk_raw = proj[..., :2 * D_MLSTM]
    v = proj[..., 2 * D_MLSTM:3 * D_MLSTM]
    o_pre = proj[..., 3 * D_MLSTM:4 * D_MLSTM]
    i_pre = proj[..., 4 * D_MLSTM:4 * D_MLSTM + H] + i_bias.astype(f32)
    f_pre = proj[..., 4 * D_MLSTM + H:4 * D_MLSTM + 2 * H] + f_bias.astype(f32)
    u = proj[..., 4 * D_MLSTM + 2 * H:]

    qk = jax.nn.silu(causal_depthwise_conv(qk_raw, conv_w.astype(f32), conv_b.astype(f32)))
    q, k = qk[..., :D_MLSTM], qk[..., D_MLSTM:]

    def heads(a):
        return a.reshape(Bsz, L, H, d).transpose(0, 2, 1, 3)

    log_i = i_pre.transpose(0, 2, 1)
    log_f = jax.nn.log_sigmoid(f_pre).transpose(0, 2, 1)
    h_m = mlstm_chunkwise(heads(q), heads(k) * (d ** -0.5), heads(v), log_i, log_f)
    h_m = h_m * lax.rsqrt(jnp.mean(h_m * h_m, axis=-1, keepdims=True) + EPS)
    h_m = h_m.transpose(0, 2, 1, 3).reshape(Bsz, L, D_MLSTM) * mlstm_norm_g.astype(f32)
    out_a = jax.nn.sigmoid(o_pre) * h_m

    out_b = s5_group_mixer(u.reshape(Bsz, L, S5_GROUPS, S5_GROUP_CH), lam_re, lam_im, log_dt,
                           b_re, b_im, c_re, c_im, d_skip, glu_w, glu_b).reshape(Bsz, L, D_S5)

    merged = jnp.concatenate([out_a, out_b], axis=-1).astype(h.dtype)
    return merged @ w_out


def hierarchical_moe(h, rg_w, rg_b, re_w, re_b, w_gate, w_up, w_down):
    f32 = jnp.float32
    Bsz, L, D = h.shape
    t = h.reshape(Bsz * L, D)
    T = t.shape[0]
    g_logits = (t @ rg_w).astype(f32) + rg_b.astype(f32)
    p_g = jax.nn.softmax(g_logits, axis=-1)
    g_w, g_idx = lax.top_k(p_g, 1)
    e_logits = ((t @ re_w).astype(f32) + re_b.astype(f32)).reshape(T, N_EXPERT_GROUPS, EXPERTS_PER_GROUP)
    e_logits = jnp.take_along_axis(e_logits, g_idx[:, :, None], axis=1)[:, 0]
    p_e = jax.nn.softmax(e_logits, axis=-1)
    e_w, e_idx = lax.top_k(p_e, TOP_K_IN_GROUP)
    e_w = e_w / jnp.sum(e_w, axis=-1, keepdims=True)
    weights = g_w * e_w
    expert_id = g_idx * EXPERTS_PER_GROUP + e_idx
    combine = jnp.einsum('tk,tke->te', weights, jax.nn.one_hot(expert_id, N_EXPERTS, dtype=f32))
    combine = combine.astype(h.dtype)
    y = jnp.zeros_like(t)
    for e in range(N_EXPERTS):
        ye = (jax.nn.silu(t @ w_gate[e]) * (t @ w_up[e])) @ w_down[e]
        y = y + combine[:, e:e + 1] * ye
    return y.reshape(Bsz, L, D)


def setup_inputs(seed: int = 0) -> dict:
    key = jax.random.key(seed)
    ks = jax.random.split(key, 32)
    f32 = jnp.float32
    nrm = lambda k, shape, s: jax.random.normal(k, shape, f32) * s
    H = MLSTM_HEADS
    G, N, C = S5_GROUPS, S5_STATE, S5_GROUP_CH
    x = jax.random.normal(ks[0], (BATCH, SEQ, D_MODEL), f32)
    norm_mix_g = 1.0 + nrm(ks[1], (DEPTH, D_MODEL), 0.02)
    w_in = nrm(ks[2], (DEPTH, D_MODEL, D_IN_PROJ), D_MODEL ** -0.5)
    conv_w = nrm(ks[3], (DEPTH, CONV_WIDTH, 2 * D_MLSTM), CONV_WIDTH ** -0.5)
    conv_b = nrm(ks[4], (DEPTH, 2 * D_MLSTM), 0.01)
    i_bias = nrm(ks[5], (DEPTH, H), 0.1)
    f_bias = jnp.linspace(3.0, 6.0, H, dtype=f32)[None, :] + nrm(ks[6], (DEPTH, H), 0.1)
    mlstm_norm_g = 1.0 + nrm(ks[7], (DEPTH, D_MLSTM), 0.02)
    s5_lambda_re = -0.5 + nrm(ks[8], (DEPTH, G, N), 0.01)
    s5_lambda_im = math.pi * jnp.arange(N, dtype=f32)[None, None, :] + nrm(ks[9], (DEPTH, G, N), 0.01)
    s5_log_dt = jax.random.uniform(ks[10], (DEPTH, G), f32, math.log(0.001), math.log(0.1))
    s5_b_re = nrm(ks[11], (DEPTH, G, N, C), (2.0 * C) ** -0.5)
    s5_b_im = nrm(ks[12], (DEPTH, G, N, C), (2.0 * C) ** -0.5)
    s5_c_re = nrm(ks[13], (DEPTH, G, C, N), (2.0 * N) ** -0.5)
    s5_c_im = nrm(ks[14], (DEPTH, G, C, N), (2.0 * N) ** -0.5)
    s5_d = nrm(ks[15], (DEPTH, G, C), 1.0)
    s5_glu_w = nrm(ks[16], (DEPTH, G, C, 2 * C), C ** -0.5)
    s5_glu_b = nrm(ks[17], (DEPTH, G, 2 * C), 0.01)
    w_out = nrm(ks[18], (DEPTH, D_MIX, D_MODEL), D_MIX ** -0.5)
    norm_ffn_g = 1.0 + nrm(ks[19], (DEPTH, D_MODEL), 0.02)
    router_group_w = nrm(ks[20], (DEPTH, D_MODEL, N_EXPERT_GROUPS), D_MODEL ** -0.5)
    router_group_b = nrm(ks[21], (DEPTH, N_EXPERT_GROUPS), 0.01)
    router_expert_w = nrm(ks[22], (DEPTH, D_MODEL, N_EXPERTS), D_MODEL ** -0.5)
    router_expert_b = nrm(ks[23], (DEPTH, N_EXPERTS), 0.01)
    expert_w_gate = nrm(ks[24], (DEPTH, N_EXPERTS, D_MODEL, D_EXPERT), D_MODEL ** -0.5)
    expert_w_up = nrm(ks[25], (DEPTH, N_EXPERTS, D_MODEL, D_EXPERT), D_MODEL ** -0.5)
    expert_w_down = nrm(ks[26], (DEPTH, N_EXPERTS, D_EXPERT, D_MODEL), D_EXPERT ** -0.5)
    norm_final_g = 1.0 + nrm(ks[27], (D_MODEL,), 0.02)
    return {"x": x, "norm_mix_g": norm_mix_g, "w_in": w_in, "conv_w": conv_w, "conv_b": conv_b,
            "i_bias": i_bias, "f_bias": f_bias, "mlstm_norm_g": mlstm_norm_g,
            "s5_lambda_re": s5_lambda_re, "s5_lambda_im": s5_lambda_im, "s5_log_dt": s5_log_dt,
            "s5_b_re": s5_b_re, "s5_b_im": s5_b_im, "s5_c_re": s5_c_re, "s5_c_im": s5_c_im,
            "s5_d": s5_d, "s5_glu_w": s5_glu_w, "s5_glu_b": s5_glu_b, "w_out": w_out,
            "norm_ffn_g": norm_ffn_g, "router_group_w": router_group_w, "router_group_b": router_group_b,
            "router_expert_w": router_expert_w, "router_expert_b": router_expert_b,
            "expert_w_gate": expert_w_gate, "expert_w_up": expert_w_up, "expert_w_down": expert_w_down,
            "norm_final_g": norm_final_g}


def reference(x, norm_mix_g, w_in, conv_w, conv_b, i_bias, f_bias, mlstm_norm_g,
              s5_lambda_re, s5_lambda_im, s5_log_dt, s5_b_re, s5_b_im, s5_c_re, s5_c_im,
              s5_d, s5_glu_w, s5_glu_b, w_out, norm_ffn_g, router_group_w, router_group_b,
              router_expert_w, router_expert_b, expert_w_gate, expert_w_up, expert_w_down,
              norm_final_g):
    for l in range(DEPTH):
        h = rmsnorm(x, norm_mix_g[l])
        mix = hybrid_mixer(h, w_in[l], conv_w[l], conv_b[l], i_bias[l], f_bias[l], mlstm_norm_g[l],
                           s5_lambda_re[l], s5_lambda_im[l], s5_log_dt[l], s5_b_re[l], s5_b_im[l],
                           s5_c_re[l], s5_c_im[l], s5_d[l], s5_glu_w[l], s5_glu_b[l], w_out[l])
        x = x + mix.astype(x.dtype)
        h = rmsnorm(x, norm_ffn_g[l])
        ffn = hierarchical_moe(h, router_group_w[l], router_group_b[l], router_expert_w[l], router_expert_b[l],
                               expert_w_gate[l], expert_w_up[l], expert_w_down[l])
        x = x + ffn.astype(x.dtype)
    return rmsnorm(x, norm_final_g)
```

```python
import functools
import math

import jax
import jax.numpy as jnp
from jax import lax
from jax.experimental import pallas as pl
from jax.experimental.pallas import tpu as pltpu

F32 = jnp.float32
BF16 = jnp.bfloat16
HIGHEST = lax.Precision.HIGHEST

D_MODEL = 1024
D_MLSTM = 512
D_S5 = 512
HEADS = 4
HEAD_DIM = 128
CONV_WIDTH = 4
S5_GROUP_CH = 16
S5_GROUPS = 32
S5_STATE = 64
N_GROUPS = 4
PER_GROUP = 8
N_EXPERTS = 32
D_EXPERT = 512
EPS = 1e-6
LAMBDA_RE_MAX = -1e-4

VMEM_LIMIT_BYTES = 56 * 1024 * 1024

INPROJ_TILE = 512
MLSTM_CHUNK = 256
S5_BLOCK = 16
S5_PAIR = 2
OUTPROJ_TILE = 512
MOE_TILE = 1024


def _nt_dims():
    return (((1,), (1,)), ((), ()))


def _tn_dims():
    return (((0,), (0,)), ((), ()))


def _log_sigmoid(x):
    return jnp.minimum(x, 0.0) - jnp.log1p(jnp.exp(-jnp.abs(x)))


def _inproj_kernel(tiles_per_seq, x_ref, g_ref, wqk_ref, wv_ref, wo_ref, wu_ref, wg_ref, wgt_ref,
                   gbc_ref, gbr_ref, cw_ref, cb_ref,
                   q_ref, k_ref, v_ref, og_ref, u_ref, gcol_ref, grow_ref, ext_ref):
    i = pl.program_id(0)
    tm = x_ref.shape[0]
    x = x_ref[...]
    h = x * lax.rsqrt(jnp.mean(x * x, axis=-1, keepdims=True) + EPS) * g_ref[...]
    hb = h.astype(BF16)

    qk_raw = jnp.dot(hb, wqk_ref[...], preferred_element_type=F32)

    @pl.when(i % tiles_per_seq == 0)
    def _():
        ext_ref[0:8, :] = jnp.zeros((8, ext_ref.shape[1]), F32)

    ext_ref[8:, :] = qk_raw
    cw = cw_ref[...]
    acc = cb_ref[...] + cw[CONV_WIDTH - 1:CONV_WIDTH, :] * qk_raw
    for s in range(1, CONV_WIDTH):
        acc = acc + cw[CONV_WIDTH - 1 - s:CONV_WIDTH - s, :] * ext_ref[pl.ds(8 - s, tm), :]
    ext_ref[0:8, :] = qk_raw[tm - 8:, :]
    qk = acc * jax.nn.sigmoid(acc)
    q_ref[...] = qk[:, :D_MLSTM].astype(BF16)
    k_ref[...] = (qk[:, D_MLSTM:] * (HEAD_DIM ** -0.5)).astype(BF16)

    v_ref[...] = jnp.dot(hb, wv_ref[...], preferred_element_type=F32).astype(BF16)
    o_pre = jnp.dot(hb, wo_ref[...], preferred_element_type=F32)
    og_ref[...] = jax.nn.sigmoid(o_pre).astype(BF16)
    u_ref[...] = jnp.dot(hb, wu_ref[...], preferred_element_type=F32).astype(BF16)

    gcol_ref[...] = jnp.dot(h, wg_ref[...], precision=HIGHEST, preferred_element_type=F32) + gbc_ref[...]
    grow_ref[...] = lax.dot_general(wgt_ref[...], h, _nt_dims(), precision=HIGHEST,
                                    preferred_element_type=F32) + gbr_ref[...]


def _inproj(x2d, norm_g, w_in, conv_w, conv_b, i_bias, f_bias, seq_len):
    T = x2d.shape[0]
    tm = INPROJ_TILE
    dm = D_MLSTM
    wqk = w_in[:, :2 * dm].astype(BF16)
    wv = w_in[:, 2 * dm:3 * dm].astype(BF16)
    wo = w_in[:, 3 * dm:4 * dm].astype(BF16)
    wg = w_in[:, 4 * dm:4 * dm + 2 * HEADS]
    wu = w_in[:, 4 * dm + 2 * HEADS:].astype(BF16)
    gb = jnp.concatenate([i_bias, f_bias]).astype(F32)
    full = lambda shape: pl.BlockSpec(shape, lambda i: (0,) * len(shape))
    row = lambda n: pl.BlockSpec((tm, n), lambda i: (i, 0))
    return pl.pallas_call(
        functools.partial(_inproj_kernel, seq_len // tm),
        grid=(T // tm,),
        in_specs=[row(D_MODEL), full((1, D_MODEL)), full((D_MODEL, 2 * dm)), full((D_MODEL, dm)),
                  full((D_MODEL, dm)), full((D_MODEL, D_S5)), full((D_MODEL, 2 * HEADS)),
                  full((2 * HEADS, D_MODEL)), full((1, 2 * HEADS)), full((2 * HEADS, 1)),
                  full((CONV_WIDTH, 2 * dm)), full((1, 2 * dm))],
        out_specs=[row(dm), row(dm), row(dm), row(dm), row(D_S5), row(2 * HEADS),
                   pl.BlockSpec((2 * HEADS, tm), lambda i: (0, i))],
        out_shape=[jax.ShapeDtypeStruct((T, dm), BF16)] * 4
        + [jax.ShapeDtypeStruct((T, D_S5), BF16),
           jax.ShapeDtypeStruct((T, 2 * HEADS), F32),
           jax.ShapeDtypeStruct((2 * HEADS, T), F32)],
        scratch_shapes=[pltpu.VMEM((tm + 8, 2 * dm), F32)],
        compiler_params=pltpu.CompilerParams(dimension_semantics=("arbitrary",),
                                             vmem_limit_bytes=VMEM_LIMIT_BYTES),
        name="inproj",
    )(x2d, norm_g.reshape(1, D_MODEL), wqk, wv, wo, wu, wg, wg.T, gb.reshape(1, -1), gb.reshape(-1, 1),
      conv_w, conv_b.reshape(1, -1))


def _mlstm_kernel(q_ref, k_ref, v_ref, og_ref, gcol_ref, grow_ref, ng_ref, out_ref, c_ref, m_ref):
    tc = q_ref.shape[0]
    d = HEAD_DIM

    @pl.when(pl.program_id(1) == 0)
    def _():
        c_ref[...] = jnp.zeros(c_ref.shape, F32)
        m_ref[...] = jnp.zeros(m_ref.shape, F32)

    gcol = gcol_ref[...]
    grow = grow_ref[...]
    rix = lax.broadcasted_iota(jnp.int32, (tc, tc), 0)
    cix = lax.broadcasted_iota(jnp.int32, (tc, tc), 1)
    causal = rix >= cix
    b_col_all = jnp.dot(causal.astype(F32), _log_sigmoid(gcol), precision=HIGHEST,
                        preferred_element_type=F32)
    b_row_all = jnp.dot(_log_sigmoid(grow), (rix <= cix).astype(F32), precision=HIGHEST,
                        preferred_element_type=F32)
    ones_col = (lax.broadcasted_iota(jnp.int32, (tc, d), 1) == 0).astype(BF16)

    for h in range(HEADS):
        hs = slice(h * d, (h + 1) * d)
        li_col = gcol[:, h:h + 1]
        li_row = grow[h:h + 1, :]
        b_col = b_col_all[:, HEADS + h:HEADS + h + 1]
        b_row = b_row_all[HEADS + h:HEADS + h + 1, :]
        m_prev = m_ref[h:h + 1, 0:1]

        dmat = jnp.where(causal, b_col - b_row + li_row, -jnp.inf)
        inter = b_col + m_prev
        m_row = jnp.maximum(inter, jnp.max(dmat, axis=1, keepdims=True))
        w_intra = jnp.exp(dmat - m_row)
        w_inter = jnp.exp(inter - m_row)

        qh = q_ref[:, hs]
        kh = k_ref[:, hs]
        v_aug = jnp.concatenate([v_ref[:, hs], ones_col], axis=1)
        s = lax.dot_general(qh, kh, _nt_dims(), preferred_element_type=F32) * w_intra
        c_aug = c_ref[h]
        tot = (jnp.dot(s.astype(BF16), v_aug, preferred_element_type=F32)
               + w_inter * lax.dot_general(qh, c_aug.astype(BF16), _nt_dims(), preferred_element_type=F32))
        num = tot[:, :d]
        den = tot[:, d:d + 1]
        hh = num / jnp.maximum(jnp.abs(den), jnp.exp(-m_row))
        hh = hh * lax.rsqrt(jnp.mean(hh * hh, axis=-1, keepdims=True) + EPS)
        out_ref[:, hs] = (og_ref[:, hs].astype(F32) * (hh * ng_ref[:, hs])).astype(BF16)

        b_last = b_col[tc - 1:tc, :]
        d_state = b_last - b_col + li_col
        m_new = jnp.maximum(b_last + m_prev, jnp.max(d_state, axis=0, keepdims=True))
        w_s = jnp.exp(d_state - m_new)
        w_c = jnp.exp(b_last + m_prev - m_new)
        upd = lax.dot_general((v_aug.astype(F32) * w_s).astype(BF16), kh, _tn_dims(),
                              preferred_element_type=F32)
        c_ref[h] = w_c * c_aug + upd
        m_ref[h:h + 1, :] = jnp.broadcast_to(m_new, (1, m_ref.shape[1]))


def _mlstm(q, k, v, og, gcol, grow, norm_g, batch, seq_len):
    tc = MLSTM_CHUNK
    nc = seq_len // tc
    T = q.shape[0]
    row = lambda n: pl.BlockSpec((tc, n), lambda b, c: (b * nc + c, 0))
    return pl.pallas_call(
        _mlstm_kernel,
        grid=(batch, nc),
        in_specs=[row(D_MLSTM), row(D_MLSTM), row(D_MLSTM), row(D_MLSTM), row(2 * HEADS),
                  pl.BlockSpec((2 * HEADS, tc), lambda b, c: (0, b * nc + c)),
                  pl.BlockSpec((1, D_MLSTM), lambda b, c: (0, 0))],
        out_specs=row(D_MLSTM),
        out_shape=jax.ShapeDtypeStruct((T, D_MLSTM), BF16),
        scratch_shapes=[pltpu.VMEM((HEADS, 2 * HEAD_DIM, HEAD_DIM), F32), pltpu.VMEM((8, 128), F32)],
        compiler_params=pltpu.CompilerParams(dimension_semantics=("parallel", "arbitrary"),
                                             vmem_limit_bytes=VMEM_LIMIT_BYTES),
        name="mlstm",
    )(q, k, v, og, gcol, grow, norm_g.reshape(1, D_MLSTM))


def _s5_tables(lam_re, lam_im, log_dt, b_re, b_im, c_re, c_im, d_skip, glu_w, glu_b, n_blocks):
    G, N, C, P = S5_GROUPS, S5_STATE, S5_GROUP_CH, S5_BLOCK
    lr = jnp.minimum(lam_re.astype(F32), LAMBDA_RE_MAX)
    li = lam_im.astype(F32)
    dt = jnp.exp(log_dt.astype(F32))[:, None]
    mag = jnp.exp(lr * dt)
    ar = mag * jnp.cos(li * dt)
    ai = mag * jnp.sin(li * dt)
    den = lr * lr + li * li
    cr = ((ar - 1.0) * lr + ai * li) / den
    ci = (ai * lr - (ar - 1.0) * li) / den
    br, bi = b_re.astype(F32), b_im.astype(F32)
    bbr = cr[..., None] * br - ci[..., None] * bi
    bbi = cr[..., None] * bi + ci[..., None] * br
    ccr, cci = c_re.astype(F32), c_im.astype(F32)

    def powers(e):
        e = e.astype(F32)[:, None, None]
        m = jnp.exp(e * (lr * dt)[None])
        return m * jnp.cos(e * (li * dt)[None]), m * jnp.sin(e * (li * dt)[None])

    pr, pi = powers(jnp.arange(P + 1))
    kr = (jnp.einsum('gon,lgn,gni->lgoi', ccr, pr[:P], bbr, precision=HIGHEST)
          - jnp.einsum('gon,lgn,gni->lgoi', cci, pi[:P], bbr, precision=HIGHEST)
          - jnp.einsum('gon,lgn,gni->lgoi', ccr, pi[:P], bbi, precision=HIGHEST)
          - jnp.einsum('gon,lgn,gni->lgoi', cci, pr[:P], bbi, precision=HIGHEST))
    lag = jnp.arange(P)[None, :] - jnp.arange(P)[:, None]
    kt = jnp.where((lag >= 0)[:, :, None, None, None], kr[jnp.clip(lag, 0)], 0.0)
    ktoep = kt.transpose(2, 0, 4, 1, 3).reshape(G, P * C, P * C)
    er, ei = pr[:P][::-1], pi[:P][::-1]
    we_r = (jnp.einsum('pgn,gnc->gpcn', er, bbr) - jnp.einsum('pgn,gnc->gpcn', ei, bbi)).reshape(G, P * C, N)
    we_i = (jnp.einsum('pgn,gnc->gpcn', er, bbi) + jnp.einsum('pgn,gnc->gpcn', ei, bbr)).reshape(G, P * C, N)
    qr, qi = pr[1:], pi[1:]
    wc_r = (jnp.einsum('gon,pgn->gnpo', ccr, qr) - jnp.einsum('gon,pgn->gnpo', cci, qi)).reshape(G, N, P * C)
    wc_i = -(jnp.einsum('gon,pgn->gnpo', ccr, qi) + jnp.einsum('gon,pgn->gnpo', cci, qr)).reshape(G, N, P * C)
    eye_p = jnp.eye(P, dtype=F32)
    gw = glu_w.astype(F32)
    gw1 = jnp.einsum('pq,gio->gpiqo', eye_p, gw[..., :C]).reshape(G, P * C, P * C)
    gw2 = jnp.einsum('pq,gio->gpiqo', eye_p, gw[..., C:]).reshape(G, P * C, P * C)
    gb = glu_b.astype(F32)
    gb1 = jnp.tile(gb[:, None, :C], (1, P, 1)).reshape(G, P * C)
    gb2 = jnp.tile(gb[:, None, C:], (1, P, 1)).reshape(G, P * C)
    dsk = jnp.tile(d_skip.astype(F32)[:, None, :], (1, P, 1)).reshape(G, P * C)

    def bdiag(m):
        g, r, k = m.shape
        m = m.reshape(g // S5_PAIR, S5_PAIR, r, k)
        eye = jnp.eye(S5_PAIR, dtype=m.dtype)
        return jnp.einsum('jark,ab->jarbk', m, eye).reshape(g // S5_PAIR, S5_PAIR * r, S5_PAIR * k)

    def pair_vec(v):
        return v.reshape(G // S5_PAIR, 1, -1)

    we = jnp.concatenate([bdiag(we_r), bdiag(we_i)], axis=2).astype(BF16)
    wc = jnp.concatenate([bdiag(wc_r), bdiag(wc_i)], axis=1).astype(BF16)
    kt2 = bdiag(ktoep).astype(BF16)
    glu = jnp.concatenate([bdiag(gw1), bdiag(gw2)], axis=2).astype(BF16)
    glub = jnp.concatenate([pair_vec(gb1), pair_vec(gb2)], axis=2)
    dsk2 = pair_vec(dsk)
    n_steps = max(1, (n_blocks - 1).bit_length())
    sr, si = powers(P * (2 ** jnp.arange(n_steps)))
    sr = sr.transpose(1, 0, 2).reshape(G // S5_PAIR, S5_PAIR, n_steps, N).transpose(0, 2, 1, 3)
    si = si.transpose(1, 0, 2).reshape(G // S5_PAIR, S5_PAIR, n_steps, N).transpose(0, 2, 1, 3)
    sr = sr.reshape(G // S5_PAIR, n_steps, S5_PAIR * N)
    si = si.reshape(G // S5_PAIR, n_steps, S5_PAIR * N)
    return we, kt2, wc, glu, glub, dsk2, sr, si, n_steps


def _s5_kernel(n_steps, u_ref, we_ref, kt_ref, wc_ref, glu_ref, glub_ref, dsk_ref, sr_ref, si_ref, out_ref):
    u = u_ref[...]
    nb = u.shape[0]
    half = S5_PAIR * S5_STATE
    e = jnp.dot(u, we_ref[0], preferred_element_type=F32)
    xr, xi = e[:, :half], e[:, half:]
    row = lax.broadcasted_iota(jnp.int32, (nb, half), 0)
    for k in range(n_steps):
        sh = 1 << k
        ar = sr_ref[0, k:k + 1, :]
        ai = si_ref[0, k:k + 1, :]
        valid = row >= sh
        yr = jnp.where(valid, pltpu.roll(xr, sh, axis=0), 0.0)
        yi = jnp.where(valid, pltpu.roll(xi, sh, axis=0), 0.0)
        xr, xi = xr + ar * yr - ai * yi, xi + ar * yi + ai * yr
    first = row >= 1
    pr = jnp.where(first, pltpu.roll(xr, 1, axis=0), 0.0)
    pi = jnp.where(first, pltpu.roll(xi, 1, axis=0), 0.0)
    sprev = jnp.concatenate([pr, pi], axis=1).astype(BF16)
    y = (jnp.dot(u, kt_ref[0], preferred_element_type=F32)
         + jnp.dot(sprev, wc_ref[0], preferred_element_type=F32)
         + dsk_ref[0] * u.astype(F32))
    y = jax.nn.gelu(y)
    z = jnp.dot(y.astype(BF16), glu_ref[0], preferred_element_type=F32) + glub_ref[0]
    w = z.shape[1] // 2
    out_ref[...] = (z[:, :w] * jax.nn.sigmoid(z[:, w:])).astype(BF16)


def _s5(u, tables, batch, seq_len):
    we, kt2, wc, glu, glub, dsk2, sr, si, n_steps = tables
    T = u.shape[0]
    P, C, G = S5_BLOCK, S5_GROUP_CH, S5_GROUPS
    nb = seq_len // P
    lanes = S5_PAIR * P * C
    u_blk = u.reshape(T // P, P, G, C).transpose(0, 2, 1, 3).reshape(T // P, G * P * C)
    per_pair = lambda a: pl.BlockSpec((1,) + a.shape[1:], lambda b, j: (j, 0, 0))
    out_blk = pl.pallas_call(
        functools.partial(_s5_kernel, n_steps),
        grid=(batch, G // S5_PAIR),
        in_specs=[pl.BlockSpec((nb, lanes), lambda b, j: (b, j)),
                  per_pair(we), per_pair(kt2), per_pair(wc), per_pair(glu), per_pair(glub),
                  per_pair(dsk2), per_pair(sr), per_pair(si)],
        out_specs=pl.BlockSpec((nb, lanes), lambda b, j: (b, j)),
        out_shape=jax.ShapeDtypeStruct((T // P, G * P * C), BF16),
        compiler_params=pltpu.CompilerParams(dimension_semantics=("parallel", "parallel"),
                                             vmem_limit_bytes=VMEM_LIMIT_BYTES),
        name="s5",
    )(u_blk, we, kt2, wc, glu, glub, dsk2, sr, si)
    return out_blk.reshape(T // P, G, P, C).transpose(0, 2, 1, 3).reshape(T, G * C)


def _outproj_kernel(a_ref, b_ref, x_ref, wa_ref, wb_ref, g_ref, rgw_ref, rgb_ref, rew_ref, reb_ref,
                    x1_ref, h2_ref, comb_ref):
    x1 = (x_ref[...]
          + jnp.dot(a_ref[...], wa_ref[...], preferred_element_type=F32)
          + jnp.dot(b_ref[...], wb_ref[...], preferred_element_type=F32))
    x1_ref[...] = x1
    h2 = x1 * lax.rsqrt(jnp.mean(x1 * x1, axis=-1, keepdims=True) + EPS) * g_ref[...]
    h2_ref[...] = h2.astype(BF16)

    gl = jnp.dot(h2, rgw_ref[...], precision=HIGHEST, preferred_element_type=F32) + rgb_ref[...]
    el = jnp.dot(h2, rew_ref[...], precision=HIGHEST, preferred_element_type=F32) + reb_ref[...]
    tm = gl.shape[0]
    gmax = jnp.max(gl, axis=1, keepdims=True)
    g_w = 1.0 / jnp.sum(jnp.exp(gl - gmax), axis=1, keepdims=True)
    lane_g = lax.broadcasted_iota(jnp.int32, (tm, N_GROUPS), 1)
    g_idx = jnp.min(jnp.where(gl == gmax, lane_g, N_GROUPS), axis=1, keepdims=True)
    lane_e = lax.broadcasted_iota(jnp.int32, (tm, N_EXPERTS), 1)
    elm = jnp.where(lane_e // PER_GROUP == g_idx, el, -jnp.inf)
    m1 = jnp.max(elm, axis=1, keepdims=True)
    i1 = jnp.min(jnp.where(elm == m1, lane_e, N_EXPERTS), axis=1, keepdims=True)
    elm2 = jnp.where(lane_e == i1, -jnp.inf, elm)
    m2 = jnp.max(elm2, axis=1, keepdims=True)
    i2 = jnp.min(jnp.where(elm2 == m2, lane_e, N_EXPERTS), axis=1, keepdims=True)
    e2 = jnp.exp(m2 - m1)
    w1 = g_w / (1.0 + e2)
    w2 = g_w * e2 / (1.0 + e2)
    comb_ref[...] = jnp.where(lane_e == i1, w1, 0.0) + jnp.where(lane_e == i2, w2, 0.0)


def _outproj(out_a, out_b, x2d, w_out, norm_g, rg_w, rg_b, re_w, re_b):
    T = x2d.shape[0]
    tm = OUTPROJ_TILE
    wa = w_out[:D_MLSTM].astype(BF16)
    wb = w_out[D_MLSTM:].astype(BF16)
    full = lambda shape: pl.BlockSpec(shape, lambda i: (0,) * len(shape))
    row = lambda n: pl.BlockSpec((tm, n), lambda i: (i, 0))
    return pl.pallas_call(
        _outproj_kernel,
        grid=(T // tm,),
        in_specs=[row(D_MLSTM), row(D_S5), row(D_MODEL), full((D_MLSTM, D_MODEL)), full((D_S5, D_MODEL)),
                  full((1, D_MODEL)), full((D_MODEL, N_GROUPS)), full((1, N_GROUPS)),
                  full((D_MODEL, N_EXPERTS)), full((1, N_EXPERTS))],
        out_specs=[row(D_MODEL), row(D_MODEL), row(N_EXPERTS)],
        out_shape=[jax.ShapeDtypeStruct((T, D_MODEL), F32), jax.ShapeDtypeStruct((T, D_MODEL), BF16),
                   jax.ShapeDtypeStruct((T, N_EXPERTS), F32)],
        compiler_params=pltpu.CompilerParams(dimension_semantics=("parallel",),
                                             vmem_limit_bytes=VMEM_LIMIT_BYTES),
        name="outproj",
    )(out_a, out_b, x2d, wa, wb, norm_g.reshape(1, D_MODEL), rg_w, rg_b.reshape(1, -1), re_w,
      re_b.reshape(1, -1))


def _moe_kernel(h_ref, comb_ref, x1_ref, wg_ref, wu_ref, wd_ref, gf_ref, out_ref, acc_ref):
    e = pl.program_id(1)

    @pl.when(e == 0)
    def _():
        acc_ref[...] = jnp.zeros(acc_ref.shape, F32)

    h = h_ref[...]
    comb = comb_ref[...]
    lane_e = lax.broadcasted_iota(jnp.int32, comb.shape, 1)
    ce = jnp.sum(jnp.where(lane_e == e, comb, 0.0), axis=1, keepdims=True)
    a = jnp.dot(h, wg_ref[0], preferred_element_type=F32)
    b = jnp.dot(h, wu_ref[0], preferred_element_type=F32)
    act = (a * jax.nn.sigmoid(a) * b * ce).astype(BF16)
    acc_ref[...] += jnp.dot(act, wd_ref[0], preferred_element_type=F32)

    @pl.when(e == pl.num_programs(1) - 1)
    def _():
        y = x1_ref[...] + acc_ref[...]
        out_ref[...] = y * lax.rsqrt(jnp.mean(y * y, axis=-1, keepdims=True) + EPS) * gf_ref[...]


def _moe(h2, comb, x1, w_gate, w_up, w_down, norm_final_g):
    T = h2.shape[0]
    tm = MOE_TILE
    row = lambda n: pl.BlockSpec((tm, n), lambda i, e: (i, 0))
    return pl.pallas_call(
        _moe_kernel,
        grid=(T // tm, N_EXPERTS),
        in_specs=[row(D_MODEL), row(N_EXPERTS), row(D_MODEL),
                  pl.BlockSpec((1, D_MODEL, D_EXPERT), lambda i, e: (e, 0, 0)),
                  pl.BlockSpec((1, D_MODEL, D_EXPERT), lambda i, e: (e, 0, 0)),
                  pl.BlockSpec((1, D_EXPERT, D_MODEL), lambda i, e: (e, 0, 0)),
                  pl.BlockSpec((1, D_MODEL), lambda i, e: (0, 0))],
        out_specs=row(D_MODEL),
        out_shape=jax.ShapeDtypeStruct((T, D_MODEL), F32),
        scratch_shapes=[pltpu.VMEM((tm, D_MODEL), F32)],
        compiler_params=pltpu.CompilerParams(dimension_semantics=("parallel", "arbitrary"),
                                             vmem_limit_bytes=VMEM_LIMIT_BYTES),
        name="moe",
    )(h2, comb, x1, w_gate.astype(BF16), w_up.astype(BF16), w_down.astype(BF16),
      norm_final_g.reshape(1, D_MODEL))


def kernel(x, norm_mix_g, w_in, conv_w, conv_b, i_bias, f_bias, mlstm_norm_g, s5_lambda_re, s5_lambda_im,
           s5_log_dt, s5_b_re, s5_b_im, s5_c_re, s5_c_im, s5_d, s5_glu_w, s5_glu_b, w_out, norm_ffn_g,
           router_group_w, router_group_b, router_expert_w, router_expert_b, expert_w_gate, expert_w_up,
           expert_w_down, norm_final_g):
    batch, seq_len, d_model = x.shape
    assert d_model == D_MODEL and norm_mix_g.shape[0] == 1
    assert seq_len % MLSTM_CHUNK == 0 and seq_len % INPROJ_TILE == 0 and seq_len % S5_BLOCK == 0
    T = batch * seq_len
    x2d = x.reshape(T, D_MODEL)

    q, k, v, og, u, gcol, grow = _inproj(x2d, norm_mix_g[0], w_in[0], conv_w[0], conv_b[0], i_bias[0],
                                         f_bias[0], seq_len)
    out_a = _mlstm(q, k, v, og, gcol, grow, mlstm_norm_g[0], batch, seq_len)
    tables = _s5_tables(s5_lambda_re[0], s5_lambda_im[0], s5_log_dt[0], s5_b_re[0], s5_b_im[0], s5_c_re[0],
                        s5_c_im[0], s5_d[0], s5_glu_w[0], s5_glu_b[0], seq_len // S5_BLOCK)
    out_b = _s5(u, tables, batch, seq_len)
    x1, h2, comb = _outproj(out_a, out_b, x2d, w_out[0], norm_ffn_g[0], router_group_w[0], router_group_b[0],
                            router_expert_w[0], router_expert_b[0])
    y = _moe(h2, comb, x1, expert_w_gate[0], expert_w_up[0], expert_w_down[0], norm_final_g)
    return y.reshape(batch, seq_len, D_MODEL)
```

```python
import functools
import math

import jax
import jax.numpy as jnp
from jax import lax
from jax.experimental import pallas as pl
from jax.experimental.pallas import tpu as pltpu

F32 = jnp.float32
BF16 = jnp.bfloat16
HIGHEST = lax.Precision.HIGHEST

D_MODEL = 1024
D_MLSTM = 512
D_S5 = 512
HEADS = 4
HEAD_DIM = 128
CONV_WIDTH = 4
S5_GROUP_CH = 16
S5_GROUPS = 32
S5_STATE = 64
N_GROUPS = 4
PER_GROUP = 8
N_EXPERTS = 32
D_EXPERT = 512
EPS = 1e-6
LAMBDA_RE_MAX = -1e-4

VMEM_LIMIT_BYTES = 56 * 1024 * 1024

INPROJ_TILE = 512
MLSTM_CHUNK = 256
S5_BLOCK = 16
S5_PAIR = 2
OUTPROJ_TILE = 512
META_COLS = 8
DISPATCH_TILE = 512
FFN_TILE = 256
COMBINE_TILE = 256
ROW_DMA_UNROLL = 8


def _nt_dims():
    return (((1,), (1,)), ((), ()))


def _tn_dims():
    return (((0,), (0,)), ((), ()))


def _log_sigmoid(x):
    return jnp.minimum(x, 0.0) - jnp.log1p(jnp.exp(-jnp.abs(x)))


def _inproj_kernel(tiles_per_seq, x_ref, g_ref, wqk_ref, wv_ref, wo_ref, wu_ref, wg_ref, wgt_ref,
                   gbc_ref, gbr_ref, cw_ref, cb_ref,
                   q_ref, k_ref, v_ref, og_ref, u_ref, gcol_ref, grow_ref, ext_ref):
    i = pl.program_id(0)
    tm = x_ref.shape[0]
    x = x_ref[...]
    h = x * lax.rsqrt(jnp.mean(x * x, axis=-1, keepdims=True) + EPS) * g_ref[...]
    hb = h.astype(BF16)

    qk_raw = jnp.dot(hb, wqk_ref[...], preferred_element_type=F32)

    @pl.when(i % tiles_per_seq == 0)
    def _():
        ext_ref[0:8, :] = jnp.zeros((8, ext_ref.shape[1]), F32)

    ext_ref[8:, :] = qk_raw
    cw = cw_ref[...]
    acc = cb_ref[...] + cw[CONV_WIDTH - 1:CONV_WIDTH, :] * qk_raw
    for s in range(1, CONV_WIDTH):
        acc = acc + cw[CONV_WIDTH - 1 - s:CONV_WIDTH - s, :] * ext_ref[pl.ds(8 - s, tm), :]
    ext_ref[0:8, :] = qk_raw[tm - 8:, :]
    qk = acc * jax.nn.sigmoid(acc)
    q_ref[...] = qk[:, :D_MLSTM].astype(BF16)
    k_ref[...] = (qk[:, D_MLSTM:] * (HEAD_DIM ** -0.5)).astype(BF16)

    v_ref[...] = jnp.dot(hb, wv_ref[...], preferred_element_type=F32).astype(BF16)
    o_pre = jnp.dot(hb, wo_ref[...], preferred_element_type=F32)
    og_ref[...] = jax.nn.sigmoid(o_pre).astype(BF16)
    u_ref[...] = jnp.dot(hb, wu_ref[...], preferred_element_type=F32).astype(BF16)

    gcol_ref[...] = jnp.dot(h, wg_ref[...], precision=HIGHEST, preferred_element_type=F32) + gbc_ref[...]
    grow_ref[...] = lax.dot_general(wgt_ref[...], h, _nt_dims(), precision=HIGHEST,
                                    preferred_element_type=F32) + gbr_ref[...]


def _inproj(x2d, norm_g, w_in, conv_w, conv_b, i_bias, f_bias, seq_len):
    T = x2d.shape[0]
    tm = INPROJ_TILE
    dm = D_MLSTM
    wqk = w_in[:, :2 * dm].astype(BF16)
    wv = w_in[:, 2 * dm:3 * dm].astype(BF16)
    wo = w_in[:, 3 * dm:4 * dm].astype(BF16)
    wg = w_in[:, 4 * dm:4 * dm + 2 * HEADS]
    wu = w_in[:, 4 * dm + 2 * HEADS:].astype(BF16)
    gb = jnp.concatenate([i_bias, f_bias]).astype(F32)
    full = lambda shape: pl.BlockSpec(shape, lambda i: (0,) * len(shape))
    row = lambda n: pl.BlockSpec((tm, n), lambda i: (i, 0))
    return pl.pallas_call(
        functools.partial(_inproj_kernel, seq_len // tm),
        grid=(T // tm,),
        in_specs=[row(D_MODEL), full((1, D_MODEL)), full((D_MODEL, 2 * dm)), full((D_MODEL, dm)),
                  full((D_MODEL, dm)), full((D_MODEL, D_S5)), full((D_MODEL, 2 * HEADS)),
                  full((2 * HEADS, D_MODEL)), full((1, 2 * HEADS)), full((2 * HEADS, 1)),
                  full((CONV_WIDTH, 2 * dm)), full((1, 2 * dm))],
        out_specs=[row(dm), row(dm), row(dm), row(dm), row(D_S5), row(2 * HEADS),
                   pl.BlockSpec((2 * HEADS, tm), lambda i: (0, i))],
        out_shape=[jax.ShapeDtypeStruct((T, dm), BF16)] * 4
        + [jax.ShapeDtypeStruct((T, D_S5), BF16),
           jax.ShapeDtypeStruct((T, 2 * HEADS), F32),
           jax.ShapeDtypeStruct((2 * HEADS, T), F32)],
        scratch_shapes=[pltpu.VMEM((tm + 8, 2 * dm), F32)],
        compiler_params=pltpu.CompilerParams(dimension_semantics=("arbitrary",),
                                             vmem_limit_bytes=VMEM_LIMIT_BYTES),
        name="inproj",
    )(x2d, norm_g.reshape(1, D_MODEL), wqk, wv, wo, wu, wg, wg.T, gb.reshape(1, -1), gb.reshape(-1, 1),
      conv_w, conv_b.reshape(1, -1))


def _mlstm_kernel(q_ref, k_ref, v_ref, og_ref, gcol_ref, grow_ref, ng_ref, out_ref, c_ref, m_ref):
    tc = q_ref.shape[0]
    d = HEAD_DIM

    @pl.when(pl.program_id(1) == 0)
    def _():
        c_ref[...] = jnp.zeros(c_ref.shape, F32)
        m_ref[...] = jnp.zeros(m_ref.shape, F32)

    gcol = gcol_ref[...]
    grow = grow_ref[...]
    rix = lax.broadcasted_iota(jnp.int32, (tc, tc), 0)
    cix = lax.broadcasted_iota(jnp.int32, (tc, tc), 1)
    causal = rix >= cix
    b_col_all = jnp.dot(causal.astype(F32), _log_sigmoid(gcol), precision=HIGHEST,
                        preferred_element_type=F32)
    b_row_all = jnp.dot(_log_sigmoid(grow), (rix <= cix).astype(F32), precision=HIGHEST,
                        preferred_element_type=F32)
    ones_col = (lax.broadcasted_iota(jnp.int32, (tc, d), 1) == 0).astype(BF16)

    for h in range(HEADS):
        hs = slice(h * d, (h + 1) * d)
        li_col = gcol[:, h:h + 1]
        li_row = grow[h:h + 1, :]
        b_col = b_col_all[:, HEADS + h:HEADS + h + 1]
        b_row = b_row_all[HEADS + h:HEADS + h + 1, :]
        m_prev = m_ref[h:h + 1, 0:1]

        dmat = jnp.where(causal, b_col - b_row + li_row, -jnp.inf)
        inter = b_col + m_prev
        m_row = jnp.maximum(inter, jnp.max(dmat, axis=1, keepdims=True))
        w_intra = jnp.exp(dmat - m_row)
        w_inter = jnp.exp(inter - m_row)

        qh = q_ref[:, hs]
        kh = k_ref[:, hs]
        v_aug = jnp.concatenate([v_ref[:, hs], ones_col], axis=1)
        s = lax.dot_general(qh, kh, _nt_dims(), preferred_element_type=F32) * w_intra
        c_aug = c_ref[h]
        tot = (jnp.dot(s.astype(BF16), v_aug, preferred_element_type=F32)
               + w_inter * lax.dot_general(qh, c_aug.astype(BF16), _nt_dims(), preferred_element_type=F32))
        num = tot[:, :d]
        den = tot[:, d:d + 1]
        hh = num / jnp.maximum(jnp.abs(den), jnp.exp(-m_row))
        hh = hh * lax.rsqrt(jnp.mean(hh * hh, axis=-1, keepdims=True) + EPS)
        out_ref[:, hs] = (og_ref[:, hs].astype(F32) * (hh * ng_ref[:, hs])).astype(BF16)

        b_last = b_col[tc - 1:tc, :]
        d_state = b_last - b_col + li_col
        m_new = jnp.maximum(b_last + m_prev, jnp.max(d_state, axis=0, keepdims=True))
        w_s = jnp.exp(d_state - m_new)
        w_c = jnp.exp(b_last + m_prev - m_new)
        upd = lax.dot_general((v_aug.astype(F32) * w_s).astype(BF16), kh, _tn_dims(),
                              preferred_element_type=F32)
        c_ref[h] = w_c * c_aug + upd
        m_ref[h:h + 1, :] = jnp.broadcast_to(m_new, (1, m_ref.shape[1]))


def _mlstm(q, k, v, og, gcol, grow, norm_g, batch, seq_len):
    tc = MLSTM_CHUNK
    nc = seq_len // tc
    T = q.shape[0]
    row = lambda n: pl.BlockSpec((tc, n), lambda b, c: (b * nc + c, 0))
    return pl.pallas_call(
        _mlstm_kernel,
        grid=(batch, nc),
        in_specs=[row(D_MLSTM), row(D_MLSTM), row(D_MLSTM), row(D_MLSTM), row(2 * HEADS),
                  pl.BlockSpec((2 * HEADS, tc), lambda b, c: (0, b * nc + c)),
                  pl.BlockSpec((1, D_MLSTM), lambda b, c: (0, 0))],
        out_specs=row(D_MLSTM),
        out_shape=jax.ShapeDtypeStruct((T, D_MLSTM), BF16),
        scratch_shapes=[pltpu.VMEM((HEADS, 2 * HEAD_DIM, HEAD_DIM), F32), pltpu.VMEM((8, 128), F32)],
        compiler_params=pltpu.CompilerParams(dimension_semantics=("parallel", "arbitrary"),
                                             vmem_limit_bytes=VMEM_LIMIT_BYTES),
        name="mlstm",
    )(q, k, v, og, gcol, grow, norm_g.reshape(1, D_MLSTM))


def _s5_tables(lam_re, lam_im, log_dt, b_re, b_im, c_re, c_im, d_skip, glu_w, glu_b, n_blocks):
    G, N, C, P = S5_GROUPS, S5_STATE, S5_GROUP_CH, S5_BLOCK
    lr = jnp.minimum(lam_re.astype(F32), LAMBDA_RE_MAX)
    li = lam_im.astype(F32)
    dt = jnp.exp(log_dt.astype(F32))[:, None]
    mag = jnp.exp(lr * dt)
    ar = mag * jnp.cos(li * dt)
    ai = mag * jnp.sin(li * dt)
    den = lr * lr + li * li
    cr = ((ar - 1.0) * lr + ai * li) / den
    ci = (ai * lr - (ar - 1.0) * li) / den
    br, bi = b_re.astype(F32), b_im.astype(F32)
    bbr = cr[..., None] * br - ci[..., None] * bi
    bbi = cr[..., None] * bi + ci[..., None] * br
    ccr, cci = c_re.astype(F32), c_im.astype(F32)

    def powers(e):
        e = e.astype(F32)[:, None, None]
        m = jnp.exp(e * (lr * dt)[None])
        return m * jnp.cos(e * (li * dt)[None]), m * jnp.sin(e * (li * dt)[None])

    pr, pi = powers(jnp.arange(P + 1))
    kr = (jnp.einsum('gon,lgn,gni->lgoi', ccr, pr[:P], bbr, precision=HIGHEST)
          - jnp.einsum('gon,lgn,gni->lgoi', cci, pi[:P], bbr, precision=HIGHEST)
          - jnp.einsum('gon,lgn,gni->lgoi', ccr, pi[:P], bbi, precision=HIGHEST)
          - jnp.einsum('gon,lgn,gni->lgoi', cci, pr[:P], bbi, precision=HIGHEST))
    lag = jnp.arange(P)[None, :] - jnp.arange(P)[:, None]
    kt = jnp.where((lag >= 0)[:, :, None, None, None], kr[jnp.clip(lag, 0)], 0.0)
    ktoep = kt.transpose(2, 0, 4, 1, 3).reshape(G, P * C, P * C)
    er, ei = pr[:P][::-1], pi[:P][::-1]
    we_r = (jnp.einsum('pgn,gnc->gpcn', er, bbr) - jnp.einsum('pgn,gnc->gpcn', ei, bbi)).reshape(G, P * C, N)
    we_i = (jnp.einsum('pgn,gnc->gpcn', er, bbi) + jnp.einsum('pgn,gnc->gpcn', ei, bbr)).reshape(G, P * C, N)
    qr, qi = pr[1:], pi[1:]
    wc_r = (jnp.einsum('gon,pgn->gnpo', ccr, qr) - jnp.einsum('gon,pgn->gnpo', cci, qi)).reshape(G, N, P * C)
    wc_i = -(jnp.einsum('gon,pgn->gnpo', ccr, qi) + jnp.einsum('gon,pgn->gnpo', cci, qr)).reshape(G, N, P * C)
    eye_p = jnp.eye(P, dtype=F32)
    gw = glu_w.astype(F32)
    gw1 = jnp.einsum('pq,gio->gpiqo', eye_p, gw[..., :C]).reshape(G, P * C, P * C)
    gw2 = jnp.einsum('pq,gio->gpiqo', eye_p, gw[..., C:]).reshape(G, P * C, P * C)
    gb = glu_b.astype(F32)
    gb1 = jnp.tile(gb[:, None, :C], (1, P, 1)).reshape(G, P * C)
    gb2 = jnp.tile(gb[:, None, C:], (1, P, 1)).reshape(G, P * C)
    dsk = jnp.tile(d_skip.astype(F32)[:, None, :], (1, P, 1)).reshape(G, P * C)

    def bdiag(m):
        g, r, k = m.shape
        m = m.reshape(g // S5_PAIR, S5_PAIR, r, k)
        eye = jnp.eye(S5_PAIR, dtype=m.dtype)
        return jnp.einsum('jark,ab->jarbk', m, eye).reshape(g // S5_PAIR, S5_PAIR * r, S5_PAIR * k)

    def pair_vec(v):
        return v.reshape(G // S5_PAIR, 1, -1)

    we = jnp.concatenate([bdiag(we_r), bdiag(we_i)], axis=2).astype(BF16)
    wc = jnp.concatenate([bdiag(wc_r), bdiag(wc_i)], axis=1).astype(BF16)
    kt2 = bdiag(ktoep).astype(BF16)
    glu = jnp.concatenate([bdiag(gw1), bdiag(gw2)], axis=2).astype(BF16)
    glub = jnp.concatenate([pair_vec(gb1), pair_vec(gb2)], axis=2)
    dsk2 = pair_vec(dsk)
    n_steps = max(1, (n_blocks - 1).bit_length())
    sr, si = powers(P * (2 ** jnp.arange(n_steps)))
    sr = sr.transpose(1, 0, 2).reshape(G // S5_PAIR, S5_PAIR, n_steps, N).transpose(0, 2, 1, 3)
    si = si.transpose(1, 0, 2).reshape(G // S5_PAIR, S5_PAIR, n_steps, N).transpose(0, 2, 1, 3)
    sr = sr.reshape(G // S5_PAIR, n_steps, S5_PAIR * N)
    si = si.reshape(G // S5_PAIR, n_steps, S5_PAIR * N)
    return we, kt2, wc, glu, glub, dsk2, sr, si, n_steps


def _s5_kernel(n_steps, u_ref, we_ref, kt_ref, wc_ref, glu_ref, glub_ref, dsk_ref, sr_ref, si_ref, out_ref):
    u = u_ref[...]
    nb = u.shape[0]
    half = S5_PAIR * S5_STATE
    e = jnp.dot(u, we_ref[0], preferred_element_type=F32)
    xr, xi = e[:, :half], e[:, half:]
    row = lax.broadcasted_iota(jnp.int32, (nb, half), 0)
    for k in range(n_steps):
        sh = 1 << k
        ar = sr_ref[0, k:k + 1, :]
        ai = si_ref[0, k:k + 1, :]
        valid = row >= sh
        yr = jnp.where(valid, pltpu.roll(xr, sh, axis=0), 0.0)
        yi = jnp.where(valid, pltpu.roll(xi, sh, axis=0), 0.0)
        xr, xi = xr + ar * yr - ai * yi, xi + ar * yi + ai * yr
    first = row >= 1
    pr = jnp.where(first, pltpu.roll(xr, 1, axis=0), 0.0)
    pi = jnp.where(first, pltpu.roll(xi, 1, axis=0), 0.0)
    sprev = jnp.concatenate([pr, pi], axis=1).astype(BF16)
    y = (jnp.dot(u, kt_ref[0], preferred_element_type=F32)
         + jnp.dot(sprev, wc_ref[0], preferred_element_type=F32)
         + dsk_ref[0] * u.astype(F32))
    y = jax.nn.gelu(y)
    z = jnp.dot(y.astype(BF16), glu_ref[0], preferred_element_type=F32) + glub_ref[0]
    w = z.shape[1] // 2
    out_ref[...] = (z[:, :w] * jax.nn.sigmoid(z[:, w:])).astype(BF16)


def _s5(u, tables, batch, seq_len):
    we, kt2, wc, glu, glub, dsk2, sr, si, n_steps = tables
    T = u.shape[0]
    P, C, G = S5_BLOCK, S5_GROUP_CH, S5_GROUPS
    nb = seq_len // P
    lanes = S5_PAIR * P * C
    u_blk = u.reshape(T // P, P, G, C).transpose(0, 2, 1, 3).reshape(T // P, G * P * C)
    per_pair = lambda a: pl.BlockSpec((1,) + a.shape[1:], lambda b, j: (j, 0, 0))
    out_blk = pl.pallas_call(
        functools.partial(_s5_kernel, n_steps),
        grid=(batch, G // S5_PAIR),
        in_specs=[pl.BlockSpec((nb, lanes), lambda b, j: (b, j)),
                  per_pair(we), per_pair(kt2), per_pair(wc), per_pair(glu), per_pair(glub),
                  per_pair(dsk2), per_pair(sr), per_pair(si)],
        out_specs=pl.BlockSpec((nb, lanes), lambda b, j: (b, j)),
        out_shape=jax.ShapeDtypeStruct((T // P, G * P * C), BF16),
        compiler_params=pltpu.CompilerParams(dimension_semantics=("parallel", "parallel"),
                                             vmem_limit_bytes=VMEM_LIMIT_BYTES),
        name="s5",
    )(u_blk, we, kt2, wc, glu, glub, dsk2, sr, si)
    return out_blk.reshape(T // P, G, P, C).transpose(0, 2, 1, 3).reshape(T, G * C)


def _outproj_kernel(a_ref, b_ref, x_ref, wa_ref, wb_ref, g_ref, rgw_ref, rgb_ref, rew_ref, reb_ref,
                    x1_ref, h2_ref, meta_ref, cnt_ref):
    @pl.when(pl.program_id(0) == 0)
    def _():
        cnt_ref[...] = jnp.zeros(cnt_ref.shape, F32)

    x1 = (x_ref[...]
          + jnp.dot(a_ref[...], wa_ref[...], preferred_element_type=F32)
          + jnp.dot(b_ref[...], wb_ref[...], preferred_element_type=F32))
    x1_ref[...] = x1
    h2 = x1 * lax.rsqrt(jnp.mean(x1 * x1, axis=-1, keepdims=True) + EPS) * g_ref[...]
    h2_ref[...] = h2

    gl = jnp.dot(h2, rgw_ref[...], precision=HIGHEST, preferred_element_type=F32) + rgb_ref[...]
    el = jnp.dot(h2, rew_ref[...], precision=HIGHEST, preferred_element_type=F32) + reb_ref[...]
    tm = gl.shape[0]
    gmax = jnp.max(gl, axis=1, keepdims=True)
    g_w = 1.0 / jnp.sum(jnp.exp(gl - gmax), axis=1, keepdims=True)
    lane_g = lax.broadcasted_iota(jnp.int32, (tm, N_GROUPS), 1)
    g_idx = jnp.min(jnp.where(gl == gmax, lane_g, N_GROUPS), axis=1, keepdims=True)
    lane_e = lax.broadcasted_iota(jnp.int32, (tm, N_EXPERTS), 1)
    elm = jnp.where(lane_e // PER_GROUP == g_idx, el, -jnp.inf)
    m1 = jnp.max(elm, axis=1, keepdims=True)
    i1 = jnp.min(jnp.where(elm == m1, lane_e, N_EXPERTS), axis=1, keepdims=True)
    elm2 = jnp.where(lane_e == i1, -jnp.inf, elm)
    m2 = jnp.max(elm2, axis=1, keepdims=True)
    i2 = jnp.min(jnp.where(elm2 == m2, lane_e, N_EXPERTS), axis=1, keepdims=True)
    e2 = jnp.exp(m2 - m1)
    w1 = g_w / (1.0 + e2)
    w2 = g_w * e2 / (1.0 + e2)

    sel1 = lane_e == i1
    sel2 = lane_e == i2
    picked = (sel1 | sel2).astype(BF16)
    rix = lax.broadcasted_iota(jnp.int32, (tm, tm), 0)
    cix = lax.broadcasted_iota(jnp.int32, (tm, tm), 1)
    before = jnp.dot((rix > cix).astype(BF16), picked, preferred_element_type=F32) + cnt_ref[...]
    r1 = jnp.sum(jnp.where(sel1, before, 0.0), axis=1, keepdims=True)
    r2 = jnp.sum(jnp.where(sel2, before, 0.0), axis=1, keepdims=True)
    cnt_ref[...] += jnp.sum(picked.astype(F32), axis=0, keepdims=True)
    col = lax.broadcasted_iota(jnp.int32, (tm, META_COLS), 1)
    meta = jnp.zeros((tm, META_COLS), F32)
    for c, val in enumerate((i1.astype(F32), i2.astype(F32), r1, r2, w1, w2)):
        meta = jnp.where(col == c, val, meta)
    meta_ref[...] = meta


def _outproj(out_a, out_b, x2d, w_out, norm_g, rg_w, rg_b, re_w, re_b):
    T = x2d.shape[0]
    tm = OUTPROJ_TILE
    wa = w_out[:D_MLSTM].astype(BF16)
    wb = w_out[D_MLSTM:].astype(BF16)
    full = lambda shape: pl.BlockSpec(shape, lambda i: (0,) * len(shape))
    row = lambda n: pl.BlockSpec((tm, n), lambda i: (i, 0))
    return pl.pallas_call(
        _outproj_kernel,
        grid=(T // tm,),
        in_specs=[row(D_MLSTM), row(D_S5), row(D_MODEL), full((D_MLSTM, D_MODEL)), full((D_S5, D_MODEL)),
                  full((1, D_MODEL)), full((D_MODEL, N_GROUPS)), full((1, N_GROUPS)),
                  full((D_MODEL, N_EXPERTS)), full((1, N_EXPERTS))],
        out_specs=[row(D_MODEL), row(D_MODEL), row(META_COLS), full((1, N_EXPERTS))],
        out_shape=[jax.ShapeDtypeStruct((T, D_MODEL), F32), jax.ShapeDtypeStruct((T, D_MODEL), F32),
                   jax.ShapeDtypeStruct((T, META_COLS), F32), jax.ShapeDtypeStruct((1, N_EXPERTS), F32)],
        compiler_params=pltpu.CompilerParams(dimension_semantics=("arbitrary",),
                                             vmem_limit_bytes=VMEM_LIMIT_BYTES),
        name="outproj",
    )(out_a, out_b, x2d, wa, wb, norm_g.reshape(1, D_MODEL), rg_w, rg_b.reshape(1, -1), re_w,
      re_b.reshape(1, -1))


def _row_copy(src_ref, src_row, dst_ref, dst_row, sem):
    return pltpu.make_async_copy(src_ref.at[pl.ds(src_row, 1), :], dst_ref.at[pl.ds(dst_row, 1), :], sem)


def _pad_copy(zero_ref, xs_ref, row, sem):
    return pltpu.make_async_copy(zero_ref, xs_ref.at[pl.ds(row, FFN_TILE), :], sem)


def _dispatch_kernel(pad_ref, p1_ref, p2_ref, h_ref, xs_ref, zero_ref, sem, zsem):
    tm = h_ref.shape[0]
    n_tiles = xs_ref.shape[0] // FFN_TILE

    @pl.when(pl.program_id(0) == 0)
    def _():
        zero_ref[...] = jnp.zeros(zero_ref.shape, F32)
        for e in range(N_EXPERTS):
            _pad_copy(zero_ref, xs_ref, pl.multiple_of(pad_ref[e], 8), zsem).start()
        for e in range(N_EXPERTS):
            _pad_copy(zero_ref, xs_ref, 0, zsem).wait()
        first_free = pad_ref[N_EXPERTS]

        def start_tail(k, carry):
            _pad_copy(zero_ref, xs_ref, pl.multiple_of(k * FFN_TILE, FFN_TILE), zsem).start()
            return carry

        def wait_tail(k, carry):
            _pad_copy(zero_ref, xs_ref, 0, zsem).wait()
            return carry

        lax.fori_loop(first_free, n_tiles, start_tail, 0)
        lax.fori_loop(first_free, n_tiles, wait_tail, 0)

    def issue(t, carry):
        _row_copy(h_ref, t, xs_ref, p1_ref[0, 0, t], sem).start()
        _row_copy(h_ref, t, xs_ref, p2_ref[0, 0, t], sem).start()
        return carry

    lax.fori_loop(0, tm, issue, 0, unroll=ROW_DMA_UNROLL)

    def drain(t, carry):
        _row_copy(h_ref, 0, xs_ref, 0, sem).wait()
        _row_copy(h_ref, 0, xs_ref, 0, sem).wait()
        return carry

    lax.fori_loop(0, tm, drain, 0, unroll=ROW_DMA_UNROLL)


def _dispatch(h2, pos1, pos2, pad_start, n_rows):
    T, w = h2.shape
    tm = DISPATCH_TILE
    smem_row = pl.BlockSpec((1, 1, tm), lambda i, pad: (i, 0, 0), memory_space=pltpu.SMEM)
    return pl.pallas_call(
        _dispatch_kernel,
        grid_spec=pltpu.PrefetchScalarGridSpec(
            num_scalar_prefetch=1,
            grid=(T // tm,),
            in_specs=[smem_row, smem_row, pl.BlockSpec((tm, w), lambda i, pad: (i, 0))],
            out_specs=pl.BlockSpec(memory_space=pl.ANY),
            scratch_shapes=[pltpu.VMEM((FFN_TILE, w), F32), pltpu.SemaphoreType.DMA, pltpu.SemaphoreType.DMA]),
        out_shape=jax.ShapeDtypeStruct((n_rows + FFN_TILE, w), F32),
        compiler_params=pltpu.CompilerParams(dimension_semantics=("arbitrary",),
                                             vmem_limit_bytes=VMEM_LIMIT_BYTES),
        name="dispatch",
    )(pad_start, pos1.reshape(T // tm, 1, tm), pos2.reshape(T // tm, 1, tm), h2)


def _ffn_kernel(te_ref, nused_ref, xs_ref, wg_ref, wu_ref, wd_ref, ys_ref, wgb_ref, wub_ref, wdb_ref):
    i = pl.program_id(0)
    used = i < nused_ref[0]

    @pl.when(used)
    def _():
        @pl.when((i == 0) | (te_ref[i] != te_ref[jnp.maximum(i - 1, 0)]))
        def _():
            wgb_ref[...] = wg_ref[0].astype(BF16)
            wub_ref[...] = wu_ref[0].astype(BF16)
            wdb_ref[...] = wd_ref[0].astype(BF16)

        x = xs_ref[...].astype(BF16)
        a = jnp.dot(x, wgb_ref[...], preferred_element_type=F32)
        b = jnp.dot(x, wub_ref[...], preferred_element_type=F32)
        act = (a * jax.nn.sigmoid(a) * b).astype(BF16)
        ys_ref[...] = jnp.dot(act, wdb_ref[...], preferred_element_type=F32)

    @pl.when(jnp.logical_not(used))
    def _():
        ys_ref[...] = jnp.zeros(ys_ref.shape, F32)


def _ffn(xs, tile_expert, n_used, n_rows, w_gate, w_up, w_down):
    w = xs.shape[1]
    tm = FFN_TILE
    wspec = lambda shape: pl.BlockSpec((1,) + shape, lambda i, te, nu: (te[i], 0, 0))
    return pl.pallas_call(
        _ffn_kernel,
        grid_spec=pltpu.PrefetchScalarGridSpec(
            num_scalar_prefetch=2,
            grid=(n_rows // tm,),
            in_specs=[pl.BlockSpec((tm, w), lambda i, te, nu: (jnp.minimum(i, nu[0] - 1), 0)),
                      wspec((D_MODEL, D_EXPERT)), wspec((D_MODEL, D_EXPERT)), wspec((D_EXPERT, D_MODEL))],
            out_specs=pl.BlockSpec((tm, w), lambda i, te, nu: (i, 0)),
            scratch_shapes=[pltpu.VMEM((D_MODEL, D_EXPERT), BF16), pltpu.VMEM((D_MODEL, D_EXPERT), BF16),
                            pltpu.VMEM((D_EXPERT, D_MODEL), BF16)]),
        out_shape=jax.ShapeDtypeStruct((n_rows, w), F32),
        compiler_params=pltpu.CompilerParams(dimension_semantics=("arbitrary",),
                                             vmem_limit_bytes=VMEM_LIMIT_BYTES),
        name="ffn",
    )(tile_expert, n_used, xs, w_gate, w_up, w_down)


def _combine_kernel(p1_ref, p2_ref, ys_ref, meta_ref, x1_ref, gf_ref, out_ref, buf_ref, sem):
    tm = x1_ref.shape[0]

    def issue(t, carry):
        _row_copy(ys_ref, p1_ref[0, 0, t], buf_ref.at[0], t, sem).start()
        _row_copy(ys_ref, p2_ref[0, 0, t], buf_ref.at[1], t, sem).start()
        return carry

    lax.fori_loop(0, tm, issue, 0, unroll=ROW_DMA_UNROLL)

    def drain(t, carry):
        _row_copy(ys_ref, 0, buf_ref.at[0], 0, sem).wait()
        _row_copy(ys_ref, 0, buf_ref.at[1], 0, sem).wait()
        return carry

    lax.fori_loop(0, tm, drain, 0, unroll=ROW_DMA_UNROLL)

    meta = meta_ref[...]
    y = x1_ref[...] + meta[:, 4:5] * buf_ref[0] + meta[:, 5:6] * buf_ref[1]
    out_ref[...] = y * lax.rsqrt(jnp.mean(y * y, axis=-1, keepdims=True) + EPS) * gf_ref[...]


def _combine(ys, pos1, pos2, meta, x1, norm_final_g):
    T = x1.shape[0]
    w = ys.shape[1]
    tm = COMBINE_TILE
    smem_row = pl.BlockSpec((1, 1, tm), lambda i: (i, 0, 0), memory_space=pltpu.SMEM)
    row = lambda n: pl.BlockSpec((tm, n), lambda i: (i, 0))
    return pl.pallas_call(
        _combine_kernel,
        grid=(T // tm,),
        in_specs=[smem_row, smem_row, pl.BlockSpec(memory_space=pl.ANY), row(META_COLS), row(D_MODEL),
                  pl.BlockSpec((1, D_MODEL), lambda i: (0, 0))],
        out_specs=row(D_MODEL),
        out_shape=jax.ShapeDtypeStruct((T, D_MODEL), F32),
        scratch_shapes=[pltpu.VMEM((2, tm, w), F32), pltpu.SemaphoreType.DMA],
        compiler_params=pltpu.CompilerParams(dimension_semantics=("arbitrary",),
                                             vmem_limit_bytes=VMEM_LIMIT_BYTES),
        name="combine",
    )(pos1.reshape(T // tm, 1, tm), pos2.reshape(T // tm, 1, tm), ys, meta, x1,
      norm_final_g.reshape(1, D_MODEL))


def _route_plan(meta, counts, n_tiles):
    cnt = counts.reshape(-1).astype(jnp.int32)
    padded = (cnt + FFN_TILE - 1) // FFN_TILE * FFN_TILE
    ends = jnp.cumsum(padded)
    base = ends - padded
    e1 = meta[:, 0].astype(jnp.int32)
    e2 = meta[:, 1].astype(jnp.int32)
    pos1 = base[e1] + meta[:, 2].astype(jnp.int32)
    pos2 = base[e2] + meta[:, 3].astype(jnp.int32)
    tile_start = jnp.arange(n_tiles, dtype=jnp.int32) * FFN_TILE
    tile_expert = jnp.sum((tile_start[:, None] >= ends[None, :]).astype(jnp.int32), axis=1)
    tile_expert = jnp.minimum(tile_expert, N_EXPERTS - 1)
    n_used = (ends[-1:] // FFN_TILE).astype(jnp.int32)
    pad_start = jnp.concatenate([base + cnt // 8 * 8, n_used])
    return pos1, pos2, pad_start, tile_expert, n_used


def kernel(x, norm_mix_g, w_in, conv_w, conv_b, i_bias, f_bias, mlstm_norm_g, s5_lambda_re, s5_lambda_im,
           s5_log_dt, s5_b_re, s5_b_im, s5_c_re, s5_c_im, s5_d, s5_glu_w, s5_glu_b, w_out, norm_ffn_g,
           router_group_w, router_group_b, router_expert_w, router_expert_b, expert_w_gate, expert_w_up,
           expert_w_down, norm_final_g):
    batch, seq_len, d_model = x.shape
    assert d_model == D_MODEL and norm_mix_g.shape[0] == 1
    assert seq_len % MLSTM_CHUNK == 0 and seq_len % INPROJ_TILE == 0 and seq_len % S5_BLOCK == 0
    T = batch * seq_len
    x2d = x.reshape(T, D_MODEL)

    q, k, v, og, u, gcol, grow = _inproj(x2d, norm_mix_g[0], w_in[0], conv_w[0], conv_b[0], i_bias[0],
                                         f_bias[0], seq_len)
    out_a = _mlstm(q, k, v, og, gcol, grow, mlstm_norm_g[0], batch, seq_len)
    tables = _s5_tables(s5_lambda_re[0], s5_lambda_im[0], s5_log_dt[0], s5_b_re[0], s5_b_im[0], s5_c_re[0],
                        s5_c_im[0], s5_d[0], s5_glu_w[0], s5_glu_b[0], seq_len // S5_BLOCK)
    out_b = _s5(u, tables, batch, seq_len)
    x1, h2, meta, counts = _outproj(out_a, out_b, x2d, w_out[0], norm_ffn_g[0], router_group_w[0],
                                    router_group_b[0], router_expert_w[0], router_expert_b[0])
    n_rows = 2 * T + N_EXPERTS * FFN_TILE
    pos1, pos2, pad_start, tile_expert, n_used = _route_plan(meta, counts, n_rows // FFN_TILE)
    xs = _dispatch(h2, pos1, pos2, pad_start, n_rows)
    ys = _ffn(xs, tile_expert, n_used, n_rows, expert_w_gate[0], expert_w_up[0], expert_w_down[0])
    y = _combine(ys, pos1, pos2, meta, x1, norm_final_g)
    return y.reshape(batch, seq_len, D_MODEL)
```

```python
import functools

import jax
import jax.numpy as jnp
from jax import lax
from jax.experimental import pallas as pl
from jax.experimental.pallas import tpu as pltpu

F32 = jnp.float32
BF16 = jnp.bfloat16
HIGHEST = lax.Precision.HIGHEST

D_MODEL = 1024
D_MLSTM = 512
D_S5 = 512
HEADS = 4
HEAD_DIM = 128
CONV_WIDTH = 4
S5_GROUP_CH = 16
S5_GROUPS = 32
S5_STATE = 64
N_GROUPS = 4
PER_GROUP = 8
N_EXPERTS = 32
D_EXPERT = 512
EPS = 1e-6
LAMBDA_RE_MAX = -1e-4

VMEM_LIMIT_BYTES = 56 * 1024 * 1024
LANES = 128

INPROJ_TILE = 512
MLSTM_CHUNK = 256
S5_BLOCK = 16
S5_PAIR = 2
S5_STEP_GROUPS = LANES // S5_GROUP_CH
S5_STEP_PAIRS = S5_STEP_GROUPS // S5_PAIR
S5_SCAN_ROWS = 16
OUTPROJ_TILE = 512
META_COLS = 8
PLAN_TILE = 2048
DISPATCH_TILE = 512
FFN_TILE = 256
COMBINE_TILE = 256
ROW_DMA_UNROLL = 8


def _nt_dims():
    return (((1,), (1,)), ((), ()))


def _tn_dims():
    return (((0,), (0,)), ((), ()))


def _log_sigmoid(x):
    return jnp.minimum(x, 0.0) - jnp.log1p(jnp.exp(-jnp.abs(x)))


def _inproj_kernel(tiles_per_seq, x_ref, g_ref, wqk_ref, wv_ref, wo_ref, wu_ref, wg_ref, wgt_ref,
                   gbc_ref, gbr_ref, cw_ref, cb_ref,
                   q_ref, k_ref, v_ref, og_ref, u_ref, gcol_ref, grow_ref, ext_ref, us_ref):
    i = pl.program_id(0)
    tm = x_ref.shape[0]
    x = x_ref[...]
    h = x * lax.rsqrt(jnp.mean(x * x, axis=-1, keepdims=True) + EPS) * g_ref[...]
    hb = h.astype(BF16)

    qk_raw = jnp.dot(hb, wqk_ref[...], preferred_element_type=F32)

    @pl.when(i % tiles_per_seq == 0)
    def _():
        ext_ref[0:8, :] = jnp.zeros((8, ext_ref.shape[1]), F32)

    ext_ref[8:, :] = qk_raw
    cw = cw_ref[...]
    acc = cb_ref[...] + cw[CONV_WIDTH - 1:CONV_WIDTH, :] * qk_raw
    for s in range(1, CONV_WIDTH):
        acc = acc + cw[CONV_WIDTH - 1 - s:CONV_WIDTH - s, :] * ext_ref[pl.ds(8 - s, tm), :]
    ext_ref[0:8, :] = qk_raw[tm - 8:, :]
    qk = acc * jax.nn.sigmoid(acc)
    q_ref[...] = qk[:, :D_MLSTM].astype(BF16)
    k_ref[...] = (qk[:, D_MLSTM:] * (HEAD_DIM ** -0.5)).astype(BF16)

    v_ref[...] = jnp.dot(hb, wv_ref[...], preferred_element_type=F32).astype(BF16)
    o_pre = jnp.dot(hb, wo_ref[...], preferred_element_type=F32)
    og_ref[...] = jax.nn.sigmoid(o_pre).astype(BF16)

    u = jnp.dot(hb, wu_ref[...], preferred_element_type=F32)
    n_slabs = us_ref.shape[0]
    for j in range(n_slabs):
        us_ref[j] = u[:, j * LANES:(j + 1) * LANES]
    for p in range(S5_BLOCK):
        u_ref[p] = jnp.concatenate([us_ref[j, pl.ds(p, tm // S5_BLOCK, stride=S5_BLOCK), :]
                                    for j in range(n_slabs)], axis=1).astype(BF16)

    gcol_ref[...] = jnp.dot(h, wg_ref[...], precision=HIGHEST, preferred_element_type=F32) + gbc_ref[...]
    grow_ref[...] = lax.dot_general(wgt_ref[...], h, _nt_dims(), precision=HIGHEST,
                                    preferred_element_type=F32) + gbr_ref[...]


def _inproj(x2d, norm_g, w_in, conv_w, conv_b, i_bias, f_bias, seq_len):
    T = x2d.shape[0]
    tm = INPROJ_TILE
    dm = D_MLSTM
    wqk = w_in[:, :2 * dm].astype(BF16)
    wv = w_in[:, 2 * dm:3 * dm].astype(BF16)
    wo = w_in[:, 3 * dm:4 * dm].astype(BF16)
    wg = w_in[:, 4 * dm:4 * dm + 2 * HEADS]
    wu = w_in[:, 4 * dm + 2 * HEADS:].astype(BF16)
    gb = jnp.concatenate([i_bias, f_bias]).astype(F32)
    full = lambda shape: pl.BlockSpec(shape, lambda i: (0,) * len(shape))
    row = lambda n: pl.BlockSpec((tm, n), lambda i: (i, 0))
    return pl.pallas_call(
        functools.partial(_inproj_kernel, seq_len // tm),
        grid=(T // tm,),
        in_specs=[row(D_MODEL), full((1, D_MODEL)), full((D_MODEL, 2 * dm)), full((D_MODEL, dm)),
                  full((D_MODEL, dm)), full((D_MODEL, D_S5)), full((D_MODEL, 2 * HEADS)),
                  full((2 * HEADS, D_MODEL)), full((1, 2 * HEADS)), full((2 * HEADS, 1)),
                  full((CONV_WIDTH, 2 * dm)), full((1, 2 * dm))],
        out_specs=[row(dm), row(dm), row(dm), row(dm),
                   pl.BlockSpec((S5_BLOCK, tm // S5_BLOCK, D_S5), lambda i: (0, i, 0)), row(2 * HEADS),
                   pl.BlockSpec((2 * HEADS, tm), lambda i: (0, i))],
        out_shape=[jax.ShapeDtypeStruct((T, dm), BF16)] * 4
        + [jax.ShapeDtypeStruct((S5_BLOCK, T // S5_BLOCK, D_S5), BF16),
           jax.ShapeDtypeStruct((T, 2 * HEADS), F32),
           jax.ShapeDtypeStruct((2 * HEADS, T), F32)],
        scratch_shapes=[pltpu.VMEM((tm + 8, 2 * dm), F32), pltpu.VMEM((D_S5 // LANES, tm, LANES), F32)],
        compiler_params=pltpu.CompilerParams(dimension_semantics=("arbitrary",),
                                             vmem_limit_bytes=VMEM_LIMIT_BYTES),
        name="inproj",
    )(x2d, norm_g.reshape(1, D_MODEL), wqk, wv, wo, wu, wg, wg.T, gb.reshape(1, -1), gb.reshape(-1, 1),
      conv_w, conv_b.reshape(1, -1))


def _mlstm_kernel(q_ref, k_ref, v_ref, og_ref, gcol_ref, grow_ref, ng_ref, out_ref, c_ref, m_ref):
    tc = q_ref.shape[0]
    d = HEAD_DIM

    @pl.when(pl.program_id(1) == 0)
    def _():
        c_ref[...] = jnp.zeros(c_ref.shape, F32)
        m_ref[...] = jnp.zeros(m_ref.shape, F32)

    gcol = gcol_ref[...]
    grow = grow_ref[...]
    rix = lax.broadcasted_iota(jnp.int32, (tc, tc), 0)
    cix = lax.broadcasted_iota(jnp.int32, (tc, tc), 1)
    causal = rix >= cix
    b_col_all = jnp.dot(causal.astype(F32), _log_sigmoid(gcol), precision=HIGHEST,
                        preferred_element_type=F32)
    b_row_all = jnp.dot(_log_sigmoid(grow), (rix <= cix).astype(F32), precision=HIGHEST,
                        preferred_element_type=F32)
    ones_col = (lax.broadcasted_iota(jnp.int32, (tc, d), 1) == 0).astype(BF16)

    for h in range(HEADS):
        hs = slice(h * d, (h + 1) * d)
        li_col = gcol[:, h:h + 1]
        li_row = grow[h:h + 1, :]
        b_col = b_col_all[:, HEADS + h:HEADS + h + 1]
        b_row = b_row_all[HEADS + h:HEADS + h + 1, :]
        m_prev = m_ref[h:h + 1, 0:1]

        dmat = jnp.where(causal, b_col - b_row + li_row, -jnp.inf)
        inter = b_col + m_prev
        m_row = jnp.maximum(inter, jnp.max(dmat, axis=1, keepdims=True))
        w_intra = jnp.exp(dmat - m_row)
        w_inter = jnp.exp(inter - m_row)

        qh = q_ref[:, hs]
        kh = k_ref[:, hs]
        v_aug = jnp.concatenate([v_ref[:, hs], ones_col], axis=1)
        s = lax.dot_general(qh, kh, _nt_dims(), preferred_element_type=F32) * w_intra
        c_aug = c_ref[h]
        tot = (jnp.dot(s.astype(BF16), v_aug, preferred_element_type=F32)
               + w_inter * lax.dot_general(qh, c_aug.astype(BF16), _nt_dims(), preferred_element_type=F32))
        num = tot[:, :d]
        den = tot[:, d:d + 1]
        hh = num / jnp.maximum(jnp.abs(den), jnp.exp(-m_row))
        hh = hh * lax.rsqrt(jnp.mean(hh * hh, axis=-1, keepdims=True) + EPS)
        out_ref[:, hs] = (og_ref[:, hs].astype(F32) * (hh * ng_ref[:, hs])).astype(BF16)

        b_last = b_col[tc - 1:tc, :]
        d_state = b_last - b_col + li_col
        m_new = jnp.maximum(b_last + m_prev, jnp.max(d_state, axis=0, keepdims=True))
        w_s = jnp.exp(d_state - m_new)
        w_c = jnp.exp(b_last + m_prev - m_new)
        upd = lax.dot_general((v_aug.astype(F32) * w_s).astype(BF16), kh, _tn_dims(),
                              preferred_element_type=F32)
        c_ref[h] = w_c * c_aug + upd
        m_ref[h:h + 1, :] = jnp.broadcast_to(m_new, (1, m_ref.shape[1]))


def _mlstm(q, k, v, og, gcol, grow, norm_g, batch, seq_len):
    tc = MLSTM_CHUNK
    nc = seq_len // tc
    T = q.shape[0]
    row = lambda n: pl.BlockSpec((tc, n), lambda b, c: (b * nc + c, 0))
    return pl.pallas_call(
        _mlstm_kernel,
        grid=(batch, nc),
        in_specs=[row(D_MLSTM), row(D_MLSTM), row(D_MLSTM), row(D_MLSTM), row(2 * HEADS),
                  pl.BlockSpec((2 * HEADS, tc), lambda b, c: (0, b * nc + c)),
                  pl.BlockSpec((1, D_MLSTM), lambda b, c: (0, 0))],
        out_specs=row(D_MLSTM),
        out_shape=jax.ShapeDtypeStruct((T, D_MLSTM), BF16),
        scratch_shapes=[pltpu.VMEM((HEADS, 2 * HEAD_DIM, HEAD_DIM), F32), pltpu.VMEM((8, LANES), F32)],
        compiler_params=pltpu.CompilerParams(dimension_semantics=("parallel", "arbitrary"),
                                             vmem_limit_bytes=VMEM_LIMIT_BYTES),
        name="mlstm",
    )(q, k, v, og, gcol, grow, norm_g.reshape(1, D_MLSTM))


def _cplx_pow(exponent, theta_re, theta_im):
    mag = jnp.exp(exponent * theta_re)
    return mag * jnp.cos(exponent * theta_im), mag * jnp.sin(exponent * theta_im)


def _s5_tables_kernel(lrr_ref, lir_ref, dtr_ref, lrc_ref, lic_ref, dtc_ref, btr_ref, bti_ref, ctr_ref, cti_ref,
                      dcol_ref, gw1_ref, gw2_ref, gbcol_ref,
                      we_ref, kt_ref, wc_ref, glu_ref, glub_ref, dsk_ref, sr_ref, si_ref):
    P, C, N = S5_BLOCK, S5_GROUP_CH, S5_STATE
    PC, NN = P * C, S5_PAIR * S5_STATE
    shift_c, shift_n = C.bit_length() - 1, N.bit_length() - 1

    lr = jnp.minimum(lrr_ref[0], LAMBDA_RE_MAX)
    li = lir_ref[0]
    dt = jnp.exp(dtr_ref[0])
    th_r, th_i = lr * dt, li * dt
    ar, ai = _cplx_pow(1.0, th_r, th_i)
    den = lr * lr + li * li
    cr = ((ar - 1.0) * lr + ai * li) / den
    ci = (ai * lr - (ar - 1.0) * li) / den
    btr, bti = btr_ref[0], bti_ref[0]
    bbr = cr * btr - ci * bti
    bbi = cr * bti + ci * btr

    back = (P - 1 - lax.broadcasted_iota(jnp.int32, (P, 1), 0)).astype(F32)
    pwr, pwi = _cplx_pow(back, th_r, th_i)
    val_r = jnp.concatenate([pwr[p:p + 1] * bbr - pwi[p:p + 1] * bbi for p in range(P)], axis=0)
    val_i = jnp.concatenate([pwr[p:p + 1] * bbi + pwi[p:p + 1] * bbr for p in range(P)], axis=0)
    lane_grp = lax.broadcasted_iota(jnp.int32, (PC, NN), 1) >> shift_n
    we_r = jnp.concatenate([jnp.where(lane_grp == g, val_r, 0.0) for g in range(S5_PAIR)], axis=0)
    we_i = jnp.concatenate([jnp.where(lane_grp == g, val_i, 0.0) for g in range(S5_PAIR)], axis=0)
    we_ref[0] = jnp.concatenate([we_r, we_i], axis=1).astype(BF16)

    steps = (P << lax.broadcasted_iota(jnp.int32, (S5_SCAN_ROWS, 1), 0)).astype(F32)
    sr_ref[0], si_ref[0] = _cplx_pow(steps, th_r, th_i)

    lrc = jnp.minimum(lrc_ref[0], LAMBDA_RE_MAX)
    dtc = jnp.exp(dtc_ref[0])
    thc_r, thc_i = lrc * dtc, lic_ref[0] * dtc
    lane = lax.broadcasted_iota(jnp.int32, (NN, PC), 1)
    p0r, p0i = _cplx_pow((lane >> shift_c).astype(F32), thc_r, thc_i)
    acr, aci = _cplx_pow(1.0, thc_r, thc_i)
    p1r, p1i = p0r * acr - p0i * aci, p0r * aci + p0i * acr
    tile_c = ((lax.broadcasted_iota(jnp.int32, (C, PC), 1) & (C - 1))
              == lax.broadcasted_iota(jnp.int32, (C, PC), 0)).astype(F32)
    ctr = jnp.dot(ctr_ref[0], tile_c, precision=HIGHEST, preferred_element_type=F32)
    cti = jnp.dot(cti_ref[0], tile_c, precision=HIGHEST, preferred_element_type=F32)
    w0r, w0i = ctr * p0r - cti * p0i, ctr * p0i + cti * p0r
    vr, vi = ctr * p1r - cti * p1i, -(ctr * p1i + cti * p1r)
    row_grp = lax.broadcasted_iota(jnp.int32, (NN, PC), 0) >> shift_n
    wc_r = jnp.concatenate([jnp.where(row_grp == g, vr, 0.0) for g in range(S5_PAIR)], axis=1)
    wc_i = jnp.concatenate([jnp.where(row_grp == g, vi, 0.0) for g in range(S5_PAIR)], axis=1)
    wc_ref[0] = jnp.concatenate([wc_r, wc_i], axis=0).astype(BF16)

    b_grp = lax.broadcasted_iota(jnp.int32, (C, NN), 1) >> shift_n
    lane_k = lax.broadcasted_iota(jnp.int32, (C, PC), 1)
    same_pos = ((lax.broadcasted_iota(jnp.int32, (PC, PC), 0) >> shift_c)
                == (lax.broadcasted_iota(jnp.int32, (PC, PC), 1) >> shift_c))
    zero_blk = jnp.zeros((PC, PC), F32)

    def on_diagonal(blk, g):
        return jnp.concatenate([blk if j == g else zero_blk for j in range(S5_PAIR)], axis=1)

    kt_rows, glu_v, glu_g, bias_v, bias_g, skip = [], [], [], [], [], []
    for g in range(S5_PAIR):
        k0 = (jnp.dot(jnp.where(b_grp == g, bbr, 0.0), w0r, precision=HIGHEST, preferred_element_type=F32)
              - jnp.dot(jnp.where(b_grp == g, bbi, 0.0), w0i, precision=HIGHEST, preferred_element_type=F32))
        toep = jnp.concatenate(
            [k0 if p == 0 else jnp.where(lane_k >= C * p, pltpu.roll(k0, C * p, axis=1), 0.0) for p in range(P)],
            axis=0)
        kt_rows.append(on_diagonal(toep, g))
        for w_ref, dst in ((gw1_ref, glu_v), (gw2_ref, glu_g)):
            t = jnp.dot(w_ref[0, g], tile_c, precision=HIGHEST, preferred_element_type=F32)
            dst.append(on_diagonal(jnp.where(same_pos, jnp.concatenate([t] * P, axis=0), 0.0), g))
        gb = gbcol_ref[0, g]
        bias_v.append(jnp.sum(gb[:C] * tile_c, axis=0, keepdims=True))
        bias_g.append(jnp.sum(gb[C:] * tile_c, axis=0, keepdims=True))
        skip.append(jnp.sum(dcol_ref[0, g] * tile_c, axis=0, keepdims=True))
    kt_ref[0] = jnp.concatenate(kt_rows, axis=0).astype(BF16)
    glu_ref[0] = jnp.concatenate([jnp.concatenate(glu_v, axis=0), jnp.concatenate(glu_g, axis=0)],
                                 axis=1).astype(BF16)
    glub_ref[0] = jnp.concatenate(bias_v + bias_g, axis=1)
    dsk_ref[0] = jnp.concatenate(skip, axis=1)


def _s5_tables(lam_re, lam_im, log_dt, b_re, b_im, c_re, c_im, d_skip, glu_w, glu_b):
    G, N, C, P = S5_GROUPS, S5_STATE, S5_GROUP_CH, S5_BLOCK
    J, NN, PC = G // S5_PAIR, S5_PAIR * S5_STATE, S5_BLOCK * S5_GROUP_CH
    f = lambda a: a.astype(F32)
    dt_full = jnp.broadcast_to(f(log_dt)[:, None], (G, N))
    rows = [a.reshape(J, 1, NN) for a in (f(lam_re), f(lam_im), dt_full)]
    cols = [a.reshape(J, NN, 1) for a in (f(lam_re), f(lam_im), dt_full)]
    bt = [f(a).reshape(J, S5_PAIR, N, C).transpose(0, 3, 1, 2).reshape(J, C, NN) for a in (b_re, b_im)]
    ct = [f(a).transpose(0, 2, 1).reshape(J, NN, C) for a in (c_re, c_im)]
    dcol = f(d_skip).reshape(J, S5_PAIR, C, 1)
    gw1 = f(glu_w)[..., :C].reshape(J, S5_PAIR, C, C)
    gw2 = f(glu_w)[..., C:].reshape(J, S5_PAIR, C, C)
    gbcol = f(glu_b).reshape(J, S5_PAIR, 2 * C, 1)
    ins = rows + cols + bt + ct + [dcol, gw1, gw2, gbcol]
    per_pair = lambda shape: pl.BlockSpec((1,) + shape, lambda j: (j,) + (0,) * len(shape))
    out_shapes = [((S5_PAIR * PC, 2 * NN), BF16), ((S5_PAIR * PC, S5_PAIR * PC), BF16),
                  ((2 * NN, S5_PAIR * PC), BF16), ((S5_PAIR * PC, 2 * S5_PAIR * PC), BF16),
                  ((1, 2 * S5_PAIR * PC), F32), ((1, S5_PAIR * PC), F32),
                  ((S5_SCAN_ROWS, NN), F32), ((S5_SCAN_ROWS, NN), F32)]
    return pl.pallas_call(
        _s5_tables_kernel,
        grid=(J,),
        in_specs=[per_pair(a.shape[1:]) for a in ins],
        out_specs=[per_pair(s) for s, _ in out_shapes],
        out_shape=[jax.ShapeDtypeStruct((J,) + s, d) for s, d in out_shapes],
        compiler_params=pltpu.CompilerParams(dimension_semantics=("parallel",),
                                             vmem_limit_bytes=VMEM_LIMIT_BYTES),
        name="s5_tables",
    )(*ins)


def _s5_kernel(n_steps, u_ref, we_ref, kt_ref, wc_ref, glu_ref, glub_ref, dsk_ref, sr_ref, si_ref, out_ref):
    P, C = S5_BLOCK, S5_GROUP_CH
    nb, slab = u_ref.shape[1], u_ref.shape[2]
    wide = slab * P
    pair_w = S5_PAIR * P * C
    half = S5_PAIR * S5_STATE
    shift_c = C.bit_length() - 1

    lane = lax.broadcasted_iota(jnp.int32, (slab, wide), 1)
    src = lax.broadcasted_iota(jnp.int32, (slab, wide), 0)
    target0 = ((src >> shift_c) << (shift_c + P.bit_length() - 1)) + (src & (C - 1))

    def place(p0):
        return jnp.concatenate([jnp.where(lane == target0 + p * C, 1.0, 0.0).astype(BF16)
                                for p in (p0, p0 + 1)], axis=0)

    u_blk = jnp.zeros((nb, wide), F32)
    for p0 in range(0, P, 2):
        lhs = jnp.concatenate([u_ref[p0], u_ref[p0 + 1]], axis=1)
        u_blk = u_blk + jnp.dot(lhs, place(p0), preferred_element_type=F32)
    u_blk = u_blk.astype(BF16)

    row = lax.broadcasted_iota(jnp.int32, (nb, half), 0)
    outs = []
    for q in range(S5_STEP_PAIRS):
        u = u_blk[:, q * pair_w:(q + 1) * pair_w]
        e = jnp.dot(u, we_ref[q], preferred_element_type=F32)
        xr, xi = e[:, :half], e[:, half:]
        for k in range(n_steps):
            sh = 1 << k
            ar = sr_ref[q, k:k + 1, :]
            ai = si_ref[q, k:k + 1, :]
            valid = row >= sh
            yr = jnp.where(valid, pltpu.roll(xr, sh, axis=0), 0.0)
            yi = jnp.where(valid, pltpu.roll(xi, sh, axis=0), 0.0)
            xr, xi = xr + ar * yr - ai * yi, xi + ar * yi + ai * yr
        first = row >= 1
        pr = jnp.where(first, pltpu.roll(xr, 1, axis=0), 0.0)
        pi = jnp.where(first, pltpu.roll(xi, 1, axis=0), 0.0)
        sprev = jnp.concatenate([pr, pi], axis=1).astype(BF16)
        y = (jnp.dot(u, kt_ref[q], preferred_element_type=F32)
             + jnp.dot(sprev, wc_ref[q], preferred_element_type=F32)
             + dsk_ref[q] * u.astype(F32))
        y = jax.nn.gelu(y)
        z = jnp.dot(y.astype(BF16), glu_ref[q], preferred_element_type=F32) + glub_ref[q]
        outs.append(z[:, :pair_w] * jax.nn.sigmoid(z[:, pair_w:]))
    out_blk = jnp.concatenate(outs, axis=1).astype(BF16)
    for p0 in range(0, P, 2):
        o = lax.dot_general(out_blk, place(p0), _nt_dims(), preferred_element_type=F32)
        out_ref[p0] = o[:, :slab].astype(BF16)
        out_ref[p0 + 1] = o[:, slab:].astype(BF16)


def _s5(u, tables, batch, seq_len):
    nb = seq_len // S5_BLOCK
    n_steps = max(1, (nb - 1).bit_length())
    assert n_steps <= S5_SCAN_ROWS
    slab = pl.BlockSpec((S5_BLOCK, nb, LANES), lambda b, o: (0, b, o))
    per_step = lambda a: pl.BlockSpec((S5_STEP_PAIRS,) + a.shape[1:], lambda b, o: (o, 0, 0))
    return pl.pallas_call(
        functools.partial(_s5_kernel, n_steps),
        grid=(batch, S5_GROUPS // S5_STEP_GROUPS),
        in_specs=[slab] + [per_step(a) for a in tables],
        out_specs=slab,
        out_shape=jax.ShapeDtypeStruct(u.shape, BF16),
        compiler_params=pltpu.CompilerParams(dimension_semantics=("parallel", "parallel"),
                                             vmem_limit_bytes=VMEM_LIMIT_BYTES),
        name="s5",
    )(u, *tables)


def _outproj_kernel(a_ref, b_ref, x_ref, wa_ref, wb_ref, g_ref, rgw_ref, rgb_ref, rew_ref, reb_ref,
                    x1_ref, h2_ref, meta_ref, metat_ref, cnt_ref, bs_ref):
    tm = x_ref.shape[0]

    @pl.when(pl.program_id(0) == 0)
    def _():
        cnt_ref[...] = jnp.zeros(cnt_ref.shape, F32)

    n_slabs = bs_ref.shape[0]
    for p in range(S5_BLOCK):
        bp = b_ref[p].astype(F32)
        for j in range(n_slabs):
            bs_ref[j, pl.ds(p, tm // S5_BLOCK, stride=S5_BLOCK), :] = bp[:, j * LANES:(j + 1) * LANES]
    out_b = jnp.concatenate([bs_ref[j] for j in range(n_slabs)], axis=1).astype(BF16)
    x1 = (x_ref[...]
          + jnp.dot(a_ref[...], wa_ref[...], preferred_element_type=F32)
          + jnp.dot(out_b, wb_ref[...], preferred_element_type=F32))
    x1_ref[...] = x1
    h2 = x1 * lax.rsqrt(jnp.mean(x1 * x1, axis=-1, keepdims=True) + EPS) * g_ref[...]
    h2_ref[...] = h2

    gl = jnp.dot(h2, rgw_ref[...], precision=HIGHEST, preferred_element_type=F32) + rgb_ref[...]
    el = jnp.dot(h2, rew_ref[...], precision=HIGHEST, preferred_element_type=F32) + reb_ref[...]
    gmax = jnp.max(gl, axis=1, keepdims=True)
    g_w = 1.0 / jnp.sum(jnp.exp(gl - gmax), axis=1, keepdims=True)
    lane_g = lax.broadcasted_iota(jnp.int32, (tm, N_GROUPS), 1)
    g_idx = jnp.min(jnp.where(gl == gmax, lane_g, N_GROUPS), axis=1, keepdims=True)
    lane_e = lax.broadcasted_iota(jnp.int32, (tm, N_EXPERTS), 1)
    elm = jnp.where(lane_e // PER_GROUP == g_idx, el, -jnp.inf)
    m1 = jnp.max(elm, axis=1, keepdims=True)
    i1 = jnp.min(jnp.where(elm == m1, lane_e, N_EXPERTS), axis=1, keepdims=True)
    elm2 = jnp.where(lane_e == i1, -jnp.inf, elm)
    m2 = jnp.max(elm2, axis=1, keepdims=True)
    i2 = jnp.min(jnp.where(elm2 == m2, lane_e, N_EXPERTS), axis=1, keepdims=True)
    e2 = jnp.exp(m2 - m1)
    w1 = g_w / (1.0 + e2)
    w2 = g_w * e2 / (1.0 + e2)

    sel1 = lane_e == i1
    sel2 = lane_e == i2
    picked = (sel1 | sel2).astype(BF16)
    rix = lax.broadcasted_iota(jnp.int32, (tm, tm), 0)
    cix = lax.broadcasted_iota(jnp.int32, (tm, tm), 1)
    before = jnp.dot((rix > cix).astype(BF16), picked, preferred_element_type=F32) + cnt_ref[...]
    r1 = jnp.sum(jnp.where(sel1, before, 0.0), axis=1, keepdims=True)
    r2 = jnp.sum(jnp.where(sel2, before, 0.0), axis=1, keepdims=True)
    cnt_ref[...] += jnp.sum(picked.astype(F32), axis=0, keepdims=True)
    col = lax.broadcasted_iota(jnp.int32, (tm, META_COLS), 1)
    meta = jnp.zeros((tm, META_COLS), F32)
    for c, val in enumerate((i1.astype(F32), i2.astype(F32), r1, r2, w1, w2)):
        meta = jnp.where(col == c, val, meta)
    meta_ref[...] = meta
    eye = (lax.broadcasted_iota(jnp.int32, (META_COLS, META_COLS), 0)
           == lax.broadcasted_iota(jnp.int32, (META_COLS, META_COLS), 1)).astype(F32)
    metat_ref[...] = lax.dot_general(eye, meta, _nt_dims(), precision=HIGHEST, preferred_element_type=F32)


def _outproj(out_a, out_b, x2d, w_out, norm_g, rg_w, rg_b, re_w, re_b):
    T = x2d.shape[0]
    tm = OUTPROJ_TILE
    wa = w_out[:D_MLSTM].astype(BF16)
    wb = w_out[D_MLSTM:].astype(BF16)
    full = lambda shape: pl.BlockSpec(shape, lambda i: (0,) * len(shape))
    row = lambda n: pl.BlockSpec((tm, n), lambda i: (i, 0))
    return pl.pallas_call(
        _outproj_kernel,
        grid=(T // tm,),
        in_specs=[row(D_MLSTM), pl.BlockSpec((S5_BLOCK, tm // S5_BLOCK, D_S5), lambda i: (0, i, 0)), row(D_MODEL),
                  full((D_MLSTM, D_MODEL)), full((D_S5, D_MODEL)),
                  full((1, D_MODEL)), full((D_MODEL, N_GROUPS)), full((1, N_GROUPS)),
                  full((D_MODEL, N_EXPERTS)), full((1, N_EXPERTS))],
        out_specs=[row(D_MODEL), row(D_MODEL), row(META_COLS), pl.BlockSpec((META_COLS, tm), lambda i: (0, i)),
                   full((1, N_EXPERTS))],
        out_shape=[jax.ShapeDtypeStruct((T, D_MODEL), F32), jax.ShapeDtypeStruct((T, D_MODEL), F32),
                   jax.ShapeDtypeStruct((T, META_COLS), F32), jax.ShapeDtypeStruct((META_COLS, T), F32),
                   jax.ShapeDtypeStruct((1, N_EXPERTS), F32)],
        scratch_shapes=[pltpu.VMEM((D_S5 // LANES, tm, LANES), F32)],
        compiler_params=pltpu.CompilerParams(dimension_semantics=("arbitrary",),
                                             vmem_limit_bytes=VMEM_LIMIT_BYTES),
        name="outproj",
    )(out_a, out_b, x2d, wa, wb, norm_g.reshape(1, D_MODEL), rg_w, rg_b.reshape(1, -1), re_w,
      re_b.reshape(1, -1))


def _plan_kernel(metat_ref, base_ref, pos_ref):
    mt = metat_ref[...]
    tt = mt.shape[1]
    expert = lax.broadcasted_iota(jnp.int32, (N_EXPERTS, tt), 0)
    base = base_ref[...]
    row = lax.broadcasted_iota(jnp.int32, (META_COLS, tt), 0)
    pos = jnp.zeros((META_COLS, tt), F32)
    for k in range(2):
        first = jnp.sum(jnp.where(expert == mt[k:k + 1, :].astype(jnp.int32), base, 0.0), axis=0, keepdims=True)
        pos = jnp.where(row == k, first + mt[2 + k:3 + k, :], pos)
    pos_ref[...] = pos.astype(jnp.int32)


def _plan(meta_t, base):
    T = meta_t.shape[1]
    tt = PLAN_TILE
    return pl.pallas_call(
        _plan_kernel,
        grid=(T // tt,),
        in_specs=[pl.BlockSpec((META_COLS, tt), lambda i: (0, i)), pl.BlockSpec((N_EXPERTS, 1), lambda i: (0, 0))],
        out_specs=pl.BlockSpec((META_COLS, tt), lambda i: (0, i)),
        out_shape=jax.ShapeDtypeStruct((META_COLS, T), jnp.int32),
        compiler_params=pltpu.CompilerParams(dimension_semantics=("parallel",),
                                             vmem_limit_bytes=VMEM_LIMIT_BYTES),
        name="plan",
    )(meta_t, base.astype(F32).reshape(N_EXPERTS, 1))


def _row_copy(src_ref, src_row, dst_ref, dst_row, sem):
    return pltpu.make_async_copy(src_ref.at[pl.ds(src_row, 1), :], dst_ref.at[pl.ds(dst_row, 1), :], sem)


def _pad_copy(zero_ref, xs_ref, row, sem):
    return pltpu.make_async_copy(zero_ref, xs_ref.at[pl.ds(row, FFN_TILE), :], sem)


def _dispatch_kernel(pad_ref, p1_ref, p2_ref, h_ref, xs_ref, zero_ref, sem, zsem):
    tm = h_ref.shape[0]
    n_tiles = xs_ref.shape[0] // FFN_TILE

    @pl.when(pl.program_id(0) == 0)
    def _():
        zero_ref[...] = jnp.zeros(zero_ref.shape, F32)
        for e in range(N_EXPERTS):
            _pad_copy(zero_ref, xs_ref, pl.multiple_of(pad_ref[e], 8), zsem).start()
        for e in range(N_EXPERTS):
            _pad_copy(zero_ref, xs_ref, 0, zsem).wait()
        first_free = pad_ref[N_EXPERTS]

        def start_tail(k, carry):
            _pad_copy(zero_ref, xs_ref, pl.multiple_of(k * FFN_TILE, FFN_TILE), zsem).start()
            return carry

        def wait_tail(k, carry):
            _pad_copy(zero_ref, xs_ref, 0, zsem).wait()
            return carry

        lax.fori_loop(first_free, n_tiles, start_tail, 0)
        lax.fori_loop(first_free, n_tiles, wait_tail, 0)

    def issue(t, carry):
        _row_copy(h_ref, t, xs_ref, p1_ref[0, 0, t], sem).start()
        _row_copy(h_ref, t, xs_ref, p2_ref[0, 0, t], sem).start()
        return carry

    lax.fori_loop(0, tm, issue, 0, unroll=ROW_DMA_UNROLL)

    def drain(t, carry):
        _row_copy(h_ref, 0, xs_ref, 0, sem).wait()
        _row_copy(h_ref, 0, xs_ref, 0, sem).wait()
        return carry

    lax.fori_loop(0, tm, drain, 0, unroll=ROW_DMA_UNROLL)


def _dispatch(h2, pos1, pos2, pad_start, n_rows):
    T, w = h2.shape
    tm = DISPATCH_TILE
    smem_row = pl.BlockSpec((1, 1, tm), lambda i, pad: (i, 0, 0), memory_space=pltpu.SMEM)
    return pl.pallas_call(
        _dispatch_kernel,
        grid_spec=pltpu.PrefetchScalarGridSpec(
            num_scalar_prefetch=1,
            grid=(T // tm,),
            in_specs=[smem_row, smem_row, pl.BlockSpec((tm, w), lambda i, pad: (i, 0))],
            out_specs=pl.BlockSpec(memory_space=pl.ANY),
            scratch_shapes=[pltpu.VMEM((FFN_TILE, w), F32), pltpu.SemaphoreType.DMA, pltpu.SemaphoreType.DMA]),
        out_shape=jax.ShapeDtypeStruct((n_rows + FFN_TILE, w), F32),
        compiler_params=pltpu.CompilerParams(dimension_semantics=("arbitrary",),
                                             vmem_limit_bytes=VMEM_LIMIT_BYTES),
        name="dispatch",
    )(pad_start, pos1.reshape(T // tm, 1, tm), pos2.reshape(T // tm, 1, tm), h2)


def _ffn_kernel(te_ref, nused_ref, xs_ref, wg_ref, wu_ref, wd_ref, ys_ref, wgb_ref, wub_ref, wdb_ref):
    i = pl.program_id(0)
    used = i < nused_ref[0]

    @pl.when(used)
    def _():
        @pl.when((i == 0) | (te_ref[i] != te_ref[jnp.maximum(i - 1, 0)]))
        def _():
            wgb_ref[...] = wg_ref[0].astype(BF16)
            wub_ref[...] = wu_ref[0].astype(BF16)
            wdb_ref[...] = wd_ref[0].astype(BF16)

        x = xs_ref[...].astype(BF16)
        a = jnp.dot(x, wgb_ref[...], preferred_element_type=F32)
        b = jnp.dot(x, wub_ref[...], preferred_element_type=F32)
        act = (a * jax.nn.sigmoid(a) * b).astype(BF16)
        ys_ref[...] = jnp.dot(act, wdb_ref[...], preferred_element_type=F32)

    @pl.when(jnp.logical_not(used))
    def _():
        ys_ref[...] = jnp.zeros(ys_ref.shape, F32)


def _ffn(xs, tile_expert, n_used, n_rows, w_gate, w_up, w_down):
    w = xs.shape[1]
    tm = FFN_TILE
    wspec = lambda shape: pl.BlockSpec((1,) + shape, lambda i, te, nu: (te[i], 0, 0))
    return pl.pallas_call(
        _ffn_kernel,
        grid_spec=pltpu.PrefetchScalarGridSpec(
            num_scalar_prefetch=2,
            grid=(n_rows // tm,),
            in_specs=[pl.BlockSpec((tm, w), lambda i, te, nu: (jnp.minimum(i, nu[0] - 1), 0)),
                      wspec((D_MODEL, D_EXPERT)), wspec((D_MODEL, D_EXPERT)), wspec((D_EXPERT, D_MODEL))],
            out_specs=pl.BlockSpec((tm, w), lambda i, te, nu: (i, 0)),
            scratch_shapes=[pltpu.VMEM((D_MODEL, D_EXPERT), BF16), pltpu.VMEM((D_MODEL, D_EXPERT), BF16),
                            pltpu.VMEM((D_EXPERT, D_MODEL), BF16)]),
        out_shape=jax.ShapeDtypeStruct((n_rows, w), F32),
        compiler_params=pltpu.CompilerParams(dimension_semantics=("arbitrary",),
                                             vmem_limit_bytes=VMEM_LIMIT_BYTES),
        name="ffn",
    )(tile_expert, n_used, xs, w_gate, w_up, w_down)


def _combine_kernel(p1_ref, p2_ref, ys_ref, meta_ref, x1_ref, gf_ref, out_ref, buf_ref, sem):
    tm = x1_ref.shape[0]

    def issue(t, carry):
        _row_copy(ys_ref, p1_ref[0, 0, t], buf_ref.at[0], t, sem).start()
        _row_copy(ys_ref, p2_ref[0, 0, t], buf_ref.at[1], t, sem).start()
        return carry

    lax.fori_loop(0, tm, issue, 0, unroll=ROW_DMA_UNROLL)

    def drain(t, carry):
        _row_copy(ys_ref, 0, buf_ref.at[0], 0, sem).wait()
        _row_copy(ys_ref, 0, buf_ref.at[1], 0, sem).wait()
        return carry

    lax.fori_loop(0, tm, drain, 0, unroll=ROW_DMA_UNROLL)

    meta = meta_ref[...]
    y = x1_ref[...] + meta[:, 4:5] * buf_ref[0] + meta[:, 5:6] * buf_ref[1]
    out_ref[...] = y * lax.rsqrt(jnp.mean(y * y, axis=-1, keepdims=True) + EPS) * gf_ref[...]


def _combine(ys, pos1, pos2, meta, x1, norm_final_g):
    T = x1.shape[0]
    w = ys.shape[1]
    tm = COMBINE_TILE
    smem_row = pl.BlockSpec((1, 1, tm), lambda i: (i, 0, 0), memory_space=pltpu.SMEM)
    row = lambda n: pl.BlockSpec((tm, n), lambda i: (i, 0))
    return pl.pallas_call(
        _combine_kernel,
        grid=(T // tm,),
        in_specs=[smem_row, smem_row, pl.BlockSpec(memory_space=pl.ANY), row(META_COLS), row(D_MODEL),
                  pl.BlockSpec((1, D_MODEL), lambda i: (0, 0))],
        out_specs=row(D_MODEL),
        out_shape=jax.ShapeDtypeStruct((T, D_MODEL), F32),
        scratch_shapes=[pltpu.VMEM((2, tm, w), F32), pltpu.SemaphoreType.DMA],
        compiler_params=pltpu.CompilerParams(dimension_semantics=("arbitrary",),
                                             vmem_limit_bytes=VMEM_LIMIT_BYTES),
        name="combine",
    )(pos1.reshape(T // tm, 1, tm), pos2.reshape(T // tm, 1, tm), ys, meta, x1,
      norm_final_g.reshape(1, D_MODEL))


def _run_layout(counts, n_tiles):
    cnt = counts.reshape(-1).astype(jnp.int32)
    padded = (cnt + FFN_TILE - 1) // FFN_TILE * FFN_TILE
    ends = jnp.cumsum(padded)
    base = ends - padded
    tile_start = jnp.arange(n_tiles, dtype=jnp.int32) * FFN_TILE
    tile_expert = jnp.sum((tile_start[:, None] >= ends[None, :]).astype(jnp.int32), axis=1)
    tile_expert = jnp.minimum(tile_expert, N_EXPERTS - 1)
    n_used = (ends[-1:] // FFN_TILE).astype(jnp.int32)
    pad_start = jnp.concatenate([base + cnt // 8 * 8, n_used])
    return base, pad_start, tile_expert, n_used


def kernel(x, norm_mix_g, w_in, conv_w, conv_b, i_bias, f_bias, mlstm_norm_g, s5_lambda_re, s5_lambda_im,
           s5_log_dt, s5_b_re, s5_b_im, s5_c_re, s5_c_im, s5_d, s5_glu_w, s5_glu_b, w_out, norm_ffn_g,
           router_group_w, router_group_b, router_expert_w, router_expert_b, expert_w_gate, expert_w_up,
           expert_w_down, norm_final_g):
    batch, seq_len, d_model = x.shape
    assert d_model == D_MODEL and norm_mix_g.shape[0] == 1
    assert seq_len % MLSTM_CHUNK == 0 and seq_len % INPROJ_TILE == 0 and seq_len % S5_BLOCK == 0
    T = batch * seq_len
    x2d = x.reshape(T, D_MODEL)

    q, k, v, og, u, gcol, grow = _inproj(x2d, norm_mix_g[0], w_in[0], conv_w[0], conv_b[0], i_bias[0],
                                         f_bias[0], seq_len)
    out_a = _mlstm(q, k, v, og, gcol, grow, mlstm_norm_g[0], batch, seq_len)
    tables = _s5_tables(s5_lambda_re[0], s5_lambda_im[0], s5_log_dt[0], s5_b_re[0], s5_b_im[0], s5_c_re[0],
                        s5_c_im[0], s5_d[0], s5_glu_w[0], s5_glu_b[0])
    out_b = _s5(u, tables, batch, seq_len)
    x1, h2, meta, meta_t, counts = _outproj(out_a, out_b, x2d, w_out[0], norm_ffn_g[0], router_group_w[0],
                                            router_group_b[0], router_expert_w[0], router_expert_b[0])
    n_rows = 2 * T + N_EXPERTS * FFN_TILE
    base, pad_start, tile_expert, n_used = _run_layout(counts, n_rows // FFN_TILE)
    pos = _plan(meta_t, base)
    xs = _dispatch(h2, pos[0], pos[1], pad_start, n_rows)
    ys = _ffn(xs, tile_expert, n_used, n_rows, expert_w_gate[0], expert_w_up[0], expert_w_down[0])
    y = _combine(ys, pos[0], pos[1], meta, x1, norm_final_g)
    return y.reshape(batch, seq_len, D_MODEL)
```

```python
import functools

import jax
import jax.numpy as jnp
from jax import lax
from jax.experimental import pallas as pl
from jax.experimental.pallas import tpu as pltpu

F32 = jnp.float32
BF16 = jnp.bfloat16
HIGHEST = lax.Precision.HIGHEST

D_MODEL = 1024
D_MLSTM = 512
D_S5 = 512
HEADS = 4
HEAD_DIM = 128
CONV_WIDTH = 4
S5_GROUP_CH = 16
S5_GROUPS = 32
S5_STATE = 64
N_GROUPS = 4
PER_GROUP = 8
N_EXPERTS = 32
D_EXPERT = 512
EPS = 1e-6
LAMBDA_RE_MAX = -1e-4

VMEM_LIMIT_BYTES = 56 * 1024 * 1024
LANES = 128

INPROJ_TILE = 512
MLSTM_CHUNK = 256
S5_BLOCK = 16
S5_PAIR = 2
S5_STEP_GROUPS = LANES // S5_GROUP_CH
S5_STEP_PAIRS = S5_STEP_GROUPS // S5_PAIR
S5_SCAN_ROWS = 16
OUTPROJ_TILE = 512
META_COLS = 8
PLAN_TILE = 2048
DISPATCH_TILE = 512
FFN_TILE = 256
COMBINE_TILE = 256
ROW_DMA_UNROLL = 8


def _nt_dims():
    return (((1,), (1,)), ((), ()))


def _tn_dims():
    return (((0,), (0,)), ((), ()))


def _log_sigmoid(x):
    return jnp.minimum(x, 0.0) - jnp.log1p(jnp.exp(-jnp.abs(x)))


def _split_weight(w):
    w = jnp.pad(w.astype(F32), ((0, 0), (0, LANES - w.shape[1])))
    hi = w.astype(BF16)
    lo = (w - hi.astype(F32)).astype(BF16)
    return jnp.concatenate([hi, lo], axis=1), hi


def _dot_split(x, x_hi, w_hilo_ref, w_hi_ref):
    x_lo = (x - x_hi.astype(F32)).astype(BF16)
    two = jnp.dot(x_hi, w_hilo_ref[...], preferred_element_type=F32)
    return two[:, :LANES] + two[:, LANES:] + jnp.dot(x_lo, w_hi_ref[...], preferred_element_type=F32)


def _transpose_small(a):
    n = a.shape[1]
    eye = (lax.broadcasted_iota(jnp.int32, (n, n), 0) == lax.broadcasted_iota(jnp.int32, (n, n), 1)).astype(F32)
    return lax.dot_general(eye, a, _nt_dims(), precision=HIGHEST, preferred_element_type=F32)


def _inproj_kernel(tiles_per_seq, x_ref, g_ref, wqk_ref, wv_ref, wo_ref, wu_ref, wg2_ref, wg1_ref,
                   gb_ref, cw_ref, cb_ref,
                   q_ref, k_ref, v_ref, og_ref, u_ref, gcol_ref, grow_ref, ext_ref, us_ref):
    i = pl.program_id(0)
    tm = x_ref.shape[0]
    x = x_ref[...]
    h = x * lax.rsqrt(jnp.mean(x * x, axis=-1, keepdims=True) + EPS) * g_ref[...]
    hb = h.astype(BF16)

    qk_raw = jnp.dot(hb, wqk_ref[...], preferred_element_type=F32)

    @pl.when(i % tiles_per_seq == 0)
    def _():
        ext_ref[0:8, :] = jnp.zeros((8, ext_ref.shape[1]), F32)

    ext_ref[8:, :] = qk_raw
    cw = cw_ref[...]
    acc = cb_ref[...] + cw[CONV_WIDTH - 1:CONV_WIDTH, :] * qk_raw
    for s in range(1, CONV_WIDTH):
        acc = acc + cw[CONV_WIDTH - 1 - s:CONV_WIDTH - s, :] * ext_ref[pl.ds(8 - s, tm), :]
    ext_ref[0:8, :] = qk_raw[tm - 8:, :]
    qk = acc * jax.nn.sigmoid(acc)
    q_ref[...] = qk[:, :D_MLSTM].astype(BF16)
    k_ref[...] = (qk[:, D_MLSTM:] * (HEAD_DIM ** -0.5)).astype(BF16)

    v_ref[...] = jnp.dot(hb, wv_ref[...], preferred_element_type=F32).astype(BF16)
    o_pre = jnp.dot(hb, wo_ref[...], preferred_element_type=F32)
    og_ref[...] = jax.nn.sigmoid(o_pre).astype(BF16)

    u = jnp.dot(hb, wu_ref[...], preferred_element_type=F32)
    n_slabs = us_ref.shape[0]
    for j in range(n_slabs):
        us_ref[j] = u[:, j * LANES:(j + 1) * LANES]
    for p in range(S5_BLOCK):
        u_ref[p] = jnp.concatenate([us_ref[j, pl.ds(p, tm // S5_BLOCK, stride=S5_BLOCK), :]
                                    for j in range(n_slabs)], axis=1).astype(BF16)

    gcol = _dot_split(h, hb, wg2_ref, wg1_ref)[:, :2 * HEADS] + gb_ref[...]
    gcol_ref[...] = gcol
    grow_ref[...] = _transpose_small(gcol)


def _inproj(x2d, norm_g, w_in, conv_w, conv_b, i_bias, f_bias, seq_len):
    T = x2d.shape[0]
    tm = INPROJ_TILE
    dm = D_MLSTM
    wqk = w_in[:, :2 * dm].astype(BF16)
    wv = w_in[:, 2 * dm:3 * dm].astype(BF16)
    wo = w_in[:, 3 * dm:4 * dm].astype(BF16)
    wg2, wg1 = _split_weight(w_in[:, 4 * dm:4 * dm + 2 * HEADS])
    wu = w_in[:, 4 * dm + 2 * HEADS:].astype(BF16)
    gb = jnp.concatenate([i_bias, f_bias]).astype(F32)
    full = lambda shape: pl.BlockSpec(shape, lambda i: (0,) * len(shape))
    row = lambda n: pl.BlockSpec((tm, n), lambda i: (i, 0))
    return pl.pallas_call(
        functools.partial(_inproj_kernel, seq_len // tm),
        grid=(T // tm,),
        in_specs=[row(D_MODEL), full((1, D_MODEL)), full((D_MODEL, 2 * dm)), full((D_MODEL, dm)),
                  full((D_MODEL, dm)), full((D_MODEL, D_S5)), full((D_MODEL, 2 * LANES)),
                  full((D_MODEL, LANES)), full((1, 2 * HEADS)),
                  full((CONV_WIDTH, 2 * dm)), full((1, 2 * dm))],
        out_specs=[row(dm), row(dm), row(dm), row(dm),
                   pl.BlockSpec((S5_BLOCK, tm // S5_BLOCK, D_S5), lambda i: (0, i, 0)), row(2 * HEADS),
                   pl.BlockSpec((2 * HEADS, tm), lambda i: (0, i))],
        out_shape=[jax.ShapeDtypeStruct((T, dm), BF16)] * 4
        + [jax.ShapeDtypeStruct((S5_BLOCK, T // S5_BLOCK, D_S5), BF16),
           jax.ShapeDtypeStruct((T, 2 * HEADS), F32),
           jax.ShapeDtypeStruct((2 * HEADS, T), F32)],
        scratch_shapes=[pltpu.VMEM((tm + 8, 2 * dm), F32), pltpu.VMEM((D_S5 // LANES, tm, LANES), F32)],
        compiler_params=pltpu.CompilerParams(dimension_semantics=("arbitrary",),
                                             vmem_limit_bytes=VMEM_LIMIT_BYTES),
        name="inproj",
    )(x2d, norm_g.reshape(1, D_MODEL), wqk, wv, wo, wu, wg2, wg1, gb.reshape(1, -1),
      conv_w, conv_b.reshape(1, -1))


def _mlstm_kernel(q_ref, k_ref, v_ref, og_ref, gcol_ref, grow_ref, ng_ref, out_ref, c_ref, m_ref):
    tc = q_ref.shape[0]
    d = HEAD_DIM

    @pl.when(pl.program_id(1) == 0)
    def _():
        c_ref[...] = jnp.zeros(c_ref.shape, F32)
        m_ref[...] = jnp.zeros(m_ref.shape, F32)

    gcol = gcol_ref[...]
    grow = grow_ref[...]
    rix = lax.broadcasted_iota(jnp.int32, (tc, tc), 0)
    cix = lax.broadcasted_iota(jnp.int32, (tc, tc), 1)
    causal = rix >= cix
    b_col_all = jnp.dot(causal.astype(F32), _log_sigmoid(gcol), precision=HIGHEST,
                        preferred_element_type=F32)
    b_row_all = jnp.dot(_log_sigmoid(grow), (rix <= cix).astype(F32), precision=HIGHEST,
                        preferred_element_type=F32)
    ones_col = (lax.broadcasted_iota(jnp.int32, (tc, d), 1) == 0).astype(BF16)

    for h in range(HEADS):
        hs = slice(h * d, (h + 1) * d)
        li_col = gcol[:, h:h + 1]
        li_row = grow[h:h + 1, :]
        b_col = b_col_all[:, HEADS + h:HEADS + h + 1]
        b_row = b_row_all[HEADS + h:HEADS + h + 1, :]
        m_prev = m_ref[h:h + 1, 0:1]

        dmat = jnp.where(causal, b_col - b_row + li_row, -jnp.inf)
        inter = b_col + m_prev
        m_row = jnp.maximum(inter, jnp.max(dmat, axis=1, keepdims=True))
        w_intra = jnp.exp(dmat - m_row)
        w_inter = jnp.exp(inter - m_row)

        qh = q_ref[:, hs]
        kh = k_ref[:, hs]
        v_aug = jnp.concatenate([v_ref[:, hs], ones_col], axis=1)
        s = lax.dot_general(qh, kh, _nt_dims(), preferred_element_type=F32) * w_intra
        c_aug = c_ref[h]
        tot = (jnp.dot(s.astype(BF16), v_aug, preferred_element_type=F32)
               + w_inter * lax.dot_general(qh, c_aug.astype(BF16), _nt_dims(), preferred_element_type=F32))
        num = tot[:, :d]
        den = tot[:, d:d + 1]
        hh = num / jnp.maximum(jnp.abs(den), jnp.exp(-m_row))
        hh = hh * lax.rsqrt(jnp.mean(hh * hh, axis=-1, keepdims=True) + EPS)
        out_ref[:, hs] = (og_ref[:, hs].astype(F32) * (hh * ng_ref[:, hs])).astype(BF16)

        b_last = b_col[tc - 1:tc, :]
        d_state = b_last - b_col + li_col
        m_new = jnp.maximum(b_last + m_prev, jnp.max(d_state, axis=0, keepdims=True))
        w_s = jnp.exp(d_state - m_new)
        w_c = jnp.exp(b_last + m_prev - m_new)
        upd = lax.dot_general((v_aug.astype(F32) * w_s).astype(BF16), kh, _tn_dims(),
                              preferred_element_type=F32)
        c_ref[h] = w_c * c_aug + upd
        m_ref[h:h + 1, :] = jnp.broadcast_to(m_new, (1, m_ref.shape[1]))


def _mlstm(q, k, v, og, gcol, grow, norm_g, batch, seq_len):
    tc = MLSTM_CHUNK
    nc = seq_len // tc
    T = q.shape[0]
    row = lambda n: pl.BlockSpec((tc, n), lambda b, c: (b * nc + c, 0))
    return pl.pallas_call(
        _mlstm_kernel,
        grid=(batch, nc),
        in_specs=[row(D_MLSTM), row(D_MLSTM), row(D_MLSTM), row(D_MLSTM), row(2 * HEADS),
                  pl.BlockSpec((2 * HEADS, tc), lambda b, c: (0, b * nc + c)),
                  pl.BlockSpec((1, D_MLSTM), lambda b, c: (0, 0))],
        out_specs=row(D_MLSTM),
        out_shape=jax.ShapeDtypeStruct((T, D_MLSTM), BF16),
        scratch_shapes=[pltpu.VMEM((HEADS, 2 * HEAD_DIM, HEAD_DIM), F32), pltpu.VMEM((8, LANES), F32)],
        compiler_params=pltpu.CompilerParams(dimension_semantics=("parallel", "arbitrary"),
                                             vmem_limit_bytes=VMEM_LIMIT_BYTES),
        name="mlstm",
    )(q, k, v, og, gcol, grow, norm_g.reshape(1, D_MLSTM))


def _cplx_pow(exponent, theta_re, theta_im):
    mag = jnp.exp(exponent * theta_re)
    return mag * jnp.cos(exponent * theta_im), mag * jnp.sin(exponent * theta_im)


def _s5_tables_kernel(lrr_ref, lir_ref, dtr_ref, lrc_ref, lic_ref, dtc_ref, btr_ref, bti_ref, ctr_ref, cti_ref,
                      dcol_ref, gw1_ref, gw2_ref, gbcol_ref,
                      we_ref, kt_ref, wc_ref, glu_ref, glub_ref, dsk_ref, sr_ref, si_ref):
    P, C, N = S5_BLOCK, S5_GROUP_CH, S5_STATE
    PC, NN = P * C, S5_PAIR * S5_STATE
    shift_c, shift_n = C.bit_length() - 1, N.bit_length() - 1

    lr = jnp.minimum(lrr_ref[0], LAMBDA_RE_MAX)
    li = lir_ref[0]
    dt = jnp.exp(dtr_ref[0])
    th_r, th_i = lr * dt, li * dt
    ar, ai = _cplx_pow(1.0, th_r, th_i)
    den = lr * lr + li * li
    cr = ((ar - 1.0) * lr + ai * li) / den
    ci = (ai * lr - (ar - 1.0) * li) / den
    btr, bti = btr_ref[0], bti_ref[0]
    bbr = cr * btr - ci * bti
    bbi = cr * bti + ci * btr

    back = (P - 1 - lax.broadcasted_iota(jnp.int32, (P, 1), 0)).astype(F32)
    pwr, pwi = _cplx_pow(back, th_r, th_i)
    val_r = jnp.concatenate([pwr[p:p + 1] * bbr - pwi[p:p + 1] * bbi for p in range(P)], axis=0)
    val_i = jnp.concatenate([pwr[p:p + 1] * bbi + pwi[p:p + 1] * bbr for p in range(P)], axis=0)
    lane_grp = lax.broadcasted_iota(jnp.int32, (PC, NN), 1) >> shift_n
    we_r = jnp.concatenate([jnp.where(lane_grp == g, val_r, 0.0) for g in range(S5_PAIR)], axis=0)
    we_i = jnp.concatenate([jnp.where(lane_grp == g, val_i, 0.0) for g in range(S5_PAIR)], axis=0)
    we_ref[0] = jnp.concatenate([we_r, we_i], axis=1).astype(BF16)

    steps = (P << lax.broadcasted_iota(jnp.int32, (S5_SCAN_ROWS, 1), 0)).astype(F32)
    sr_ref[0], si_ref[0] = _cplx_pow(steps, th_r, th_i)

    lrc = jnp.minimum(lrc_ref[0], LAMBDA_RE_MAX)
    dtc = jnp.exp(dtc_ref[0])
    thc_r, thc_i = lrc * dtc, lic_ref[0] * dtc
    lane = lax.broadcasted_iota(jnp.int32, (NN, PC), 1)
    p0r, p0i = _cplx_pow((lane >> shift_c).astype(F32), thc_r, thc_i)
    acr, aci = _cplx_pow(1.0, thc_r, thc_i)
    p1r, p1i = p0r * acr - p0i * aci, p0r * aci + p0i * acr
    tile_c = ((lax.broadcasted_iota(jnp.int32, (C, PC), 1) & (C - 1))
              == lax.broadcasted_iota(jnp.int32, (C, PC), 0)).astype(F32)
    ctr = jnp.dot(ctr_ref[0], tile_c, precision=HIGHEST, preferred_element_type=F32)
    cti = jnp.dot(cti_ref[0], tile_c, precision=HIGHEST, preferred_element_type=F32)
    w0r, w0i = ctr * p0r - cti * p0i, ctr * p0i + cti * p0r
    vr, vi = ctr * p1r - cti * p1i, -(ctr * p1i + cti * p1r)
    row_grp = lax.broadcasted_iota(jnp.int32, (NN, PC), 0) >> shift_n
    wc_r = jnp.concatenate([jnp.where(row_grp == g, vr, 0.0) for g in range(S5_PAIR)], axis=1)
    wc_i = jnp.concatenate([jnp.where(row_grp == g, vi, 0.0) for g in range(S5_PAIR)], axis=1)
    wc_ref[0] = jnp.concatenate([wc_r, wc_i], axis=0).astype(BF16)

    b_grp = lax.broadcasted_iota(jnp.int32, (C, NN), 1) >> shift_n
    lane_k = lax.broadcasted_iota(jnp.int32, (C, PC), 1)
    same_pos = ((lax.broadcasted_iota(jnp.int32, (PC, PC), 0) >> shift_c)
                == (lax.broadcasted_iota(jnp.int32, (PC, PC), 1) >> shift_c))
    zero_blk = jnp.zeros((PC, PC), F32)

    def on_diagonal(blk, g):
        return jnp.concatenate([blk if j == g else zero_blk for j in range(S5_PAIR)], axis=1)

    kt_rows, glu_v, glu_g, bias_v, bias_g, skip = [], [], [], [], [], []
    for g in range(S5_PAIR):
        k0 = (jnp.dot(jnp.where(b_grp == g, bbr, 0.0), w0r, precision=HIGHEST, preferred_element_type=F32)
              - jnp.dot(jnp.where(b_grp == g, bbi, 0.0), w0i, precision=HIGHEST, preferred_element_type=F32))
        toep = jnp.concatenate(
            [k0 if p == 0 else jnp.where(lane_k >= C * p, pltpu.roll(k0, C * p, axis=1), 0.0) for p in range(P)],
            axis=0)
        kt_rows.append(on_diagonal(toep, g))
        for w_ref, dst in ((gw1_ref, glu_v), (gw2_ref, glu_g)):
            t = jnp.dot(w_ref[0, g], tile_c, precision=HIGHEST, preferred_element_type=F32)
            dst.append(on_diagonal(jnp.where(same_pos, jnp.concatenate([t] * P, axis=0), 0.0), g))
        gb = gbcol_ref[0, g]
        bias_v.append(jnp.sum(gb[:C] * tile_c, axis=0, keepdims=True))
        bias_g.append(jnp.sum(gb[C:] * tile_c, axis=0, keepdims=True))
        skip.append(jnp.sum(dcol_ref[0, g] * tile_c, axis=0, keepdims=True))
    kt_ref[0] = jnp.concatenate(kt_rows, axis=0).astype(BF16)
    glu_ref[0] = jnp.concatenate([jnp.concatenate(glu_v, axis=0), jnp.concatenate(glu_g, axis=0)],
                                 axis=1).astype(BF16)
    glub_ref[0] = jnp.concatenate(bias_v + bias_g, axis=1)
    dsk_ref[0] = jnp.concatenate(skip, axis=1)


def _s5_tables(lam_re, lam_im, log_dt, b_re, b_im, c_re, c_im, d_skip, glu_w, glu_b):
    G, N, C, P = S5_GROUPS, S5_STATE, S5_GROUP_CH, S5_BLOCK
    J, NN, PC = G // S5_PAIR, S5_PAIR * S5_STATE, S5_BLOCK * S5_GROUP_CH
    f = lambda a: a.astype(F32)
    dt_full = jnp.broadcast_to(f(log_dt)[:, None], (G, N))
    rows = [a.reshape(J, 1, NN) for a in (f(lam_re), f(lam_im), dt_full)]
    cols = [a.reshape(J, NN, 1) for a in (f(lam_re), f(lam_im), dt_full)]
    bt = [f(a).reshape(J, S5_PAIR, N, C).transpose(0, 3, 1, 2).reshape(J, C, NN) for a in (b_re, b_im)]
    ct = [f(a).transpose(0, 2, 1).reshape(J, NN, C) for a in (c_re, c_im)]
    dcol = f(d_skip).reshape(J, S5_PAIR, C, 1)
    gw1 = f(glu_w)[..., :C].reshape(J, S5_PAIR, C, C)
    gw2 = f(glu_w)[..., C:].reshape(J, S5_PAIR, C, C)
    gbcol = f(glu_b).reshape(J, S5_PAIR, 2 * C, 1)
    ins = rows + cols + bt + ct + [dcol, gw1, gw2, gbcol]
    per_pair = lambda shape: pl.BlockSpec((1,) + shape, lambda j: (j,) + (0,) * len(shape))
    out_shapes = [((S5_PAIR * PC, 2 * NN), BF16), ((S5_PAIR * PC, S5_PAIR * PC), BF16),
                  ((2 * NN, S5_PAIR * PC), BF16), ((S5_PAIR * PC, 2 * S5_PAIR * PC), BF16),
                  ((1, 2 * S5_PAIR * PC), F32), ((1, S5_PAIR * PC), F32),
                  ((S5_SCAN_ROWS, NN), F32), ((S5_SCAN_ROWS, NN), F32)]
    return pl.pallas_call(
        _s5_tables_kernel,
        grid=(J,),
        in_specs=[per_pair(a.shape[1:]) for a in ins],
        out_specs=[per_pair(s) for s, _ in out_shapes],
        out_shape=[jax.ShapeDtypeStruct((J,) + s, d) for s, d in out_shapes],
        compiler_params=pltpu.CompilerParams(dimension_semantics=("parallel",),
                                             vmem_limit_bytes=VMEM_LIMIT_BYTES),
        name="s5_tables",
    )(*ins)


def _s5_kernel(n_steps, u_ref, we_ref, kt_ref, wc_ref, glu_ref, glub_ref, dsk_ref, sr_ref, si_ref, out_ref):
    P, C = S5_BLOCK, S5_GROUP_CH
    nb, slab = u_ref.shape[1], u_ref.shape[2]
    wide = slab * P
    pair_w = S5_PAIR * P * C
    half = S5_PAIR * S5_STATE
    shift_c = C.bit_length() - 1

    lane = lax.broadcasted_iota(jnp.int32, (slab, wide), 1)
    src = lax.broadcasted_iota(jnp.int32, (slab, wide), 0)
    target0 = ((src >> shift_c) << (shift_c + P.bit_length() - 1)) + (src & (C - 1))

    def place(p0):
        return jnp.concatenate([jnp.where(lane == target0 + p * C, 1.0, 0.0).astype(BF16)
                                for p in (p0, p0 + 1)], axis=0)

    u_blk = jnp.zeros((nb, wide), F32)
    for p0 in range(0, P, 2):
        lhs = jnp.concatenate([u_ref[p0], u_ref[p0 + 1]], axis=1)
        u_blk = u_blk + jnp.dot(lhs, place(p0), preferred_element_type=F32)
    u_blk = u_blk.astype(BF16)

    row = lax.broadcasted_iota(jnp.int32, (nb, half), 0)
    outs = []
    for q in range(S5_STEP_PAIRS):
        u = u_blk[:, q * pair_w:(q + 1) * pair_w]
        e = jnp.dot(u, we_ref[q], preferred_element_type=F32)
        xr, xi = e[:, :half], e[:, half:]
        for k in range(n_steps):
            sh = 1 << k
            ar = sr_ref[q, k:k + 1, :]
            ai = si_ref[q, k:k + 1, :]
            valid = row >= sh
            yr = jnp.where(valid, pltpu.roll(xr, sh, axis=0), 0.0)
            yi = jnp.where(valid, pltpu.roll(xi, sh, axis=0), 0.0)
            xr, xi = xr + ar * yr - ai * yi, xi + ar * yi + ai * yr
        first = row >= 1
        pr = jnp.where(first, pltpu.roll(xr, 1, axis=0), 0.0)
        pi = jnp.where(first, pltpu.roll(xi, 1, axis=0), 0.0)
        sprev = jnp.concatenate([pr, pi], axis=1).astype(BF16)
        y = (jnp.dot(u, kt_ref[q], preferred_element_type=F32)
             + jnp.dot(sprev, wc_ref[q], preferred_element_type=F32)
             + dsk_ref[q] * u.astype(F32))
        y = jax.nn.gelu(y)
        z = jnp.dot(y.astype(BF16), glu_ref[q], preferred_element_type=F32) + glub_ref[q]
        outs.append(z[:, :pair_w] * jax.nn.sigmoid(z[:, pair_w:]))
    out_blk = jnp.concatenate(outs, axis=1).astype(BF16)
    for p0 in range(0, P, 2):
        o = lax.dot_general(out_blk, place(p0), _nt_dims(), preferred_element_type=F32)
        out_ref[p0] = o[:, :slab].astype(BF16)
        out_ref[p0 + 1] = o[:, slab:].astype(BF16)


def _s5(u, tables, batch, seq_len):
    nb = seq_len // S5_BLOCK
    n_steps = max(1, (nb - 1).bit_length())
    assert n_steps <= S5_SCAN_ROWS
    slab = pl.BlockSpec((S5_BLOCK, nb, LANES), lambda b, o: (0, b, o))
    per_step = lambda a: pl.BlockSpec((S5_STEP_PAIRS,) + a.shape[1:], lambda b, o: (o, 0, 0))
    return pl.pallas_call(
        functools.partial(_s5_kernel, n_steps),
        grid=(batch, S5_GROUPS // S5_STEP_GROUPS),
        in_specs=[slab] + [per_step(a) for a in tables],
        out_specs=slab,
        out_shape=jax.ShapeDtypeStruct(u.shape, BF16),
        compiler_params=pltpu.CompilerParams(dimension_semantics=("parallel", "parallel"),
                                             vmem_limit_bytes=VMEM_LIMIT_BYTES),
        name="s5",
    )(u, *tables)


def _outproj_kernel(a_ref, b_ref, x_ref, wa_ref, wb_ref, g_ref, rw2_ref, rw1_ref, rgb_ref, reb_ref,
                    x1_ref, h2_ref, meta_ref, metat_ref, cnt_ref, bs_ref):
    tm = x_ref.shape[0]

    @pl.when(pl.program_id(0) == 0)
    def _():
        cnt_ref[...] = jnp.zeros(cnt_ref.shape, F32)

    n_slabs = bs_ref.shape[0]
    for p in range(S5_BLOCK):
        bp = b_ref[p].astype(F32)
        for j in range(n_slabs):
            bs_ref[j, pl.ds(p, tm // S5_BLOCK, stride=S5_BLOCK), :] = bp[:, j * LANES:(j + 1) * LANES]
    out_b = jnp.concatenate([bs_ref[j] for j in range(n_slabs)], axis=1).astype(BF16)
    x1 = (x_ref[...]
          + jnp.dot(a_ref[...], wa_ref[...], preferred_element_type=F32)
          + jnp.dot(out_b, wb_ref[...], preferred_element_type=F32))
    x1_ref[...] = x1
    h2 = x1 * lax.rsqrt(jnp.mean(x1 * x1, axis=-1, keepdims=True) + EPS) * g_ref[...]
    h2_ref[...] = h2

    logits = _dot_split(h2, h2.astype(BF16), rw2_ref, rw1_ref)
    el = logits[:, :N_EXPERTS] + reb_ref[...]
    gl = logits[:, N_EXPERTS:N_EXPERTS + N_GROUPS] + rgb_ref[...]
    gmax = jnp.max(gl, axis=1, keepdims=True)
    g_w = 1.0 / jnp.sum(jnp.exp(gl - gmax), axis=1, keepdims=True)
    lane_g = lax.broadcasted_iota(jnp.int32, (tm, N_GROUPS), 1)
    g_idx = jnp.min(jnp.where(gl == gmax, lane_g, N_GROUPS), axis=1, keepdims=True)
    lane_e = lax.broadcasted_iota(jnp.int32, (tm, N_EXPERTS), 1)
    elm = jnp.where(lane_e // PER_GROUP == g_idx, el, -jnp.inf)
    m1 = jnp.max(elm, axis=1, keepdims=True)
    i1 = jnp.min(jnp.where(elm == m1, lane_e, N_EXPERTS), axis=1, keepdims=True)
    elm2 = jnp.where(lane_e == i1, -jnp.inf, elm)
    m2 = jnp.max(elm2, axis=1, keepdims=True)
    i2 = jnp.min(jnp.where(elm2 == m2, lane_e, N_EXPERTS), axis=1, keepdims=True)
    e2 = jnp.exp(m2 - m1)
    w1 = g_w / (1.0 + e2)
    w2 = g_w * e2 / (1.0 + e2)

    sel1 = lane_e == i1
    sel2 = lane_e == i2
    picked = (sel1 | sel2).astype(BF16)
    rix = lax.broadcasted_iota(jnp.int32, (tm, tm), 0)
    cix = lax.broadcasted_iota(jnp.int32, (tm, tm), 1)
    before = jnp.dot((rix > cix).astype(BF16), picked, preferred_element_type=F32) + cnt_ref[...]
    r1 = jnp.sum(jnp.where(sel1, before, 0.0), axis=1, keepdims=True)
    r2 = jnp.sum(jnp.where(sel2, before, 0.0), axis=1, keepdims=True)
    cnt_ref[...] += jnp.sum(picked.astype(F32), axis=0, keepdims=True)
    col = lax.broadcasted_iota(jnp.int32, (tm, META_COLS), 1)
    meta = jnp.zeros((tm, META_COLS), F32)
    for c, val in enumerate((i1.astype(F32), i2.astype(F32), r1, r2, w1, w2)):
        meta = jnp.where(col == c, val, meta)
    meta_ref[...] = meta
    metat_ref[...] = _transpose_small(meta)


def _outproj(out_a, out_b, x2d, w_out, norm_g, rg_w, rg_b, re_w, re_b):
    T = x2d.shape[0]
    tm = OUTPROJ_TILE
    wa = w_out[:D_MLSTM].astype(BF16)
    wb = w_out[D_MLSTM:].astype(BF16)
    rw2, rw1 = _split_weight(jnp.concatenate([re_w, rg_w], axis=1))
    full = lambda shape: pl.BlockSpec(shape, lambda i: (0,) * len(shape))
    row = lambda n: pl.BlockSpec((tm, n), lambda i: (i, 0))
    return pl.pallas_call(
        _outproj_kernel,
        grid=(T // tm,),
        in_specs=[row(D_MLSTM), pl.BlockSpec((S5_BLOCK, tm // S5_BLOCK, D_S5), lambda i: (0, i, 0)), row(D_MODEL),
                  full((D_MLSTM, D_MODEL)), full((D_S5, D_MODEL)),
                  full((1, D_MODEL)), full((D_MODEL, 2 * LANES)), full((D_MODEL, LANES)), full((1, N_GROUPS)),
                  full((1, N_EXPERTS))],
        out_specs=[row(D_MODEL), row(D_MODEL), row(META_COLS), pl.BlockSpec((META_COLS, tm), lambda i: (0, i)),
                   full((1, N_EXPERTS))],
        out_shape=[jax.ShapeDtypeStruct((T, D_MODEL), F32), jax.ShapeDtypeStruct((T, D_MODEL), F32),
                   jax.ShapeDtypeStruct((T, META_COLS), F32), jax.ShapeDtypeStruct((META_COLS, T), F32),
                   jax.ShapeDtypeStruct((1, N_EXPERTS), F32)],
        scratch_shapes=[pltpu.VMEM((D_S5 // LANES, tm, LANES), F32)],
        compiler_params=pltpu.CompilerParams(dimension_semantics=("arbitrary",),
                                             vmem_limit_bytes=VMEM_LIMIT_BYTES),
        name="outproj",
    )(out_a, out_b, x2d, wa, wb, norm_g.reshape(1, D_MODEL), rw2, rw1, rg_b.reshape(1, -1), re_b.reshape(1, -1))


def _plan_kernel(metat_ref, base_ref, pos_ref):
    mt = metat_ref[...]
    tt = mt.shape[1]
    expert = lax.broadcasted_iota(jnp.int32, (N_EXPERTS, tt), 0)
    base = base_ref[...]
    row = lax.broadcasted_iota(jnp.int32, (META_COLS, tt), 0)
    pos = jnp.zeros((META_COLS, tt), F32)
    for k in range(2):
        first = jnp.sum(jnp.where(expert == mt[k:k + 1, :].astype(jnp.int32), base, 0.0), axis=0, keepdims=True)
        pos = jnp.where(row == k, first + mt[2 + k:3 + k, :], pos)
    pos_ref[...] = pos.astype(jnp.int32)


def _plan(meta_t, base):
    T = meta_t.shape[1]
    tt = PLAN_TILE
    return pl.pallas_call(
        _plan_kernel,
        grid=(T // tt,),
        in_specs=[pl.BlockSpec((META_COLS, tt), lambda i: (0, i)), pl.BlockSpec((N_EXPERTS, 1), lambda i: (0, 0))],
        out_specs=pl.BlockSpec((META_COLS, tt), lambda i: (0, i)),
        out_shape=jax.ShapeDtypeStruct((META_COLS, T), jnp.int32),
        compiler_params=pltpu.CompilerParams(dimension_semantics=("parallel",),
                                             vmem_limit_bytes=VMEM_LIMIT_BYTES),
        name="plan",
    )(meta_t, base.astype(F32).reshape(N_EXPERTS, 1))


def _row_copy(src_ref, src_row, dst_ref, dst_row, sem):
    return pltpu.make_async_copy(src_ref.at[pl.ds(src_row, 1), :], dst_ref.at[pl.ds(dst_row, 1), :], sem)


def _pad_copy(zero_ref, xs_ref, row, sem):
    return pltpu.make_async_copy(zero_ref, xs_ref.at[pl.ds(row, FFN_TILE), :], sem)


def _dispatch_kernel(pad_ref, p1_ref, p2_ref, h_ref, xs_ref, zero_ref, sem, zsem):
    tm = h_ref.shape[0]
    n_tiles = xs_ref.shape[0] // FFN_TILE

    @pl.when(pl.program_id(0) == 0)
    def _():
        zero_ref[...] = jnp.zeros(zero_ref.shape, F32)
        for e in range(N_EXPERTS):
            _pad_copy(zero_ref, xs_ref, pl.multiple_of(pad_ref[e], 8), zsem).start()
        for e in range(N_EXPERTS):
            _pad_copy(zero_ref, xs_ref, 0, zsem).wait()
        first_free = pad_ref[N_EXPERTS]

        def start_tail(k, carry):
            _pad_copy(zero_ref, xs_ref, pl.multiple_of(k * FFN_TILE, FFN_TILE), zsem).start()
            return carry

        def wait_tail(k, carry):
            _pad_copy(zero_ref, xs_ref, 0, zsem).wait()
            return carry

        lax.fori_loop(first_free, n_tiles, start_tail, 0)
        lax.fori_loop(first_free, n_tiles, wait_tail, 0)

    def issue(t, carry):
        _row_copy(h_ref, t, xs_ref, p1_ref[0, 0, t], sem).start(priority=0)
        _row_copy(h_ref, t, xs_ref, p2_ref[0, 0, t], sem).start(priority=1)
        return carry

    lax.fori_loop(0, tm, issue, 0, unroll=ROW_DMA_UNROLL)

    def drain(t, carry):
        _row_copy(h_ref, 0, xs_ref, 0, sem).wait()
        _row_copy(h_ref, 0, xs_ref, 0, sem).wait()
        return carry

    lax.fori_loop(0, tm, drain, 0, unroll=ROW_DMA_UNROLL)


def _dispatch(h2, pos1, pos2, pad_start, n_rows):
    T, w = h2.shape
    tm = DISPATCH_TILE
    smem_row = pl.BlockSpec((1, 1, tm), lambda i, pad: (i, 0, 0), memory_space=pltpu.SMEM)
    return pl.pallas_call(
        _dispatch_kernel,
        grid_spec=pltpu.PrefetchScalarGridSpec(
            num_scalar_prefetch=1,
            grid=(T // tm,),
            in_specs=[smem_row, smem_row, pl.BlockSpec((tm, w), lambda i, pad: (i, 0))],
            out_specs=pl.BlockSpec(memory_space=pl.ANY),
            scratch_shapes=[pltpu.VMEM((FFN_TILE, w), F32), pltpu.SemaphoreType.DMA, pltpu.SemaphoreType.DMA]),
        out_shape=jax.ShapeDtypeStruct((n_rows + FFN_TILE, w), F32),
        compiler_params=pltpu.CompilerParams(dimension_semantics=("arbitrary",),
                                             vmem_limit_bytes=VMEM_LIMIT_BYTES),
        name="dispatch",
    )(pad_start, pos1.reshape(T // tm, 1, tm), pos2.reshape(T // tm, 1, tm), h2)


def _tile_rows(ref, tile):
    return ref.at[pl.ds(pl.multiple_of(tile * FFN_TILE, FFN_TILE), FFN_TILE), :]


def _tile_in(xs_ref, tile, xbuf_ref, slot, sem_ref):
    return pltpu.make_async_copy(_tile_rows(xs_ref, tile), xbuf_ref.at[slot], sem_ref.at[slot])


def _tile_out(ybuf_ref, slot, ys_ref, tile, sem_ref):
    return pltpu.make_async_copy(ybuf_ref.at[slot], _tile_rows(ys_ref, tile), sem_ref.at[slot])


def _ffn_kernel(first_ref, ntile_ref, nused_ref, xs_ref, wg_ref, wu_ref, wd_ref, ys_ref,
                xbuf_ref, ybuf_ref, wgb_ref, wub_ref, wdb_ref, isem, osem):
    e = pl.program_id(0)
    t0 = first_ref[e]
    n = ntile_ref[e]
    n_tiles = ys_ref.shape[0] // FFN_TILE

    @pl.when(n > 0)
    def _():
        wgb_ref[...] = wg_ref[0].astype(BF16)
        wub_ref[...] = wu_ref[0].astype(BF16)
        wdb_ref[...] = wd_ref[0].astype(BF16)
        _tile_in(xs_ref, t0, xbuf_ref, 0, isem).start()

        def two_tiles(k2, carry):
            for slot in (0, 1):
                k = 2 * k2 + slot

                @pl.when(k < n)
                def _():
                    _tile_in(xs_ref, t0 + k, xbuf_ref, slot, isem).wait()

                    @pl.when(k + 1 < n)
                    def _():
                        _tile_in(xs_ref, t0 + k + 1, xbuf_ref, 1 - slot, isem).start()

                    @pl.when(k >= 2)
                    def _():
                        _tile_out(ybuf_ref, slot, ys_ref, t0 + k - 2, osem).wait()

                    x = xbuf_ref[slot].astype(BF16)
                    a = jnp.dot(x, wgb_ref[...], preferred_element_type=F32)
                    b = jnp.dot(x, wub_ref[...], preferred_element_type=F32)
                    act = (a * jax.nn.sigmoid(a) * b).astype(BF16)
                    ybuf_ref[slot] = jnp.dot(act, wdb_ref[...], preferred_element_type=F32)
                    _tile_out(ybuf_ref, slot, ys_ref, t0 + k, osem).start()
            return carry

        lax.fori_loop(0, (n + 1) // 2, two_tiles, 0)
        for slot in (0, 1):
            @pl.when(n > slot)
            def _():
                _tile_out(ybuf_ref, slot, ys_ref, t0, osem).wait()

    @pl.when(e == pl.num_programs(0) - 1)
    def _():
        ybuf_ref[0] = jnp.zeros(ybuf_ref.shape[1:], F32)

        def start_tail(k, carry):
            _tile_out(ybuf_ref, 0, ys_ref, k, osem).start()
            return carry

        def wait_tail(k, carry):
            _tile_out(ybuf_ref, 0, ys_ref, k, osem).wait()
            return carry

        lax.fori_loop(nused_ref[0], n_tiles, start_tail, 0)
        lax.fori_loop(nused_ref[0], n_tiles, wait_tail, 0)


def _ffn(xs, first_tile, n_tile, n_used, n_rows, w_gate, w_up, w_down):
    w = xs.shape[1]
    wspec = lambda shape: pl.BlockSpec((1,) + shape, lambda e, *_: (e, 0, 0))
    return pl.pallas_call(
        _ffn_kernel,
        grid_spec=pltpu.PrefetchScalarGridSpec(
            num_scalar_prefetch=3,
            grid=(N_EXPERTS,),
            in_specs=[pl.BlockSpec(memory_space=pl.ANY),
                      wspec((D_MODEL, D_EXPERT)), wspec((D_MODEL, D_EXPERT)), wspec((D_EXPERT, D_MODEL))],
            out_specs=pl.BlockSpec(memory_space=pl.ANY),
            scratch_shapes=[pltpu.VMEM((2, FFN_TILE, w), F32), pltpu.VMEM((2, FFN_TILE, w), F32),
                            pltpu.VMEM((D_MODEL, D_EXPERT), BF16), pltpu.VMEM((D_MODEL, D_EXPERT), BF16),
                            pltpu.VMEM((D_EXPERT, D_MODEL), BF16),
                            pltpu.SemaphoreType.DMA((2,)), pltpu.SemaphoreType.DMA((2,))]),
        out_shape=jax.ShapeDtypeStruct((n_rows, w), F32),
        compiler_params=pltpu.CompilerParams(dimension_semantics=("arbitrary",),
                                             vmem_limit_bytes=VMEM_LIMIT_BYTES),
        name="ffn",
    )(first_tile, n_tile, n_used, xs, w_gate, w_up, w_down)


def _combine_kernel(p1_ref, p2_ref, ys_ref, meta_ref, x1_ref, gf_ref, out_ref, buf_ref, sem):
    tm = x1_ref.shape[0]

    def issue(t, carry):
        _row_copy(ys_ref, p1_ref[0, 0, t], buf_ref.at[0], t, sem).start(priority=0)
        _row_copy(ys_ref, p2_ref[0, 0, t], buf_ref.at[1], t, sem).start(priority=1)
        return carry

    lax.fori_loop(0, tm, issue, 0, unroll=ROW_DMA_UNROLL)

    def drain(t, carry):
        _row_copy(ys_ref, 0, buf_ref.at[0], 0, sem).wait()
        _row_copy(ys_ref, 0, buf_ref.at[1], 0, sem).wait()
        return carry

    lax.fori_loop(0, tm, drain, 0, unroll=ROW_DMA_UNROLL)

    meta = meta_ref[...]
    y = x1_ref[...] + meta[:, 4:5] * buf_ref[0] + meta[:, 5:6] * buf_ref[1]
    out_ref[...] = y * lax.rsqrt(jnp.mean(y * y, axis=-1, keepdims=True) + EPS) * gf_ref[...]


def _combine(ys, pos1, pos2, meta, x1, norm_final_g):
    T = x1.shape[0]
    w = ys.shape[1]
    tm = COMBINE_TILE
    smem_row = pl.BlockSpec((1, 1, tm), lambda i: (i, 0, 0), memory_space=pltpu.SMEM)
    row = lambda n: pl.BlockSpec((tm, n), lambda i: (i, 0))
    return pl.pallas_call(
        _combine_kernel,
        grid=(T // tm,),
        in_specs=[smem_row, smem_row, pl.BlockSpec(memory_space=pl.ANY), row(META_COLS), row(D_MODEL),
                  pl.BlockSpec((1, D_MODEL), lambda i: (0, 0))],
        out_specs=row(D_MODEL),
        out_shape=jax.ShapeDtypeStruct((T, D_MODEL), F32),
        scratch_shapes=[pltpu.VMEM((2, tm, w), F32), pltpu.SemaphoreType.DMA],
        compiler_params=pltpu.CompilerParams(dimension_semantics=("arbitrary",),
                                             vmem_limit_bytes=VMEM_LIMIT_BYTES),
        name="combine",
    )(pos1.reshape(T // tm, 1, tm), pos2.reshape(T // tm, 1, tm), ys, meta, x1,
      norm_final_g.reshape(1, D_MODEL))


def _run_layout(counts):
    cnt = counts.reshape(-1).astype(jnp.int32)
    padded = (cnt + FFN_TILE - 1) // FFN_TILE * FFN_TILE
    ends = jnp.cumsum(padded)
    base = ends - padded
    n_used = (ends[-1:] // FFN_TILE).astype(jnp.int32)
    pad_start = jnp.concatenate([base + cnt // 8 * 8, n_used])
    return base, pad_start, base // FFN_TILE, padded // FFN_TILE, n_used


def kernel(x, norm_mix_g, w_in, conv_w, conv_b, i_bias, f_bias, mlstm_norm_g, s5_lambda_re, s5_lambda_im,
           s5_log_dt, s5_b_re, s5_b_im, s5_c_re, s5_c_im, s5_d, s5_glu_w, s5_glu_b, w_out, norm_ffn_g,
           router_group_w, router_group_b, router_expert_w, router_expert_b, expert_w_gate, expert_w_up,
           expert_w_down, norm_final_g):
    batch, seq_len, d_model = x.shape
    assert d_model == D_MODEL and norm_mix_g.shape[0] == 1
    assert seq_len % MLSTM_CHUNK == 0 and seq_len % INPROJ_TILE == 0 and seq_len % S5_BLOCK == 0
    T = batch * seq_len
    x2d = x.reshape(T, D_MODEL)

    q, k, v, og, u, gcol, grow = _inproj(x2d, norm_mix_g[0], w_in[0], conv_w[0], conv_b[0], i_bias[0],
                                         f_bias[0], seq_len)
    out_a = _mlstm(q, k, v, og, gcol, grow, mlstm_norm_g[0], batch, seq_len)
    tables = _s5_tables(s5_lambda_re[0], s5_lambda_im[0], s5_log_dt[0], s5_b_re[0], s5_b_im[0], s5_c_re[0],
                        s5_c_im[0], s5_d[0], s5_glu_w[0], s5_glu_b[0])
    out_b = _s5(u, tables, batch, seq_len)
    x1, h2, meta, meta_t, counts = _outproj(out_a, out_b, x2d, w_out[0], norm_ffn_g[0], router_group_w[0],
                                            router_group_b[0], router_expert_w[0], router_expert_b[0])
    n_rows = 2 * T + N_EXPERTS * FFN_TILE
    base, pad_start, first_tile, n_tile, n_used = _run_layout(counts)
    pos = _plan(meta_t, base)
    xs = _dispatch(h2, pos[0], pos[1], pad_start, n_rows)
    ys = _ffn(xs, first_tile, n_tile, n_used, n_rows, expert_w_gate[0], expert_w_up[0], expert_w_down[0])
    y = _combine(ys, pos[0], pos[1], meta, x1, norm_final_g)
    return y.reshape(batch, seq_len, D_MODEL)
```

```python
import functools

import jax
import jax.numpy as jnp
from jax import lax
from jax.experimental import pallas as pl
from jax.experimental.pallas import tpu as pltpu

F32 = jnp.float32
BF16 = jnp.bfloat16
HIGHEST = lax.Precision.HIGHEST

D_MODEL = 1024
D_MLSTM = 512
D_S5 = 512
HEADS = 4
HEAD_DIM = 128
CONV_WIDTH = 4
S5_GROUP_CH = 16
S5_GROUPS = 32
S5_STATE = 64
N_GROUPS = 4
PER_GROUP = 8
N_EXPERTS = 32
D_EXPERT = 512
EPS = 1e-6
LAMBDA_RE_MAX = -1e-4

VMEM_LIMIT_BYTES = 56 * 1024 * 1024
LANES = 128
SUBLANES = 8
ROW_SLABS = D_MODEL // LANES
assert ROW_SLABS == SUBLANES

INPROJ_TILE = 512
MLSTM_CHUNK = 256
S5_BLOCK = 16
S5_PAIR = 2
S5_STEP_GROUPS = LANES // S5_GROUP_CH
S5_STEP_PAIRS = S5_STEP_GROUPS // S5_PAIR
S5_SCAN_ROWS = 16
OUTPROJ_TILE = 512
META_COLS = 8
PLAN_TILE = 2048
DISPATCH_TILE = 512
FFN_TILE = 256
COMBINE_TILE = 256
ROW_DMA_UNROLL = 8


def _nt_dims():
    return (((1,), (1,)), ((), ()))


def _tn_dims():
    return (((0,), (0,)), ((), ()))


def _log_sigmoid(x):
    return jnp.minimum(x, 0.0) - jnp.log1p(jnp.exp(-jnp.abs(x)))


def _split_weight(w):
    w = jnp.pad(w.astype(F32), ((0, 0), (0, LANES - w.shape[1])))
    hi = w.astype(BF16)
    lo = (w - hi.astype(F32)).astype(BF16)
    return jnp.concatenate([hi, lo], axis=1), hi


def _dot_split(x, x_hi, w_hilo_ref, w_hi_ref):
    x_lo = (x - x_hi.astype(F32)).astype(BF16)
    two = jnp.dot(x_hi, w_hilo_ref[...], preferred_element_type=F32)
    return two[:, :LANES] + two[:, LANES:] + jnp.dot(x_lo, w_hi_ref[...], preferred_element_type=F32)


def _store_slabs(ref, value, lead=()):
    m = value.shape[0]
    for s in range(ROW_SLABS):
        ref[lead + (pl.ds(s, m, stride=ROW_SLABS), slice(None))] = value[:, s * LANES:(s + 1) * LANES]


def _load_slabs(ref, m, lead=()):
    return jnp.concatenate([ref[lead + (pl.ds(s, m, stride=ROW_SLABS), slice(None))] for s in range(ROW_SLABS)],
                           axis=1)


def _transpose_small(a):
    n = a.shape[1]
    eye = (lax.broadcasted_iota(jnp.int32, (n, n), 0) == lax.broadcasted_iota(jnp.int32, (n, n), 1)).astype(F32)
    return lax.dot_general(eye, a, _nt_dims(), precision=HIGHEST, preferred_element_type=F32)


def _inproj_kernel(tiles_per_seq, x_ref, g_ref, wqk_ref, wv_ref, wo_ref, wu_ref, wg2_ref, wg1_ref,
                   gb_ref, cw_ref, cb_ref,
                   q_ref, k_ref, v_ref, og_ref, u_ref, gcol_ref, grow_ref, ext_ref, us_ref):
    i = pl.program_id(0)
    tm = x_ref.shape[0]
    x = x_ref[...]
    h = x * lax.rsqrt(jnp.mean(x * x, axis=-1, keepdims=True) + EPS) * g_ref[...]
    hb = h.astype(BF16)

    qk_raw = jnp.dot(hb, wqk_ref[...], preferred_element_type=F32)

    @pl.when(i % tiles_per_seq == 0)
    def _():
        ext_ref[0:8, :] = jnp.zeros((8, ext_ref.shape[1]), F32)

    ext_ref[8:, :] = qk_raw
    cw = cw_ref[...]
    acc = cb_ref[...] + cw[CONV_WIDTH - 1:CONV_WIDTH, :] * qk_raw
    for s in range(1, CONV_WIDTH):
        acc = acc + cw[CONV_WIDTH - 1 - s:CONV_WIDTH - s, :] * ext_ref[pl.ds(8 - s, tm), :]
    ext_ref[0:8, :] = qk_raw[tm - 8:, :]
    qk = acc * jax.nn.sigmoid(acc)
    q_ref[...] = qk[:, :D_MLSTM].astype(BF16)
    k_ref[...] = (qk[:, D_MLSTM:] * (HEAD_DIM ** -0.5)).astype(BF16)

    v_ref[...] = jnp.dot(hb, wv_ref[...], preferred_element_type=F32).astype(BF16)
    o_pre = jnp.dot(hb, wo_ref[...], preferred_element_type=F32)
    og_ref[...] = jax.nn.sigmoid(o_pre).astype(BF16)

    u = jnp.dot(hb, wu_ref[...], preferred_element_type=F32)
    n_slabs = us_ref.shape[0]
    for j in range(n_slabs):
        us_ref[j] = u[:, j * LANES:(j + 1) * LANES]
    for p in range(S5_BLOCK):
        u_ref[p] = jnp.concatenate([us_ref[j, pl.ds(p, tm // S5_BLOCK, stride=S5_BLOCK), :]
                                    for j in range(n_slabs)], axis=1).astype(BF16)

    gcol = _dot_split(h, hb, wg2_ref, wg1_ref)[:, :2 * HEADS] + gb_ref[...]
    gcol_ref[...] = gcol
    grow_ref[...] = _transpose_small(gcol)


def _inproj(x2d, norm_g, w_in, conv_w, conv_b, i_bias, f_bias, seq_len):
    T = x2d.shape[0]
    tm = INPROJ_TILE
    dm = D_MLSTM
    wqk = w_in[:, :2 * dm].astype(BF16)
    wv = w_in[:, 2 * dm:3 * dm].astype(BF16)
    wo = w_in[:, 3 * dm:4 * dm].astype(BF16)
    wg2, wg1 = _split_weight(w_in[:, 4 * dm:4 * dm + 2 * HEADS])
    wu = w_in[:, 4 * dm + 2 * HEADS:].astype(BF16)
    gb = jnp.concatenate([i_bias, f_bias]).astype(F32)
    full = lambda shape: pl.BlockSpec(shape, lambda i: (0,) * len(shape))
    row = lambda n: pl.BlockSpec((tm, n), lambda i: (i, 0))
    return pl.pallas_call(
        functools.partial(_inproj_kernel, seq_len // tm),
        grid=(T // tm,),
        in_specs=[row(D_MODEL), full((1, D_MODEL)), full((D_MODEL, 2 * dm)), full((D_MODEL, dm)),
                  full((D_MODEL, dm)), full((D_MODEL, D_S5)), full((D_MODEL, 2 * LANES)),
                  full((D_MODEL, LANES)), full((1, 2 * HEADS)),
                  full((CONV_WIDTH, 2 * dm)), full((1, 2 * dm))],
        out_specs=[row(dm), row(dm), row(dm), row(dm),
                   pl.BlockSpec((S5_BLOCK, tm // S5_BLOCK, D_S5), lambda i: (0, i, 0)), row(2 * HEADS),
                   pl.BlockSpec((2 * HEADS, tm), lambda i: (0, i))],
        out_shape=[jax.ShapeDtypeStruct((T, dm), BF16)] * 4
        + [jax.ShapeDtypeStruct((S5_BLOCK, T // S5_BLOCK, D_S5), BF16),
           jax.ShapeDtypeStruct((T, 2 * HEADS), F32),
           jax.ShapeDtypeStruct((2 * HEADS, T), F32)],
        scratch_shapes=[pltpu.VMEM((tm + 8, 2 * dm), F32), pltpu.VMEM((D_S5 // LANES, tm, LANES), F32)],
        compiler_params=pltpu.CompilerParams(dimension_semantics=("arbitrary",),
                                             vmem_limit_bytes=VMEM_LIMIT_BYTES),
        name="inproj",
    )(x2d, norm_g.reshape(1, D_MODEL), wqk, wv, wo, wu, wg2, wg1, gb.reshape(1, -1),
      conv_w, conv_b.reshape(1, -1))


def _mlstm_kernel(q_ref, k_ref, v_ref, og_ref, gcol_ref, grow_ref, ng_ref, out_ref, c_ref, m_ref):
    tc = q_ref.shape[0]
    d = HEAD_DIM

    @pl.when(pl.program_id(1) == 0)
    def _():
        c_ref[...] = jnp.zeros(c_ref.shape, F32)
        m_ref[...] = jnp.zeros(m_ref.shape, F32)

    gcol = gcol_ref[...]
    grow = grow_ref[...]
    rix = lax.broadcasted_iota(jnp.int32, (tc, tc), 0)
    cix = lax.broadcasted_iota(jnp.int32, (tc, tc), 1)
    causal = rix >= cix
    b_col_all = jnp.dot(causal.astype(F32), _log_sigmoid(gcol), precision=HIGHEST,
                        preferred_element_type=F32)
    b_row_all = jnp.dot(_log_sigmoid(grow), (rix <= cix).astype(F32), precision=HIGHEST,
                        preferred_element_type=F32)
    ones_col = (lax.broadcasted_iota(jnp.int32, (tc, d), 1) == 0).astype(BF16)

    for h in range(HEADS):
        hs = slice(h * d, (h + 1) * d)
        li_col = gcol[:, h:h + 1]
        li_row = grow[h:h + 1, :]
        b_col = b_col_all[:, HEADS + h:HEADS + h + 1]
        b_row = b_row_all[HEADS + h:HEADS + h + 1, :]
        m_prev = m_ref[h:h + 1, 0:1]

        dmat = jnp.where(causal, b_col - b_row + li_row, -jnp.inf)
        inter = b_col + m_prev
        m_row = jnp.maximum(inter, jnp.max(dmat, axis=1, keepdims=True))
        w_intra = jnp.exp(dmat - m_row)
        w_inter = jnp.exp(inter - m_row)

        qh = q_ref[:, hs]
        kh = k_ref[:, hs]
        v_aug = jnp.concatenate([v_ref[:, hs], ones_col], axis=1)
        s = lax.dot_general(qh, kh, _nt_dims(), preferred_element_type=F32) * w_intra
        c_aug = c_ref[h]
        tot = (jnp.dot(s.astype(BF16), v_aug, preferred_element_type=F32)
               + w_inter * lax.dot_general(qh, c_aug.astype(BF16), _nt_dims(), preferred_element_type=F32))
        num = tot[:, :d]
        den = tot[:, d:d + 1]
        hh = num / jnp.maximum(jnp.abs(den), jnp.exp(-m_row))
        hh = hh * lax.rsqrt(jnp.mean(hh * hh, axis=-1, keepdims=True) + EPS)
        out_ref[:, hs] = (og_ref[:, hs].astype(F32) * (hh * ng_ref[:, hs])).astype(BF16)

        b_last = b_col[tc - 1:tc, :]
        d_state = b_last - b_col + li_col
        m_new = jnp.maximum(b_last + m_prev, jnp.max(d_state, axis=0, keepdims=True))
        w_s = jnp.exp(d_state - m_new)
        w_c = jnp.exp(b_last + m_prev - m_new)
        upd = lax.dot_general((v_aug.astype(F32) * w_s).astype(BF16), kh, _tn_dims(),
                              preferred_element_type=F32)
        c_ref[h] = w_c * c_aug + upd
        m_ref[h:h + 1, :] = jnp.broadcast_to(m_new, (1, m_ref.shape[1]))


def _mlstm(q, k, v, og, gcol, grow, norm_g, batch, seq_len):
    tc = MLSTM_CHUNK
    nc = seq_len // tc
    T = q.shape[0]
    row = lambda n: pl.BlockSpec((tc, n), lambda b, c: (b * nc + c, 0))
    return pl.pallas_call(
        _mlstm_kernel,
        grid=(batch, nc),
        in_specs=[row(D_MLSTM), row(D_MLSTM), row(D_MLSTM), row(D_MLSTM), row(2 * HEADS),
                  pl.BlockSpec((2 * HEADS, tc), lambda b, c: (0, b * nc + c)),
                  pl.BlockSpec((1, D_MLSTM), lambda b, c: (0, 0))],
        out_specs=row(D_MLSTM),
        out_shape=jax.ShapeDtypeStruct((T, D_MLSTM), BF16),
        scratch_shapes=[pltpu.VMEM((HEADS, 2 * HEAD_DIM, HEAD_DIM), F32), pltpu.VMEM((8, LANES), F32)],
        compiler_params=pltpu.CompilerParams(dimension_semantics=("parallel", "arbitrary"),
                                             vmem_limit_bytes=VMEM_LIMIT_BYTES),
        name="mlstm",
    )(q, k, v, og, gcol, grow, norm_g.reshape(1, D_MLSTM))


def _cplx_pow(exponent, theta_re, theta_im):
    mag = jnp.exp(exponent * theta_re)
    return mag * jnp.cos(exponent * theta_im), mag * jnp.sin(exponent * theta_im)


def _s5_tables_kernel(lrr_ref, lir_ref, dtr_ref, lrc_ref, lic_ref, dtc_ref, btr_ref, bti_ref, ctr_ref, cti_ref,
                      dcol_ref, gw1_ref, gw2_ref, gbcol_ref,
                      we_ref, kt_ref, wc_ref, glu_ref, glub_ref, dsk_ref, sr_ref, si_ref):
    P, C, N = S5_BLOCK, S5_GROUP_CH, S5_STATE
    PC, NN = P * C, S5_PAIR * S5_STATE
    shift_c, shift_n = C.bit_length() - 1, N.bit_length() - 1

    lr = jnp.minimum(lrr_ref[0], LAMBDA_RE_MAX)
    li = lir_ref[0]
    dt = jnp.exp(dtr_ref[0])
    th_r, th_i = lr * dt, li * dt
    ar, ai = _cplx_pow(1.0, th_r, th_i)
    den = lr * lr + li * li
    cr = ((ar - 1.0) * lr + ai * li) / den
    ci = (ai * lr - (ar - 1.0) * li) / den
    btr, bti = btr_ref[0], bti_ref[0]
    bbr = cr * btr - ci * bti
    bbi = cr * bti + ci * btr

    back = (P - 1 - lax.broadcasted_iota(jnp.int32, (P, 1), 0)).astype(F32)
    pwr, pwi = _cplx_pow(back, th_r, th_i)
    val_r = jnp.concatenate([pwr[p:p + 1] * bbr - pwi[p:p + 1] * bbi for p in range(P)], axis=0)
    val_i = jnp.concatenate([pwr[p:p + 1] * bbi + pwi[p:p + 1] * bbr for p in range(P)], axis=0)
    lane_grp = lax.broadcasted_iota(jnp.int32, (PC, NN), 1) >> shift_n
    we_r = jnp.concatenate([jnp.where(lane_grp == g, val_r, 0.0) for g in range(S5_PAIR)], axis=0)
    we_i = jnp.concatenate([jnp.where(lane_grp == g, val_i, 0.0) for g in range(S5_PAIR)], axis=0)
    we_ref[0] = jnp.concatenate([we_r, we_i], axis=1).astype(BF16)

    steps = (P << lax.broadcasted_iota(jnp.int32, (S5_SCAN_ROWS, 1), 0)).astype(F32)
    sr_ref[0], si_ref[0] = _cplx_pow(steps, th_r, th_i)

    lrc = jnp.minimum(lrc_ref[0], LAMBDA_RE_MAX)
    dtc = jnp.exp(dtc_ref[0])
    thc_r, thc_i = lrc * dtc, lic_ref[0] * dtc
    lane = lax.broadcasted_iota(jnp.int32, (NN, PC), 1)
    p0r, p0i = _cplx_pow((lane >> shift_c).astype(F32), thc_r, thc_i)
    acr, aci = _cplx_pow(1.0, thc_r, thc_i)
    p1r, p1i = p0r * acr - p0i * aci, p0r * aci + p0i * acr
    tile_c = ((lax.broadcasted_iota(jnp.int32, (C, PC), 1) & (C - 1))
              == lax.broadcasted_iota(jnp.int32, (C, PC), 0)).astype(F32)
    ctr = jnp.dot(ctr_ref[0], tile_c, precision=HIGHEST, preferred_element_type=F32)
    cti = jnp.dot(cti_ref[0], tile_c, precision=HIGHEST, preferred_element_type=F32)
    w0r, w0i = ctr * p0r - cti * p0i, ctr * p0i + cti * p0r
    vr, vi = ctr * p1r - cti * p1i, -(ctr * p1i + cti * p1r)
    row_grp = lax.broadcasted_iota(jnp.int32, (NN, PC), 0) >> shift_n
    wc_r = jnp.concatenate([jnp.where(row_grp == g, vr, 0.0) for g in range(S5_PAIR)], axis=1)
    wc_i = jnp.concatenate([jnp.where(row_grp == g, vi, 0.0) for g in range(S5_PAIR)], axis=1)
    wc_ref[0] = jnp.concatenate([wc_r, wc_i], axis=0).astype(BF16)

    b_grp = lax.broadcasted_iota(jnp.int32, (C, NN), 1) >> shift_n
    lane_k = lax.broadcasted_iota(jnp.int32, (C, PC), 1)
    same_pos = ((lax.broadcasted_iota(jnp.int32, (PC, PC), 0) >> shift_c)
                == (lax.broadcasted_iota(jnp.int32, (PC, PC), 1) >> shift_c))
    zero_blk = jnp.zeros((PC, PC), F32)

    def on_diagonal(blk, g):
        return jnp.concatenate([blk if j == g else zero_blk for j in range(S5_PAIR)], axis=1)

    kt_rows, glu_v, glu_g, bias_v, bias_g, skip = [], [], [], [], [], []
    for g in range(S5_PAIR):
        k0 = (jnp.dot(jnp.where(b_grp == g, bbr, 0.0), w0r, precision=HIGHEST, preferred_element_type=F32)
              - jnp.dot(jnp.where(b_grp == g, bbi, 0.0), w0i, precision=HIGHEST, preferred_element_type=F32))
        toep = jnp.concatenate(
            [k0 if p == 0 else jnp.where(lane_k >= C * p, pltpu.roll(k0, C * p, axis=1), 0.0) for p in range(P)],
            axis=0)
        kt_rows.append(on_diagonal(toep, g))
        for w_ref, dst in ((gw1_ref, glu_v), (gw2_ref, glu_g)):
            t = jnp.dot(w_ref[0, g], tile_c, precision=HIGHEST, preferred_element_type=F32)
            dst.append(on_diagonal(jnp.where(same_pos, jnp.concatenate([t] * P, axis=0), 0.0), g))
        gb = gbcol_ref[0, g]
        bias_v.append(jnp.sum(gb[:C] * tile_c, axis=0, keepdims=True))
        bias_g.append(jnp.sum(gb[C:] * tile_c, axis=0, keepdims=True))
        skip.append(jnp.sum(dcol_ref[0, g] * tile_c, axis=0, keepdims=True))
    kt_ref[0] = jnp.concatenate(kt_rows, axis=0).astype(BF16)
    glu_ref[0] = jnp.concatenate([jnp.concatenate(glu_v, axis=0), jnp.concatenate(glu_g, axis=0)],
                                 axis=1).astype(BF16)
    glub_ref[0] = jnp.concatenate(bias_v + bias_g, axis=1)
    dsk_ref[0] = jnp.concatenate(skip, axis=1)


def _s5_tables(lam_re, lam_im, log_dt, b_re, b_im, c_re, c_im, d_skip, glu_w, glu_b):
    G, N, C, P = S5_GROUPS, S5_STATE, S5_GROUP_CH, S5_BLOCK
    J, NN, PC = G // S5_PAIR, S5_PAIR * S5_STATE, S5_BLOCK * S5_GROUP_CH
    f = lambda a: a.astype(F32)
    dt_full = jnp.broadcast_to(f(log_dt)[:, None], (G, N))
    rows = [a.reshape(J, 1, NN) for a in (f(lam_re), f(lam_im), dt_full)]
    cols = [a.reshape(J, NN, 1) for a in (f(lam_re), f(lam_im), dt_full)]
    bt = [f(a).reshape(J, S5_PAIR, N, C).transpose(0, 3, 1, 2).reshape(J, C, NN) for a in (b_re, b_im)]
    ct = [f(a).transpose(0, 2, 1).reshape(J, NN, C) for a in (c_re, c_im)]
    dcol = f(d_skip).reshape(J, S5_PAIR, C, 1)
    gw1 = f(glu_w)[..., :C].reshape(J, S5_PAIR, C, C)
    gw2 = f(glu_w)[..., C:].reshape(J, S5_PAIR, C, C)
    gbcol = f(glu_b).reshape(J, S5_PAIR, 2 * C, 1)
    ins = rows + cols + bt + ct + [dcol, gw1, gw2, gbcol]
    per_pair = lambda shape: pl.BlockSpec((1,) + shape, lambda j: (j,) + (0,) * len(shape))
    out_shapes = [((S5_PAIR * PC, 2 * NN), BF16), ((S5_PAIR * PC, S5_PAIR * PC), BF16),
                  ((2 * NN, S5_PAIR * PC), BF16), ((S5_PAIR * PC, 2 * S5_PAIR * PC), BF16),
                  ((1, 2 * S5_PAIR * PC), F32), ((1, S5_PAIR * PC), F32),
                  ((S5_SCAN_ROWS, NN), F32), ((S5_SCAN_ROWS, NN), F32)]
    return pl.pallas_call(
        _s5_tables_kernel,
        grid=(J,),
        in_specs=[per_pair(a.shape[1:]) for a in ins],
        out_specs=[per_pair(s) for s, _ in out_shapes],
        out_shape=[jax.ShapeDtypeStruct((J,) + s, d) for s, d in out_shapes],
        compiler_params=pltpu.CompilerParams(dimension_semantics=("parallel",),
                                             vmem_limit_bytes=VMEM_LIMIT_BYTES),
        name="s5_tables",
    )(*ins)


def _s5_kernel(n_steps, u_ref, we_ref, kt_ref, wc_ref, glu_ref, glub_ref, dsk_ref, sr_ref, si_ref, out_ref):
    P, C = S5_BLOCK, S5_GROUP_CH
    nb, slab = u_ref.shape[1], u_ref.shape[2]
    wide = slab * P
    pair_w = S5_PAIR * P * C
    half = S5_PAIR * S5_STATE
    shift_c = C.bit_length() - 1

    lane = lax.broadcasted_iota(jnp.int32, (slab, wide), 1)
    src = lax.broadcasted_iota(jnp.int32, (slab, wide), 0)
    target0 = ((src >> shift_c) << (shift_c + P.bit_length() - 1)) + (src & (C - 1))

    def place(p0):
        return jnp.concatenate([jnp.where(lane == target0 + p * C, 1.0, 0.0).astype(BF16)
                                for p in (p0, p0 + 1)], axis=0)

    u_blk = jnp.zeros((nb, wide), F32)
    for p0 in range(0, P, 2):
        lhs = jnp.concatenate([u_ref[p0], u_ref[p0 + 1]], axis=1)
        u_blk = u_blk + jnp.dot(lhs, place(p0), preferred_element_type=F32)
    u_blk = u_blk.astype(BF16)

    row = lax.broadcasted_iota(jnp.int32, (nb, half), 0)
    outs = []
    for q in range(S5_STEP_PAIRS):
        u = u_blk[:, q * pair_w:(q + 1) * pair_w]
        e = jnp.dot(u, we_ref[q], preferred_element_type=F32)
        xr, xi = e[:, :half], e[:, half:]
        for k in range(n_steps):
            sh = 1 << k
            ar = sr_ref[q, k:k + 1, :]
            ai = si_ref[q, k:k + 1, :]
            valid = row >= sh
            yr = jnp.where(valid, pltpu.roll(xr, sh, axis=0), 0.0)
            yi = jnp.where(valid, pltpu.roll(xi, sh, axis=0), 0.0)
            xr, xi = xr + ar * yr - ai * yi, xi + ar * yi + ai * yr
        first = row >= 1
        pr = jnp.where(first, pltpu.roll(xr, 1, axis=0), 0.0)
        pi = jnp.where(first, pltpu.roll(xi, 1, axis=0), 0.0)
        sprev = jnp.concatenate([pr, pi], axis=1).astype(BF16)
        y = (jnp.dot(u, kt_ref[q], preferred_element_type=F32)
             + jnp.dot(sprev, wc_ref[q], preferred_element_type=F32)
             + dsk_ref[q] * u.astype(F32))
        y = jax.nn.gelu(y)
        z = jnp.dot(y.astype(BF16), glu_ref[q], preferred_element_type=F32) + glub_ref[q]
        outs.append(z[:, :pair_w] * jax.nn.sigmoid(z[:, pair_w:]))
    out_blk = jnp.concatenate(outs, axis=1).astype(BF16)
    for p0 in range(0, P, 2):
        o = lax.dot_general(out_blk, place(p0), _nt_dims(), preferred_element_type=F32)
        out_ref[p0] = o[:, :slab].astype(BF16)
        out_ref[p0 + 1] = o[:, slab:].astype(BF16)


def _s5(u, tables, batch, seq_len):
    nb = seq_len // S5_BLOCK
    n_steps = max(1, (nb - 1).bit_length())
    assert n_steps <= S5_SCAN_ROWS
    slab = pl.BlockSpec((S5_BLOCK, nb, LANES), lambda b, o: (0, b, o))
    per_step = lambda a: pl.BlockSpec((S5_STEP_PAIRS,) + a.shape[1:], lambda b, o: (o, 0, 0))
    return pl.pallas_call(
        functools.partial(_s5_kernel, n_steps),
        grid=(batch, S5_GROUPS // S5_STEP_GROUPS),
        in_specs=[slab] + [per_step(a) for a in tables],
        out_specs=slab,
        out_shape=jax.ShapeDtypeStruct(u.shape, BF16),
        compiler_params=pltpu.CompilerParams(dimension_semantics=("parallel", "parallel"),
                                             vmem_limit_bytes=VMEM_LIMIT_BYTES),
        name="s5",
    )(u, *tables)


def _outproj_kernel(a_ref, b_ref, x_ref, wa_ref, wb_ref, g_ref, rw2_ref, rw1_ref, rgb_ref, reb_ref,
                    x1_ref, h2_ref, meta_ref, metat_ref, cnt_ref, bs_ref):
    tm = x_ref.shape[0]

    @pl.when(pl.program_id(0) == 0)
    def _():
        cnt_ref[...] = jnp.zeros(cnt_ref.shape, F32)

    n_slabs = bs_ref.shape[0]
    for p in range(S5_BLOCK):
        bp = b_ref[p].astype(F32)
        for j in range(n_slabs):
            bs_ref[j, pl.ds(p, tm // S5_BLOCK, stride=S5_BLOCK), :] = bp[:, j * LANES:(j + 1) * LANES]
    out_b = jnp.concatenate([bs_ref[j] for j in range(n_slabs)], axis=1).astype(BF16)
    x1 = (x_ref[...]
          + jnp.dot(a_ref[...], wa_ref[...], preferred_element_type=F32)
          + jnp.dot(out_b, wb_ref[...], preferred_element_type=F32))
    x1_ref[...] = x1
    h2 = x1 * lax.rsqrt(jnp.mean(x1 * x1, axis=-1, keepdims=True) + EPS) * g_ref[...]
    _store_slabs(h2_ref, h2)

    logits = _dot_split(h2, h2.astype(BF16), rw2_ref, rw1_ref)
    el = logits[:, :N_EXPERTS] + reb_ref[...]
    gl = logits[:, N_EXPERTS:N_EXPERTS + N_GROUPS] + rgb_ref[...]
    gmax = jnp.max(gl, axis=1, keepdims=True)
    g_w = 1.0 / jnp.sum(jnp.exp(gl - gmax), axis=1, keepdims=True)
    lane_g = lax.broadcasted_iota(jnp.int32, (tm, N_GROUPS), 1)
    g_idx = jnp.min(jnp.where(gl == gmax, lane_g, N_GROUPS), axis=1, keepdims=True)
    lane_e = lax.broadcasted_iota(jnp.int32, (tm, N_EXPERTS), 1)
    elm = jnp.where(lane_e // PER_GROUP == g_idx, el, -jnp.inf)
    m1 = jnp.max(elm, axis=1, keepdims=True)
    i1 = jnp.min(jnp.where(elm == m1, lane_e, N_EXPERTS), axis=1, keepdims=True)
    elm2 = jnp.where(lane_e == i1, -jnp.inf, elm)
    m2 = jnp.max(elm2, axis=1, keepdims=True)
    i2 = jnp.min(jnp.where(elm2 == m2, lane_e, N_EXPERTS), axis=1, keepdims=True)
    e2 = jnp.exp(m2 - m1)
    w1 = g_w / (1.0 + e2)
    w2 = g_w * e2 / (1.0 + e2)

    sel1 = lane_e == i1
    sel2 = lane_e == i2
    picked = (sel1 | sel2).astype(BF16)
    rix = lax.broadcasted_iota(jnp.int32, (tm, tm), 0)
    cix = lax.broadcasted_iota(jnp.int32, (tm, tm), 1)
    before = jnp.dot((rix > cix).astype(BF16), picked, preferred_element_type=F32) + cnt_ref[...]
    r1 = jnp.sum(jnp.where(sel1, before, 0.0), axis=1, keepdims=True)
    r2 = jnp.sum(jnp.where(sel2, before, 0.0), axis=1, keepdims=True)
    cnt_ref[...] += jnp.sum(picked.astype(F32), axis=0, keepdims=True)
    col = lax.broadcasted_iota(jnp.int32, (tm, META_COLS), 1)
    meta = jnp.zeros((tm, META_COLS), F32)
    for c, val in enumerate((i1.astype(F32), i2.astype(F32), r1, r2, w1, w2)):
        meta = jnp.where(col == c, val, meta)
    meta_ref[...] = meta
    metat_ref[...] = _transpose_small(meta)


def _outproj(out_a, out_b, x2d, w_out, norm_g, rg_w, rg_b, re_w, re_b):
    T = x2d.shape[0]
    tm = OUTPROJ_TILE
    wa = w_out[:D_MLSTM].astype(BF16)
    wb = w_out[D_MLSTM:].astype(BF16)
    rw2, rw1 = _split_weight(jnp.concatenate([re_w, rg_w], axis=1))
    full = lambda shape: pl.BlockSpec(shape, lambda i: (0,) * len(shape))
    row = lambda n: pl.BlockSpec((tm, n), lambda i: (i, 0))
    return pl.pallas_call(
        _outproj_kernel,
        grid=(T // tm,),
        in_specs=[row(D_MLSTM), pl.BlockSpec((S5_BLOCK, tm // S5_BLOCK, D_S5), lambda i: (0, i, 0)), row(D_MODEL),
                  full((D_MLSTM, D_MODEL)), full((D_S5, D_MODEL)),
                  full((1, D_MODEL)), full((D_MODEL, 2 * LANES)), full((D_MODEL, LANES)), full((1, N_GROUPS)),
                  full((1, N_EXPERTS))],
        out_specs=[row(D_MODEL), pl.BlockSpec((tm * ROW_SLABS, LANES), lambda i: (i, 0)), row(META_COLS),
                   pl.BlockSpec((META_COLS, tm), lambda i: (0, i)),
                   full((1, N_EXPERTS))],
        out_shape=[jax.ShapeDtypeStruct((T, D_MODEL), F32), jax.ShapeDtypeStruct((T * ROW_SLABS, LANES), F32),
                   jax.ShapeDtypeStruct((T, META_COLS), F32), jax.ShapeDtypeStruct((META_COLS, T), F32),
                   jax.ShapeDtypeStruct((1, N_EXPERTS), F32)],
        scratch_shapes=[pltpu.VMEM((D_S5 // LANES, tm, LANES), F32)],
        compiler_params=pltpu.CompilerParams(dimension_semantics=("arbitrary",),
                                             vmem_limit_bytes=VMEM_LIMIT_BYTES),
        name="outproj",
    )(out_a, out_b, x2d, wa, wb, norm_g.reshape(1, D_MODEL), rw2, rw1, rg_b.reshape(1, -1), re_b.reshape(1, -1))


def _plan_kernel(metat_ref, base_ref, pos_ref):
    mt = metat_ref[...]
    tt = mt.shape[1]
    expert = lax.broadcasted_iota(jnp.int32, (N_EXPERTS, tt), 0)
    base = base_ref[...]
    row = lax.broadcasted_iota(jnp.int32, (META_COLS, tt), 0)
    pos = jnp.zeros((META_COLS, tt), F32)
    for k in range(2):
        first = jnp.sum(jnp.where(expert == mt[k:k + 1, :].astype(jnp.int32), base, 0.0), axis=0, keepdims=True)
        pos = jnp.where(row == k, first + mt[2 + k:3 + k, :], pos)
    pos_ref[...] = pos.astype(jnp.int32)


def _plan(meta_t, base):
    T = meta_t.shape[1]
    tt = PLAN_TILE
    return pl.pallas_call(
        _plan_kernel,
        grid=(T // tt,),
        in_specs=[pl.BlockSpec((META_COLS, tt), lambda i: (0, i)), pl.BlockSpec((N_EXPERTS, 1), lambda i: (0, 0))],
        out_specs=pl.BlockSpec((META_COLS, tt), lambda i: (0, i)),
        out_shape=jax.ShapeDtypeStruct((META_COLS, T), jnp.int32),
        compiler_params=pltpu.CompilerParams(dimension_semantics=("parallel",),
                                             vmem_limit_bytes=VMEM_LIMIT_BYTES),
        name="plan",
    )(meta_t, base.astype(F32).reshape(N_EXPERTS, 1))


def _slab_rows(ref, row, n_rows):
    return ref.at[pl.ds(pl.multiple_of(row * ROW_SLABS, ROW_SLABS), n_rows * ROW_SLABS), :]


def _row_copy(src_ref, src_row, dst_ref, dst_row, sem):
    return pltpu.make_async_copy(_slab_rows(src_ref, src_row, 1), _slab_rows(dst_ref, dst_row, 1), sem)


def _pad_copy(zero_ref, xs_ref, row, sem):
    return pltpu.make_async_copy(zero_ref, _slab_rows(xs_ref, row, FFN_TILE), sem)


def _dispatch_kernel(pad_ref, p1_ref, p2_ref, h_ref, xs_ref, zero_ref, sem, zsem):
    tm = h_ref.shape[0] // ROW_SLABS
    n_tiles = xs_ref.shape[0] // (FFN_TILE * ROW_SLABS)

    @pl.when(pl.program_id(0) == 0)
    def _():
        zero_ref[...] = jnp.zeros(zero_ref.shape, F32)
        for e in range(N_EXPERTS):
            _pad_copy(zero_ref, xs_ref, pad_ref[e], zsem).start()
        for e in range(N_EXPERTS):
            _pad_copy(zero_ref, xs_ref, 0, zsem).wait()
        first_free = pad_ref[N_EXPERTS]

        def start_tail(k, carry):
            _pad_copy(zero_ref, xs_ref, k * FFN_TILE, zsem).start()
            return carry

        def wait_tail(k, carry):
            _pad_copy(zero_ref, xs_ref, 0, zsem).wait()
            return carry

        lax.fori_loop(first_free, n_tiles, start_tail, 0)
        lax.fori_loop(first_free, n_tiles, wait_tail, 0)

    def issue(t, carry):
        _row_copy(h_ref, t, xs_ref, p1_ref[0, 0, t], sem).start(priority=0)
        _row_copy(h_ref, t, xs_ref, p2_ref[0, 0, t], sem).start(priority=1)
        return carry

    lax.fori_loop(0, tm, issue, 0, unroll=ROW_DMA_UNROLL)

    def drain(t, carry):
        _row_copy(h_ref, 0, xs_ref, 0, sem).wait()
        _row_copy(h_ref, 0, xs_ref, 0, sem).wait()
        return carry

    lax.fori_loop(0, tm, drain, 0, unroll=ROW_DMA_UNROLL)


def _dispatch(h2, pos1, pos2, pad_start, n_rows):
    T = h2.shape[0] // ROW_SLABS
    tm = DISPATCH_TILE
    smem_row = pl.BlockSpec((1, 1, tm), lambda i, pad: (i, 0, 0), memory_space=pltpu.SMEM)
    return pl.pallas_call(
        _dispatch_kernel,
        grid_spec=pltpu.PrefetchScalarGridSpec(
            num_scalar_prefetch=1,
            grid=(T // tm,),
            in_specs=[smem_row, smem_row, pl.BlockSpec((tm * ROW_SLABS, LANES), lambda i, pad: (i, 0))],
            out_specs=pl.BlockSpec(memory_space=pl.ANY),
            scratch_shapes=[pltpu.VMEM((FFN_TILE * ROW_SLABS, LANES), F32), pltpu.SemaphoreType.DMA,
                            pltpu.SemaphoreType.DMA]),
        out_shape=jax.ShapeDtypeStruct(((n_rows + FFN_TILE) * ROW_SLABS, LANES), F32),
        compiler_params=pltpu.CompilerParams(dimension_semantics=("arbitrary",),
                                             vmem_limit_bytes=VMEM_LIMIT_BYTES),
        name="dispatch",
    )(pad_start, pos1.reshape(T // tm, 1, tm), pos2.reshape(T // tm, 1, tm), h2)


def _tile_in(xs_ref, tile, xbuf_ref, slot, sem_ref):
    return pltpu.make_async_copy(_slab_rows(xs_ref, tile * FFN_TILE, FFN_TILE), xbuf_ref.at[slot], sem_ref.at[slot])


def _tile_out(ybuf_ref, slot, ys_ref, tile, sem_ref):
    return pltpu.make_async_copy(ybuf_ref.at[slot], _slab_rows(ys_ref, tile * FFN_TILE, FFN_TILE), sem_ref.at[slot])


def _ffn_kernel(first_ref, ntile_ref, nused_ref, xs_ref, wg_ref, wu_ref, wd_ref, ys_ref,
                xbuf_ref, ybuf_ref, wgb_ref, wub_ref, wdb_ref, isem, osem):
    e = pl.program_id(0)
    t0 = first_ref[e]
    n = ntile_ref[e]
    n_tiles = ys_ref.shape[0] // (FFN_TILE * ROW_SLABS)

    @pl.when(n > 0)
    def _():
        wgb_ref[...] = wg_ref[0].astype(BF16)
        wub_ref[...] = wu_ref[0].astype(BF16)
        wdb_ref[...] = wd_ref[0].astype(BF16)
        _tile_in(xs_ref, t0, xbuf_ref, 0, isem).start(priority=1)

        def two_tiles(k2, carry):
            for slot in (0, 1):
                k = 2 * k2 + slot

                @pl.when(k < n)
                def _():
                    _tile_in(xs_ref, t0 + k, xbuf_ref, slot, isem).wait()

                    @pl.when(k + 1 < n)
                    def _():
                        _tile_in(xs_ref, t0 + k + 1, xbuf_ref, 1 - slot, isem).start(priority=1)

                    @pl.when(k >= 2)
                    def _():
                        _tile_out(ybuf_ref, slot, ys_ref, t0 + k - 2, osem).wait()

                    x = _load_slabs(xbuf_ref, FFN_TILE, (slot,)).astype(BF16)
                    a = jnp.dot(x, wgb_ref[...], preferred_element_type=F32)
                    b = jnp.dot(x, wub_ref[...], preferred_element_type=F32)
                    act = (a * jax.nn.sigmoid(a) * b).astype(BF16)
                    _store_slabs(ybuf_ref, jnp.dot(act, wdb_ref[...], preferred_element_type=F32), (slot,))
                    _tile_out(ybuf_ref, slot, ys_ref, t0 + k, osem).start(priority=1)
            return carry

        lax.fori_loop(0, (n + 1) // 2, two_tiles, 0)
        for slot in (0, 1):
            @pl.when(n > slot)
            def _():
                _tile_out(ybuf_ref, slot, ys_ref, t0, osem).wait()

    @pl.when(e == pl.num_programs(0) - 1)
    def _():
        ybuf_ref[0] = jnp.zeros(ybuf_ref.shape[1:], F32)

        def start_tail(k, carry):
            _tile_out(ybuf_ref, 0, ys_ref, k, osem).start()
            return carry

        def wait_tail(k, carry):
            _tile_out(ybuf_ref, 0, ys_ref, k, osem).wait()
            return carry

        lax.fori_loop(nused_ref[0], n_tiles, start_tail, 0)
        lax.fori_loop(nused_ref[0], n_tiles, wait_tail, 0)


def _ffn(xs, first_tile, n_tile, n_used, n_rows, w_gate, w_up, w_down):
    wspec = lambda shape: pl.BlockSpec((1,) + shape, lambda e, *_: (e, 0, 0))
    tile_buf = pltpu.VMEM((2, FFN_TILE * ROW_SLABS, LANES), F32)
    return pl.pallas_call(
        _ffn_kernel,
        grid_spec=pltpu.PrefetchScalarGridSpec(
            num_scalar_prefetch=3,
            grid=(N_EXPERTS,),
            in_specs=[pl.BlockSpec(memory_space=pl.ANY),
                      wspec((D_MODEL, D_EXPERT)), wspec((D_MODEL, D_EXPERT)), wspec((D_EXPERT, D_MODEL))],
            out_specs=pl.BlockSpec(memory_space=pl.ANY),
            scratch_shapes=[tile_buf, tile_buf,
                            pltpu.VMEM((D_MODEL, D_EXPERT), BF16), pltpu.VMEM((D_MODEL, D_EXPERT), BF16),
                            pltpu.VMEM((D_EXPERT, D_MODEL), BF16),
                            pltpu.SemaphoreType.DMA((2,)), pltpu.SemaphoreType.DMA((2,))]),
        out_shape=jax.ShapeDtypeStruct((n_rows * ROW_SLABS, LANES), F32),
        compiler_params=pltpu.CompilerParams(dimension_semantics=("arbitrary",),
                                             vmem_limit_bytes=VMEM_LIMIT_BYTES),
        name="ffn",
    )(first_tile, n_tile, n_used, xs, w_gate, w_up, w_down)


def _combine_kernel(p1_ref, p2_ref, ys_ref, meta_ref, x1_ref, gf_ref, out_ref, buf_ref, sem):
    tm = x1_ref.shape[0]

    def issue(t, carry):
        _row_copy(ys_ref, p1_ref[0, 0, t], buf_ref.at[0], t, sem).start(priority=0)
        _row_copy(ys_ref, p2_ref[0, 0, t], buf_ref.at[1], t, sem).start(priority=1)
        return carry

    lax.fori_loop(0, tm, issue, 0, unroll=ROW_DMA_UNROLL)

    def drain(t, carry):
        _row_copy(ys_ref, 0, buf_ref.at[0], 0, sem).wait()
        _row_copy(ys_ref, 0, buf_ref.at[1], 0, sem).wait()
        return carry

    lax.fori_loop(0, tm, drain, 0, unroll=ROW_DMA_UNROLL)

    meta = meta_ref[...]
    y = x1_ref[...] + meta[:, 4:5] * _load_slabs(buf_ref, tm, (0,)) + meta[:, 5:6] * _load_slabs(buf_ref, tm, (1,))
    out_ref[...] = y * lax.rsqrt(jnp.mean(y * y, axis=-1, keepdims=True) + EPS) * gf_ref[...]


def _combine(ys, pos1, pos2, meta, x1, norm_final_g):
    T = x1.shape[0]
    tm = COMBINE_TILE
    smem_row = pl.BlockSpec((1, 1, tm), lambda i: (i, 0, 0), memory_space=pltpu.SMEM)
    row = lambda n: pl.BlockSpec((tm, n), lambda i: (i, 0))
    return pl.pallas_call(
        _combine_kernel,
        grid=(T // tm,),
        in_specs=[smem_row, smem_row, pl.BlockSpec(memory_space=pl.ANY), row(META_COLS), row(D_MODEL),
                  pl.BlockSpec((1, D_MODEL), lambda i: (0, 0))],
        out_specs=row(D_MODEL),
        out_shape=jax.ShapeDtypeStruct((T, D_MODEL), F32),
        scratch_shapes=[pltpu.VMEM((2, tm * ROW_SLABS, LANES), F32), pltpu.SemaphoreType.DMA],
        compiler_params=pltpu.CompilerParams(dimension_semantics=("arbitrary",),
                                             vmem_limit_bytes=VMEM_LIMIT_BYTES),
        name="combine",
    )(pos1.reshape(T // tm, 1, tm), pos2.reshape(T // tm, 1, tm), ys, meta, x1,
      norm_final_g.reshape(1, D_MODEL))


def _run_layout(counts):
    cnt = counts.reshape(-1).astype(jnp.int32)
    padded = (cnt + FFN_TILE - 1) // FFN_TILE * FFN_TILE
    ends = jnp.cumsum(padded)
    base = ends - padded
    n_used = (ends[-1:] // FFN_TILE).astype(jnp.int32)
    pad_start = jnp.concatenate([base + cnt, n_used])
    return base, pad_start, base // FFN_TILE, padded // FFN_TILE, n_used


def kernel(x, norm_mix_g, w_in, conv_w, conv_b, i_bias, f_bias, mlstm_norm_g, s5_lambda_re, s5_lambda_im,
           s5_log_dt, s5_b_re, s5_b_im, s5_c_re, s5_c_im, s5_d, s5_glu_w, s5_glu_b, w_out, norm_ffn_g,
           router_group_w, router_group_b, router_expert_w, router_expert_b, expert_w_gate, expert_w_up,
           expert_w_down, norm_final_g):
    batch, seq_len, d_model = x.shape
    assert d_model == D_MODEL and norm_mix_g.shape[0] == 1
    assert seq_len % MLSTM_CHUNK == 0 and seq_len % INPROJ_TILE == 0 and seq_len % S5_BLOCK == 0
    T = batch * seq_len
    x2d = x.reshape(T, D_MODEL)

    q, k, v, og, u, gcol, grow = _inproj(x2d, norm_mix_g[0], w_in[0], conv_w[0], conv_b[0], i_bias[0],
                                         f_bias[0], seq_len)
    out_a = _mlstm(q, k, v, og, gcol, grow, mlstm_norm_g[0], batch, seq_len)
    tables = _s5_tables(s5_lambda_re[0], s5_lambda_im[0], s5_log_dt[0], s5_b_re[0], s5_b_im[0], s5_c_re[0],
                        s5_c_im[0], s5_d[0], s5_glu_w[0], s5_glu_b[0])
    out_b = _s5(u, tables, batch, seq_len)
    x1, h2, meta, meta_t, counts = _outproj(out_a, out_b, x2d, w_out[0], norm_ffn_g[0], router_group_w[0],
                                            router_group_b[0], router_expert_w[0], router_expert_b[0])
    n_rows = 2 * T + N_EXPERTS * FFN_TILE
    base, pad_start, first_tile, n_tile, n_used = _run_layout(counts)
    pos = _plan(meta_t, base)
    xs = _dispatch(h2, pos[0], pos[1], pad_start, n_rows)
    ys = _ffn(xs, first_tile, n_tile, n_used, n_rows, expert_w_gate[0], expert_w_up[0], expert_w_down[0])
    y = _combine(ys, pos[0], pos[1], meta, x1, norm_final_g)
    return y.reshape(batch, seq_len, D_MODEL)
```

```python
import functools

import jax
import jax.numpy as jnp
from jax import lax
from jax.experimental import pallas as pl
from jax.experimental.pallas import tpu as pltpu

F32 = jnp.float32
BF16 = jnp.bfloat16
HIGHEST = lax.Precision.HIGHEST

D_MODEL = 1024
D_MLSTM = 512
D_S5 = 512
HEADS = 4
HEAD_DIM = 128
CONV_WIDTH = 4
S5_GROUP_CH = 16
S5_GROUPS = 32
S5_STATE = 64
N_GROUPS = 4
PER_GROUP = 8
N_EXPERTS = 32
D_EXPERT = 512
EPS = 1e-6
LAMBDA_RE_MAX = -1e-4

VMEM_LIMIT_BYTES = 56 * 1024 * 1024
LANES = 128
SUBLANES = 8
ROW_SLABS = D_MODEL // LANES
assert ROW_SLABS == SUBLANES

INPROJ_TILE = 512
MLSTM_CHUNK = 256
S5_BLOCK = 16
S5_PAIR = 2
S5_STEP_GROUPS = LANES // S5_GROUP_CH
S5_STEP_PAIRS = S5_STEP_GROUPS // S5_PAIR
S5_SCAN_ROWS = 16
OUTPROJ_TILE = 512
META_COLS = 8
PLAN_TILE = 2048
DISPATCH_TILE = 512
FFN_TILE = 256
COMBINE_TILE = 512
ROW_DMA_UNROLL = 8


def _nt_dims():
    return (((1,), (1,)), ((), ()))


def _tn_dims():
    return (((0,), (0,)), ((), ()))


def _log_sigmoid(x):
    return jnp.minimum(x, 0.0) - jnp.log1p(jnp.exp(-jnp.abs(x)))


def _split_weight(w):
    w = jnp.pad(w.astype(F32), ((0, 0), (0, LANES - w.shape[1])))
    hi = w.astype(BF16)
    lo = (w - hi.astype(F32)).astype(BF16)
    return jnp.concatenate([hi, lo], axis=1), hi


def _dot_split(x, x_hi, w_hilo_ref, w_hi_ref):
    x_lo = (x - x_hi.astype(F32)).astype(BF16)
    two = jnp.dot(x_hi, w_hilo_ref[...], preferred_element_type=F32)
    return two[:, :LANES] + two[:, LANES:] + jnp.dot(x_lo, w_hi_ref[...], preferred_element_type=F32)


def _store_slabs(ref, value, lead=()):
    m = value.shape[0]
    for s in range(ROW_SLABS):
        ref[lead + (pl.ds(s, m, stride=ROW_SLABS), slice(None))] = value[:, s * LANES:(s + 1) * LANES]


def _load_slabs(ref, m, lead=()):
    return jnp.concatenate([ref[lead + (pl.ds(s, m, stride=ROW_SLABS), slice(None))] for s in range(ROW_SLABS)],
                           axis=1)


def _transpose_small(a):
    n = a.shape[1]
    eye = (lax.broadcasted_iota(jnp.int32, (n, n), 0) == lax.broadcasted_iota(jnp.int32, (n, n), 1)).astype(F32)
    return lax.dot_general(eye, a, _nt_dims(), precision=HIGHEST, preferred_element_type=F32)


def _inproj_kernel(tiles_per_seq, x_ref, g_ref, wqk_ref, wv_ref, wo_ref, wu_ref, wg2_ref, wg1_ref,
                   gb_ref, cw_ref, cb_ref,
                   q_ref, k_ref, v_ref, og_ref, u_ref, gcol_ref, grow_ref, ext_ref, us_ref):
    i = pl.program_id(0)
    tm = x_ref.shape[0]
    x = x_ref[...]
    h = x * lax.rsqrt(jnp.mean(x * x, axis=-1, keepdims=True) + EPS) * g_ref[...]
    hb = h.astype(BF16)

    qk_raw = jnp.dot(hb, wqk_ref[...], preferred_element_type=F32)

    @pl.when(i % tiles_per_seq == 0)
    def _():
        ext_ref[0:8, :] = jnp.zeros((8, ext_ref.shape[1]), F32)

    ext_ref[8:, :] = qk_raw
    cw = cw_ref[...]
    acc = cb_ref[...] + cw[CONV_WIDTH - 1:CONV_WIDTH, :] * qk_raw
    for s in range(1, CONV_WIDTH):
        acc = acc + cw[CONV_WIDTH - 1 - s:CONV_WIDTH - s, :] * ext_ref[pl.ds(8 - s, tm), :]
    ext_ref[0:8, :] = qk_raw[tm - 8:, :]
    qk = acc * jax.nn.sigmoid(acc)
    q_ref[...] = qk[:, :D_MLSTM].astype(BF16)
    k_ref[...] = (qk[:, D_MLSTM:] * (HEAD_DIM ** -0.5)).astype(BF16)

    v_ref[...] = jnp.dot(hb, wv_ref[...], preferred_element_type=F32).astype(BF16)
    o_pre = jnp.dot(hb, wo_ref[...], preferred_element_type=F32)
    og_ref[...] = jax.nn.sigmoid(o_pre).astype(BF16)

    u = jnp.dot(hb, wu_ref[...], preferred_element_type=F32)
    n_slabs = us_ref.shape[0]
    for j in range(n_slabs):
        us_ref[j] = u[:, j * LANES:(j + 1) * LANES]
    for p in range(S5_BLOCK):
        u_ref[p] = jnp.concatenate([us_ref[j, pl.ds(p, tm // S5_BLOCK, stride=S5_BLOCK), :]
                                    for j in range(n_slabs)], axis=1).astype(BF16)

    gcol = _dot_split(h, hb, wg2_ref, wg1_ref)[:, :2 * HEADS] + gb_ref[...]
    gcol_ref[...] = gcol
    grow_ref[...] = _transpose_small(gcol)


def _inproj(x2d, norm_g, w_in, conv_w, conv_b, i_bias, f_bias, seq_len):
    T = x2d.shape[0]
    tm = INPROJ_TILE
    dm = D_MLSTM
    wqk = w_in[:, :2 * dm].astype(BF16)
    wv = w_in[:, 2 * dm:3 * dm].astype(BF16)
    wo = w_in[:, 3 * dm:4 * dm].astype(BF16)
    wg2, wg1 = _split_weight(w_in[:, 4 * dm:4 * dm + 2 * HEADS])
    wu = w_in[:, 4 * dm + 2 * HEADS:].astype(BF16)
    gb = jnp.concatenate([i_bias, f_bias]).astype(F32)
    full = lambda shape: pl.BlockSpec(shape, lambda i: (0,) * len(shape))
    row = lambda n: pl.BlockSpec((tm, n), lambda i: (i, 0))
    return pl.pallas_call(
        functools.partial(_inproj_kernel, seq_len // tm),
        grid=(T // tm,),
        in_specs=[row(D_MODEL), full((1, D_MODEL)), full((D_MODEL, 2 * dm)), full((D_MODEL, dm)),
                  full((D_MODEL, dm)), full((D_MODEL, D_S5)), full((D_MODEL, 2 * LANES)),
                  full((D_MODEL, LANES)), full((1, 2 * HEADS)),
                  full((CONV_WIDTH, 2 * dm)), full((1, 2 * dm))],
        out_specs=[row(dm), row(dm), row(dm), row(dm),
                   pl.BlockSpec((S5_BLOCK, tm // S5_BLOCK, D_S5), lambda i: (0, i, 0)), row(2 * HEADS),
                   pl.BlockSpec((2 * HEADS, tm), lambda i: (0, i))],
        out_shape=[jax.ShapeDtypeStruct((T, dm), BF16)] * 4
        + [jax.ShapeDtypeStruct((S5_BLOCK, T // S5_BLOCK, D_S5), BF16),
           jax.ShapeDtypeStruct((T, 2 * HEADS), F32),
           jax.ShapeDtypeStruct((2 * HEADS, T), F32)],
        scratch_shapes=[pltpu.VMEM((tm + 8, 2 * dm), F32), pltpu.VMEM((D_S5 // LANES, tm, LANES), F32)],
        compiler_params=pltpu.CompilerParams(dimension_semantics=("arbitrary",),
                                             vmem_limit_bytes=VMEM_LIMIT_BYTES),
        name="inproj",
    )(x2d, norm_g.reshape(1, D_MODEL), wqk, wv, wo, wu, wg2, wg1, gb.reshape(1, -1),
      conv_w, conv_b.reshape(1, -1))


def _mlstm_kernel(batch, q_ref, k_ref, v_ref, og_ref, gcol_ref, ng_ref, *refs):
    grow_refs = refs[:batch]
    out_ref, c_ref, m_ref = refs[batch:]
    tc = q_ref.shape[1]
    d = HEAD_DIM

    @pl.when(pl.program_id(0) == 0)
    def _():
        c_ref[...] = jnp.zeros(c_ref.shape, F32)
        m_ref[...] = jnp.zeros(m_ref.shape, F32)

    causal = lax.broadcasted_iota(jnp.int32, (tc, tc), 0) >= lax.broadcasted_iota(jnp.int32, (tc, tc), 1)
    ones_col = (lax.broadcasted_iota(jnp.int32, (tc, d), 1) == 0).astype(BF16)
    for b in range(batch):
        gcol = gcol_ref[b]
        grow = grow_refs[b][...]
        b_col_all = jnp.dot(causal.astype(F32), _log_sigmoid(gcol), precision=HIGHEST,
                            preferred_element_type=F32)
        b_row_all = _transpose_small(b_col_all)
        for h in range(HEADS):
            _mlstm_head(b, h, causal, ones_col, gcol, grow, b_col_all, b_row_all,
                        q_ref, k_ref, v_ref, og_ref, ng_ref, out_ref, c_ref, m_ref)


def _mlstm_head(b, h, causal, ones_col, gcol, grow, b_col_all, b_row_all,
                q_ref, k_ref, v_ref, og_ref, ng_ref, out_ref, c_ref, m_ref):
    tc, d = q_ref.shape[1], HEAD_DIM
    hs = slice(h * d, (h + 1) * d)
    st = b * HEADS + h
    li_col = gcol[:, h:h + 1]
    li_row = grow[h:h + 1, :]
    b_col = b_col_all[:, HEADS + h:HEADS + h + 1]
    b_row = b_row_all[HEADS + h:HEADS + h + 1, :]
    m_prev = m_ref[st:st + 1, 0:1]

    dmat = jnp.where(causal, b_col - b_row + li_row, -jnp.inf)
    inter = b_col + m_prev
    m_row = jnp.maximum(inter, jnp.max(dmat, axis=1, keepdims=True))
    w_intra = jnp.exp(dmat - m_row)
    w_inter = jnp.exp(inter - m_row)

    qh = q_ref[b, :, hs]
    kh = k_ref[b, :, hs]
    v_aug = jnp.concatenate([v_ref[b, :, hs], ones_col], axis=1)
    s = lax.dot_general(qh, kh, _nt_dims(), preferred_element_type=F32) * w_intra
    c_aug = c_ref[st]
    tot = (jnp.dot(s.astype(BF16), v_aug, preferred_element_type=F32)
           + w_inter * lax.dot_general(qh, c_aug.astype(BF16), _nt_dims(), preferred_element_type=F32))
    num = tot[:, :d]
    den = tot[:, d:d + 1]
    hh = num / jnp.maximum(jnp.abs(den), jnp.exp(-m_row))
    hh = hh * lax.rsqrt(jnp.mean(hh * hh, axis=-1, keepdims=True) + EPS)
    out_ref[b, :, hs] = (og_ref[b, :, hs].astype(F32) * (hh * ng_ref[:, hs])).astype(BF16)

    b_last = b_col[tc - 1:tc, :]
    d_state = b_last - b_col + li_col
    m_new = jnp.maximum(b_last + m_prev, jnp.max(d_state, axis=0, keepdims=True))
    w_s = jnp.exp(d_state - m_new)
    w_c = jnp.exp(b_last + m_prev - m_new)
    upd = lax.dot_general((v_aug.astype(F32) * w_s).astype(BF16), kh, _tn_dims(),
                          preferred_element_type=F32)
    c_ref[st] = w_c * c_aug + upd
    m_ref[st:st + 1, :] = jnp.broadcast_to(m_new, (1, m_ref.shape[1]))


def _mlstm(q, k, v, og, gcol, grow, norm_g, batch, seq_len):
    tc = MLSTM_CHUNK
    nc = seq_len // tc
    T = q.shape[0]
    n_state = batch * HEADS
    seq = lambda a: a.reshape(batch, seq_len, a.shape[1])
    rows = lambda n: pl.BlockSpec((batch, tc, n), lambda c: (0, c, 0))
    grow_specs = [pl.BlockSpec((2 * HEADS, tc), lambda c, b=b: (0, b * nc + c)) for b in range(batch)]
    out = pl.pallas_call(
        functools.partial(_mlstm_kernel, batch),
        grid=(nc,),
        in_specs=[rows(D_MLSTM), rows(D_MLSTM), rows(D_MLSTM), rows(D_MLSTM), rows(2 * HEADS),
                  pl.BlockSpec((1, D_MLSTM), lambda c: (0, 0))] + grow_specs,
        out_specs=rows(D_MLSTM),
        out_shape=jax.ShapeDtypeStruct((batch, seq_len, D_MLSTM), BF16),
        scratch_shapes=[pltpu.VMEM((n_state, 2 * HEAD_DIM, HEAD_DIM), F32),
                        pltpu.VMEM((-(-n_state // SUBLANES) * SUBLANES, LANES), F32)],
        compiler_params=pltpu.CompilerParams(dimension_semantics=("arbitrary",),
                                             vmem_limit_bytes=VMEM_LIMIT_BYTES),
        name="mlstm",
    )(seq(q), seq(k), seq(v), seq(og), seq(gcol), norm_g.reshape(1, D_MLSTM), *([grow] * batch))
    return out.reshape(T, D_MLSTM)


def _cplx_pow(exponent, theta_re, theta_im):
    mag = jnp.exp(exponent * theta_re)
    return mag * jnp.cos(exponent * theta_im), mag * jnp.sin(exponent * theta_im)


def _s5_tables_kernel(lrr_ref, lir_ref, dtr_ref, lrc_ref, lic_ref, dtc_ref, btr_ref, bti_ref, ctr_ref, cti_ref,
                      dcol_ref, gw1_ref, gw2_ref, gbcol_ref,
                      we_ref, kt_ref, wc_ref, glu_ref, glub_ref, dsk_ref, sr_ref, si_ref):
    P, C, N = S5_BLOCK, S5_GROUP_CH, S5_STATE
    PC, NN = P * C, S5_PAIR * S5_STATE
    shift_c, shift_n = C.bit_length() - 1, N.bit_length() - 1

    lr = jnp.minimum(lrr_ref[0], LAMBDA_RE_MAX)
    li = lir_ref[0]
    dt = jnp.exp(dtr_ref[0])
    th_r, th_i = lr * dt, li * dt
    ar, ai = _cplx_pow(1.0, th_r, th_i)
    den = lr * lr + li * li
    cr = ((ar - 1.0) * lr + ai * li) / den
    ci = (ai * lr - (ar - 1.0) * li) / den
    btr, bti = btr_ref[0], bti_ref[0]
    bbr = cr * btr - ci * bti
    bbi = cr * bti + ci * btr

    back = (P - 1 - lax.broadcasted_iota(jnp.int32, (P, 1), 0)).astype(F32)
    pwr, pwi = _cplx_pow(back, th_r, th_i)
    val_r = jnp.concatenate([pwr[p:p + 1] * bbr - pwi[p:p + 1] * bbi for p in range(P)], axis=0)
    val_i = jnp.concatenate([pwr[p:p + 1] * bbi + pwi[p:p + 1] * bbr for p in range(P)], axis=0)
    lane_grp = lax.broadcasted_iota(jnp.int32, (PC, NN), 1) >> shift_n
    we_r = jnp.concatenate([jnp.where(lane_grp == g, val_r, 0.0) for g in range(S5_PAIR)], axis=0)
    we_i = jnp.concatenate([jnp.where(lane_grp == g, val_i, 0.0) for g in range(S5_PAIR)], axis=0)
    we_ref[0] = jnp.concatenate([we_r, we_i], axis=1).astype(BF16)

    steps = (P << lax.broadcasted_iota(jnp.int32, (S5_SCAN_ROWS, 1), 0)).astype(F32)
    sr_ref[0], si_ref[0] = _cplx_pow(steps, th_r, th_i)

    lrc = jnp.minimum(lrc_ref[0], LAMBDA_RE_MAX)
    dtc = jnp.exp(dtc_ref[0])
    thc_r, thc_i = lrc * dtc, lic_ref[0] * dtc
    lane = lax.broadcasted_iota(jnp.int32, (NN, PC), 1)
    p0r, p0i = _cplx_pow((lane >> shift_c).astype(F32), thc_r, thc_i)
    acr, aci = _cplx_pow(1.0, thc_r, thc_i)
    p1r, p1i = p0r * acr - p0i * aci, p0r * aci + p0i * acr
    tile_c = ((lax.broadcasted_iota(jnp.int32, (C, PC), 1) & (C - 1))
              == lax.broadcasted_iota(jnp.int32, (C, PC), 0)).astype(F32)
    ctr = jnp.dot(ctr_ref[0], tile_c, precision=HIGHEST, preferred_element_type=F32)
    cti = jnp.dot(cti_ref[0], tile_c, precision=HIGHEST, preferred_element_type=F32)
    w0r, w0i = ctr * p0r - cti * p0i, ctr * p0i + cti * p0r
    vr, vi = ctr * p1r - cti * p1i, -(ctr * p1i + cti * p1r)
    row_grp = lax.broadcasted_iota(jnp.int32, (NN, PC), 0) >> shift_n
    wc_r = jnp.concatenate([jnp.where(row_grp == g, vr, 0.0) for g in range(S5_PAIR)], axis=1)
    wc_i = jnp.concatenate([jnp.where(row_grp == g, vi, 0.0) for g in range(S5_PAIR)], axis=1)
    wc_ref[0] = jnp.concatenate([wc_r, wc_i], axis=0).astype(BF16)

    b_grp = lax.broadcasted_iota(jnp.int32, (C, NN), 1) >> shift_n
    lane_k = lax.broadcasted_iota(jnp.int32, (C, PC), 1)
    same_pos = ((lax.broadcasted_iota(jnp.int32, (PC, PC), 0) >> shift_c)
                == (lax.broadcasted_iota(jnp.int32, (PC, PC), 1) >> shift_c))
    zero_blk = jnp.zeros((PC, PC), F32)

    def on_diagonal(blk, g):
        return jnp.concatenate([blk if j == g else zero_blk for j in range(S5_PAIR)], axis=1)

    kt_rows, glu_v, glu_g, bias_v, bias_g, skip = [], [], [], [], [], []
    for g in range(S5_PAIR):
        k0 = (jnp.dot(jnp.where(b_grp == g, bbr, 0.0), w0r, precision=HIGHEST, preferred_element_type=F32)
              - jnp.dot(jnp.where(b_grp == g, bbi, 0.0), w0i, precision=HIGHEST, preferred_element_type=F32))
        toep = jnp.concatenate(
            [k0 if p == 0 else jnp.where(lane_k >= C * p, pltpu.roll(k0, C * p, axis=1), 0.0) for p in range(P)],
            axis=0)
        kt_rows.append(on_diagonal(toep, g))
        for w_ref, dst in ((gw1_ref, glu_v), (gw2_ref, glu_g)):
            t = jnp.dot(w_ref[0, g], tile_c, precision=HIGHEST, preferred_element_type=F32)
            dst.append(on_diagonal(jnp.where(same_pos, jnp.concatenate([t] * P, axis=0), 0.0), g))
        gb = gbcol_ref[0, g]
        bias_v.append(jnp.sum(gb[:C] * tile_c, axis=0, keepdims=True))
        bias_g.append(jnp.sum(gb[C:] * tile_c, axis=0, keepdims=True))
        skip.append(jnp.sum(dcol_ref[0, g] * tile_c, axis=0, keepdims=True))
    kt_ref[0] = jnp.concatenate(kt_rows, axis=0).astype(BF16)
    glu_ref[0] = jnp.concatenate([jnp.concatenate(glu_v, axis=0), jnp.concatenate(glu_g, axis=0)],
                                 axis=1).astype(BF16)
    glub_ref[0] = jnp.concatenate(bias_v + bias_g, axis=1)
    dsk_ref[0] = jnp.concatenate(skip, axis=1)


def _s5_tables(lam_re, lam_im, log_dt, b_re, b_im, c_re, c_im, d_skip, glu_w, glu_b):
    G, N, C, P = S5_GROUPS, S5_STATE, S5_GROUP_CH, S5_BLOCK
    J, NN, PC = G // S5_PAIR, S5_PAIR * S5_STATE, S5_BLOCK * S5_GROUP_CH
    f = lambda a: a.astype(F32)
    dt_full = jnp.broadcast_to(f(log_dt)[:, None], (G, N))
    rows = [a.reshape(J, 1, NN) for a in (f(lam_re), f(lam_im), dt_full)]
    cols = [a.reshape(J, NN, 1) for a in (f(lam_re), f(lam_im), dt_full)]
    bt = [f(a).reshape(J, S5_PAIR, N, C).transpose(0, 3, 1, 2).reshape(J, C, NN) for a in (b_re, b_im)]
    ct = [f(a).transpose(0, 2, 1).reshape(J, NN, C) for a in (c_re, c_im)]
    dcol = f(d_skip).reshape(J, S5_PAIR, C, 1)
    gw1 = f(glu_w)[..., :C].reshape(J, S5_PAIR, C, C)
    gw2 = f(glu_w)[..., C:].reshape(J, S5_PAIR, C, C)
    gbcol = f(glu_b).reshape(J, S5_PAIR, 2 * C, 1)
    ins = rows + cols + bt + ct + [dcol, gw1, gw2, gbcol]
    per_pair = lambda shape: pl.BlockSpec((1,) + shape, lambda j: (j,) + (0,) * len(shape))
    out_shapes = [((S5_PAIR * PC, 2 * NN), BF16), ((S5_PAIR * PC, S5_PAIR * PC), BF16),
                  ((2 * NN, S5_PAIR * PC), BF16), ((S5_PAIR * PC, 2 * S5_PAIR * PC), BF16),
                  ((1, 2 * S5_PAIR * PC), F32), ((1, S5_PAIR * PC), F32),
                  ((S5_SCAN_ROWS, NN), F32), ((S5_SCAN_ROWS, NN), F32)]
    return pl.pallas_call(
        _s5_tables_kernel,
        grid=(J,),
        in_specs=[per_pair(a.shape[1:]) for a in ins],
        out_specs=[per_pair(s) for s, _ in out_shapes],
        out_shape=[jax.ShapeDtypeStruct((J,) + s, d) for s, d in out_shapes],
        compiler_params=pltpu.CompilerParams(dimension_semantics=("parallel",),
                                             vmem_limit_bytes=VMEM_LIMIT_BYTES),
        name="s5_tables",
    )(*ins)


def _s5_kernel(n_steps, u_ref, we_ref, kt_ref, wc_ref, glu_ref, glub_ref, dsk_ref, sr_ref, si_ref, out_ref):
    P, C = S5_BLOCK, S5_GROUP_CH
    nb, slab = u_ref.shape[1], u_ref.shape[2]
    wide = slab * P
    pair_w = S5_PAIR * P * C
    half = S5_PAIR * S5_STATE
    shift_c = C.bit_length() - 1

    lane = lax.broadcasted_iota(jnp.int32, (slab, wide), 1)
    src = lax.broadcasted_iota(jnp.int32, (slab, wide), 0)
    target0 = ((src >> shift_c) << (shift_c + P.bit_length() - 1)) + (src & (C - 1))

    def place(p0):
        return jnp.concatenate([jnp.where(lane == target0 + p * C, 1.0, 0.0).astype(BF16)
                                for p in (p0, p0 + 1)], axis=0)

    u_blk = jnp.zeros((nb, wide), F32)
    for p0 in range(0, P, 2):
        lhs = jnp.concatenate([u_ref[p0], u_ref[p0 + 1]], axis=1)
        u_blk = u_blk + jnp.dot(lhs, place(p0), preferred_element_type=F32)
    u_blk = u_blk.astype(BF16)

    row = lax.broadcasted_iota(jnp.int32, (nb, half), 0)
    outs = []
    for q in range(S5_STEP_PAIRS):
        u = u_blk[:, q * pair_w:(q + 1) * pair_w]
        e = jnp.dot(u, we_ref[q], preferred_element_type=F32)
        xr, xi = e[:, :half], e[:, half:]
        for k in range(n_steps):
            sh = 1 << k
            ar = sr_ref[q, k:k + 1, :]
            ai = si_ref[q, k:k + 1, :]
            valid = row >= sh
            yr = jnp.where(valid, pltpu.roll(xr, sh, axis=0), 0.0)
            yi = jnp.where(valid, pltpu.roll(xi, sh, axis=0), 0.0)
            xr, xi = xr + ar * yr - ai * yi, xi + ar * yi + ai * yr
        first = row >= 1
        pr = jnp.where(first, pltpu.roll(xr, 1, axis=0), 0.0)
        pi = jnp.where(first, pltpu.roll(xi, 1, axis=0), 0.0)
        sprev = jnp.concatenate([pr, pi], axis=1).astype(BF16)
        y = (jnp.dot(u, kt_ref[q], preferred_element_type=F32)
             + jnp.dot(sprev, wc_ref[q], preferred_element_type=F32)
             + dsk_ref[q] * u.astype(F32))
        y = jax.nn.gelu(y)
        z = jnp.dot(y.astype(BF16), glu_ref[q], preferred_element_type=F32) + glub_ref[q]
        outs.append(z[:, :pair_w] * jax.nn.sigmoid(z[:, pair_w:]))
    out_blk = jnp.concatenate(outs, axis=1).astype(BF16)
    for p0 in range(0, P, 2):
        o = lax.dot_general(out_blk, place(p0), _nt_dims(), preferred_element_type=F32)
        out_ref[p0] = o[:, :slab].astype(BF16)
        out_ref[p0 + 1] = o[:, slab:].astype(BF16)


def _s5(u, tables, batch, seq_len):
    nb = seq_len // S5_BLOCK
    n_steps = max(1, (nb - 1).bit_length())
    assert n_steps <= S5_SCAN_ROWS
    slab = pl.BlockSpec((S5_BLOCK, nb, LANES), lambda b, o: (0, b, o))
    per_step = lambda a: pl.BlockSpec((S5_STEP_PAIRS,) + a.shape[1:], lambda b, o: (o, 0, 0))
    return pl.pallas_call(
        functools.partial(_s5_kernel, n_steps),
        grid=(batch, S5_GROUPS // S5_STEP_GROUPS),
        in_specs=[slab] + [per_step(a) for a in tables],
        out_specs=slab,
        out_shape=jax.ShapeDtypeStruct(u.shape, BF16),
        compiler_params=pltpu.CompilerParams(dimension_semantics=("parallel", "parallel"),
                                             vmem_limit_bytes=VMEM_LIMIT_BYTES),
        name="s5",
    )(u, *tables)


def _outproj_kernel(a_ref, b_ref, x_ref, wa_ref, wb_ref, g_ref, rw2_ref, rw1_ref, rgb_ref, reb_ref,
                    x1_ref, h2_ref, meta_ref, metat_ref, cnt_ref, bs_ref):
    tm = x_ref.shape[0]

    @pl.when(pl.program_id(0) == 0)
    def _():
        cnt_ref[...] = jnp.zeros(cnt_ref.shape, F32)

    n_slabs = bs_ref.shape[0]
    for p in range(S5_BLOCK):
        bp = b_ref[p].astype(F32)
        for j in range(n_slabs):
            bs_ref[j, pl.ds(p, tm // S5_BLOCK, stride=S5_BLOCK), :] = bp[:, j * LANES:(j + 1) * LANES]
    out_b = jnp.concatenate([bs_ref[j] for j in range(n_slabs)], axis=1).astype(BF16)
    x1 = (x_ref[...]
          + jnp.dot(a_ref[...], wa_ref[...], preferred_element_type=F32)
          + jnp.dot(out_b, wb_ref[...], preferred_element_type=F32))
    x1_ref[...] = x1
    h2 = x1 * lax.rsqrt(jnp.mean(x1 * x1, axis=-1, keepdims=True) + EPS) * g_ref[...]
    _store_slabs(h2_ref, h2)

    logits = _dot_split(h2, h2.astype(BF16), rw2_ref, rw1_ref)
    el = logits[:, :N_EXPERTS] + reb_ref[...]
    gl = logits[:, N_EXPERTS:N_EXPERTS + N_GROUPS] + rgb_ref[...]
    gmax = jnp.max(gl, axis=1, keepdims=True)
    g_w = 1.0 / jnp.sum(jnp.exp(gl - gmax), axis=1, keepdims=True)
    lane_g = lax.broadcasted_iota(jnp.int32, (tm, N_GROUPS), 1)
    g_idx = jnp.min(jnp.where(gl == gmax, lane_g, N_GROUPS), axis=1, keepdims=True)
    lane_e = lax.broadcasted_iota(jnp.int32, (tm, N_EXPERTS), 1)
    elm = jnp.where(lane_e // PER_GROUP == g_idx, el, -jnp.inf)
    m1 = jnp.max(elm, axis=1, keepdims=True)
    i1 = jnp.min(jnp.where(elm == m1, lane_e, N_EXPERTS), axis=1, keepdims=True)
    elm2 = jnp.where(lane_e == i1, -jnp.inf, elm)
    m2 = jnp.max(elm2, axis=1, keepdims=True)
    i2 = jnp.min(jnp.where(elm2 == m2, lane_e, N_EXPERTS), axis=1, keepdims=True)
    e2 = jnp.exp(m2 - m1)
    w1 = g_w / (1.0 + e2)
    w2 = g_w * e2 / (1.0 + e2)

    sel1 = lane_e == i1
    sel2 = lane_e == i2
    picked = (sel1 | sel2).astype(BF16)
    rix = lax.broadcasted_iota(jnp.int32, (tm, tm), 0)
    cix = lax.broadcasted_iota(jnp.int32, (tm, tm), 1)
    before = jnp.dot((rix > cix).astype(BF16), picked, preferred_element_type=F32) + cnt_ref[...]
    r1 = jnp.sum(jnp.where(sel1, before, 0.0), axis=1, keepdims=True)
    r2 = jnp.sum(jnp.where(sel2, before, 0.0), axis=1, keepdims=True)
    cnt_ref[...] += jnp.sum(picked.astype(F32), axis=0, keepdims=True)
    col = lax.broadcasted_iota(jnp.int32, (tm, META_COLS), 1)
    meta = jnp.zeros((tm, META_COLS), F32)
    for c, val in enumerate((i1.astype(F32), i2.astype(F32), r1, r2, w1, w2)):
        meta = jnp.where(col == c, val, meta)
    meta_ref[...] = meta
    metat_ref[...] = _transpose_small(meta)


def _outproj(out_a, out_b, x2d, w_out, norm_g, rg_w, rg_b, re_w, re_b):
    T = x2d.shape[0]
    tm = OUTPROJ_TILE
    wa = w_out[:D_MLSTM].astype(BF16)
    wb = w_out[D_MLSTM:].astype(BF16)
    rw2, rw1 = _split_weight(jnp.concatenate([re_w, rg_w], axis=1))
    full = lambda shape: pl.BlockSpec(shape, lambda i: (0,) * len(shape))
    row = lambda n: pl.BlockSpec((tm, n), lambda i: (i, 0))
    return pl.pallas_call(
        _outproj_kernel,
        grid=(T // tm,),
        in_specs=[row(D_MLSTM), pl.BlockSpec((S5_BLOCK, tm // S5_BLOCK, D_S5), lambda i: (0, i, 0)), row(D_MODEL),
                  full((D_MLSTM, D_MODEL)), full((D_S5, D_MODEL)),
                  full((1, D_MODEL)), full((D_MODEL, 2 * LANES)), full((D_MODEL, LANES)), full((1, N_GROUPS)),
                  full((1, N_EXPERTS))],
        out_specs=[row(D_MODEL), pl.BlockSpec((tm * ROW_SLABS, LANES), lambda i: (i, 0)), row(META_COLS),
                   pl.BlockSpec((META_COLS, tm), lambda i: (0, i)),
                   full((1, N_EXPERTS))],
        out_shape=[jax.ShapeDtypeStruct((T, D_MODEL), F32), jax.ShapeDtypeStruct((T * ROW_SLABS, LANES), F32),
                   jax.ShapeDtypeStruct((T, META_COLS), F32), jax.ShapeDtypeStruct((META_COLS, T), F32),
                   jax.ShapeDtypeStruct((1, N_EXPERTS), F32)],
        scratch_shapes=[pltpu.VMEM((D_S5 // LANES, tm, LANES), F32)],
        compiler_params=pltpu.CompilerParams(dimension_semantics=("arbitrary",),
                                             vmem_limit_bytes=VMEM_LIMIT_BYTES),
        name="outproj",
    )(out_a, out_b, x2d, wa, wb, norm_g.reshape(1, D_MODEL), rw2, rw1, rg_b.reshape(1, -1), re_b.reshape(1, -1))


def _plan_kernel(metat_ref, base_ref, pos_ref):
    mt = metat_ref[...]
    tt = mt.shape[1]
    expert = lax.broadcasted_iota(jnp.int32, (N_EXPERTS, tt), 0)
    base = base_ref[...]
    row = lax.broadcasted_iota(jnp.int32, (META_COLS, tt), 0)
    pos = jnp.zeros((META_COLS, tt), F32)
    for k in range(2):
        first = jnp.sum(jnp.where(expert == mt[k:k + 1, :].astype(jnp.int32), base, 0.0), axis=0, keepdims=True)
        pos = jnp.where(row == k, first + mt[2 + k:3 + k, :], pos)
    pos_ref[...] = pos.astype(jnp.int32)


def _plan(meta_t, base):
    T = meta_t.shape[1]
    tt = PLAN_TILE
    return pl.pallas_call(
        _plan_kernel,
        grid=(T // tt,),
        in_specs=[pl.BlockSpec((META_COLS, tt), lambda i: (0, i)), pl.BlockSpec((N_EXPERTS, 1), lambda i: (0, 0))],
        out_specs=pl.BlockSpec((META_COLS, tt), lambda i: (0, i)),
        out_shape=jax.ShapeDtypeStruct((META_COLS, T), jnp.int32),
        compiler_params=pltpu.CompilerParams(dimension_semantics=("parallel",),
                                             vmem_limit_bytes=VMEM_LIMIT_BYTES),
        name="plan",
    )(meta_t, base.astype(F32).reshape(N_EXPERTS, 1))


def _slab_rows(ref, row, n_rows):
    return ref.at[pl.ds(pl.multiple_of(row * ROW_SLABS, ROW_SLABS), n_rows * ROW_SLABS), :]


def _row_copy(src_ref, src_row, dst_ref, dst_row, sem):
    return pltpu.make_async_copy(_slab_rows(src_ref, src_row, 1), _slab_rows(dst_ref, dst_row, 1), sem)


def _pad_copy(zero_ref, xs_ref, row, sem):
    return pltpu.make_async_copy(zero_ref, _slab_rows(xs_ref, row, FFN_TILE), sem)


def _dispatch_kernel(pad_ref, p1_ref, p2_ref, h_ref, xs_ref, zero_ref, sem, zsem):
    tm = h_ref.shape[0] // ROW_SLABS
    n_tiles = xs_ref.shape[0] // (FFN_TILE * ROW_SLABS)

    @pl.when(pl.program_id(0) == 0)
    def _():
        zero_ref[...] = jnp.zeros(zero_ref.shape, F32)
        for e in range(N_EXPERTS):
            _pad_copy(zero_ref, xs_ref, pad_ref[e], zsem).start()
        for e in range(N_EXPERTS):
            _pad_copy(zero_ref, xs_ref, 0, zsem).wait()
        first_free = pad_ref[N_EXPERTS]

        def start_tail(k, carry):
            _pad_copy(zero_ref, xs_ref, k * FFN_TILE, zsem).start()
            return carry

        def wait_tail(k, carry):
            _pad_copy(zero_ref, xs_ref, 0, zsem).wait()
            return carry

        lax.fori_loop(first_free, n_tiles, start_tail, 0)
        lax.fori_loop(first_free, n_tiles, wait_tail, 0)

    def issue(t, carry):
        _row_copy(h_ref, t, xs_ref, p1_ref[0, 0, t], sem).start(priority=0)
        _row_copy(h_ref, t, xs_ref, p2_ref[0, 0, t], sem).start(priority=1)
        return carry

    lax.fori_loop(0, tm, issue, 0, unroll=ROW_DMA_UNROLL)

    def drain(t, carry):
        _row_copy(h_ref, 0, xs_ref, 0, sem).wait()
        _row_copy(h_ref, 0, xs_ref, 0, sem).wait()
        return carry

    lax.fori_loop(0, tm, drain, 0, unroll=ROW_DMA_UNROLL)


def _dispatch(h2, pos1, pos2, pad_start, n_rows):
    T = h2.shape[0] // ROW_SLABS
    tm = DISPATCH_TILE
    smem_row = pl.BlockSpec((1, 1, tm), lambda i, pad: (i, 0, 0), memory_space=pltpu.SMEM)
    return pl.pallas_call(
        _dispatch_kernel,
        grid_spec=pltpu.PrefetchScalarGridSpec(
            num_scalar_prefetch=1,
            grid=(T // tm,),
            in_specs=[smem_row, smem_row, pl.BlockSpec((tm * ROW_SLABS, LANES), lambda i, pad: (i, 0))],
            out_specs=pl.BlockSpec(memory_space=pl.ANY),
            scratch_shapes=[pltpu.VMEM((FFN_TILE * ROW_SLABS, LANES), F32), pltpu.SemaphoreType.DMA,
                            pltpu.SemaphoreType.DMA]),
        out_shape=jax.ShapeDtypeStruct(((n_rows + FFN_TILE) * ROW_SLABS, LANES), F32),
        compiler_params=pltpu.CompilerParams(dimension_semantics=("arbitrary",),
                                             vmem_limit_bytes=VMEM_LIMIT_BYTES),
        name="dispatch",
    )(pad_start, pos1.reshape(T // tm, 1, tm), pos2.reshape(T // tm, 1, tm), h2)


def _tile_in(xs_ref, tile, xbuf_ref, slot, sem_ref):
    return pltpu.make_async_copy(_slab_rows(xs_ref, tile * FFN_TILE, FFN_TILE), xbuf_ref.at[slot], sem_ref.at[slot])


def _tile_out(ybuf_ref, slot, ys_ref, tile, sem_ref):
    return pltpu.make_async_copy(ybuf_ref.at[slot], _slab_rows(ys_ref, tile * FFN_TILE, FFN_TILE), sem_ref.at[slot])


def _ffn_kernel(first_ref, ntile_ref, nused_ref, xs_ref, wg_ref, wu_ref, wd_ref, ys_ref,
                xbuf_ref, ybuf_ref, wgb_ref, wub_ref, wdb_ref, isem, osem):
    e = pl.program_id(0)
    t0 = first_ref[e]
    n = ntile_ref[e]
    n_tiles = ys_ref.shape[0] // (FFN_TILE * ROW_SLABS)

    @pl.when(n > 0)
    def _():
        wgb_ref[...] = wg_ref[0].astype(BF16)
        wub_ref[...] = wu_ref[0].astype(BF16)
        wdb_ref[...] = wd_ref[0].astype(BF16)
        _tile_in(xs_ref, t0, xbuf_ref, 0, isem).start()

        def two_tiles(k2, carry):
            for slot in (0, 1):
                k = 2 * k2 + slot

                @pl.when(k < n)
                def _():
                    _tile_in(xs_ref, t0 + k, xbuf_ref, slot, isem).wait()

                    @pl.when(k + 1 < n)
                    def _():
                        _tile_in(xs_ref, t0 + k + 1, xbuf_ref, 1 - slot, isem).start()

                    @pl.when(k >= 2)
                    def _():
                        _tile_out(ybuf_ref, slot, ys_ref, t0 + k - 2, osem).wait()


                    x = _load_slabs(xbuf_ref, FFN_TILE, (slot,)).astype(BF16)
                    a = jnp.dot(x, wgb_ref[...], preferred_element_type=F32)
                    b = jnp.dot(x, wub_ref[...], preferred_element_type=F32)
                    act = (a * jax.nn.sigmoid(a) * b).astype(BF16)
                    _store_slabs(ybuf_ref, jnp.dot(act, wdb_ref[...], preferred_element_type=F32), (slot,))
                    _tile_out(ybuf_ref, slot, ys_ref, t0 + k, osem).start()
            return carry

        lax.fori_loop(0, (n + 1) // 2, two_tiles, 0)
        for slot in (0, 1):
            @pl.when(n > slot)
            def _():
                _tile_out(ybuf_ref, slot, ys_ref, t0, osem).wait()

    @pl.when(e == pl.num_programs(0) - 1)
    def _():
        ybuf_ref[0] = jnp.zeros(ybuf_ref.shape[1:], F32)

        def start_tail(k, carry):
            _tile_out(ybuf_ref, 0, ys_ref, k, osem).start()
            return carry

        def wait_tail(k, carry):
            _tile_out(ybuf_ref, 0, ys_ref, k, osem).wait()
            return carry

        lax.fori_loop(nused_ref[0], n_tiles, start_tail, 0)
        lax.fori_loop(nused_ref[0], n_tiles, wait_tail, 0)


def _ffn(xs, first_tile, n_tile, n_used, n_rows, w_gate, w_up, w_down):
    wspec = lambda shape: pl.BlockSpec((1,) + shape, lambda e, *_: (e, 0, 0))
    tile_buf = pltpu.VMEM((2, FFN_TILE * ROW_SLABS, LANES), F32)
    return pl.pallas_call(
        _ffn_kernel,
        grid_spec=pltpu.PrefetchScalarGridSpec(
            num_scalar_prefetch=3,
            grid=(N_EXPERTS,),
            in_specs=[pl.BlockSpec(memory_space=pl.ANY),
                      wspec((D_MODEL, D_EXPERT)), wspec((D_MODEL, D_EXPERT)), wspec((D_EXPERT, D_MODEL))],
            out_specs=pl.BlockSpec(memory_space=pl.ANY),
            scratch_shapes=[tile_buf, tile_buf,
                            pltpu.VMEM((D_MODEL, D_EXPERT), BF16), pltpu.VMEM((D_MODEL, D_EXPERT), BF16),
                            pltpu.VMEM((D_EXPERT, D_MODEL), BF16),
                            pltpu.SemaphoreType.DMA((2,)), pltpu.SemaphoreType.DMA((2,))]),
        out_shape=jax.ShapeDtypeStruct((n_rows * ROW_SLABS, LANES), F32),
        compiler_params=pltpu.CompilerParams(dimension_semantics=("arbitrary",),
                                             vmem_limit_bytes=VMEM_LIMIT_BYTES),
        name="ffn",
    )(first_tile, n_tile, n_used, xs, w_gate, w_up, w_down)


def _combine_kernel(p1_ref, p2_ref, ys_ref, meta_ref, x1_ref, gf_ref, out_ref, buf_ref, sem):
    tm = x1_ref.shape[0]

    def issue(t, carry):
        _row_copy(ys_ref, p1_ref[0, 0, t], buf_ref.at[0], t, sem).start(priority=0)
        _row_copy(ys_ref, p2_ref[0, 0, t], buf_ref.at[1], t, sem).start(priority=1)
        return carry

    lax.fori_loop(0, tm, issue, 0, unroll=ROW_DMA_UNROLL)

    def drain(t, carry):
        _row_copy(ys_ref, 0, buf_ref.at[0], 0, sem).wait()
        _row_copy(ys_ref, 0, buf_ref.at[1], 0, sem).wait()
        return carry

    lax.fori_loop(0, tm, drain, 0, unroll=ROW_DMA_UNROLL)

    meta = meta_ref[...]
    y = x1_ref[...] + meta[:, 4:5] * _load_slabs(buf_ref, tm, (0,)) + meta[:, 5:6] * _load_slabs(buf_ref, tm, (1,))
    out_ref[...] = y * lax.rsqrt(jnp.mean(y * y, axis=-1, keepdims=True) + EPS) * gf_ref[...]


def _combine(ys, pos1, pos2, meta, x1, norm_final_g):
    T = x1.shape[0]
    tm = COMBINE_TILE
    smem_row = pl.BlockSpec((1, 1, tm), lambda i: (i, 0, 0), memory_space=pltpu.SMEM)
    row = lambda n: pl.BlockSpec((tm, n), lambda i: (i, 0))
    return pl.pallas_call(
        _combine_kernel,
        grid=(T // tm,),
        in_specs=[smem_row, smem_row, pl.BlockSpec(memory_space=pl.ANY), row(META_COLS), row(D_MODEL),
                  pl.BlockSpec((1, D_MODEL), lambda i: (0, 0))],
        out_specs=row(D_MODEL),
        out_shape=jax.ShapeDtypeStruct((T, D_MODEL), F32),
        scratch_shapes=[pltpu.VMEM((2, tm * ROW_SLABS, LANES), F32), pltpu.SemaphoreType.DMA],
        compiler_params=pltpu.CompilerParams(dimension_semantics=("arbitrary",),
                                             vmem_limit_bytes=VMEM_LIMIT_BYTES),
        name="combine",
    )(pos1.reshape(T // tm, 1, tm), pos2.reshape(T // tm, 1, tm), ys, meta, x1,
      norm_final_g.reshape(1, D_MODEL))


def _run_layout(counts):
    cnt = counts.reshape(-1).astype(jnp.int32)
    padded = (cnt + FFN_TILE - 1) // FFN_TILE * FFN_TILE
    ends = jnp.cumsum(padded)
    base = ends - padded
    n_used = (ends[-1:] // FFN_TILE).astype(jnp.int32)
    pad_start = jnp.concatenate([base + cnt, n_used])
    return base, pad_start, base // FFN_TILE, padded // FFN_TILE, n_used


def kernel(x, norm_mix_g, w_in, conv_w, conv_b, i_bias, f_bias, mlstm_norm_g, s5_lambda_re, s5_lambda_im,
           s5_log_dt, s5_b_re, s5_b_im, s5_c_re, s5_c_im, s5_d, s5_glu_w, s5_glu_b, w_out, norm_ffn_g,
           router_group_w, router_group_b, router_expert_w, router_expert_b, expert_w_gate, expert_w_up,
           expert_w_down, norm_final_g):
    batch, seq_len, d_model = x.shape
    assert d_model == D_MODEL and norm_mix_g.shape[0] == 1
    assert seq_len % MLSTM_CHUNK == 0 and seq_len % INPROJ_TILE == 0 and seq_len % S5_BLOCK == 0
    T = batch * seq_len
    x2d = x.reshape(T, D_MODEL)

    q, k, v, og, u, gcol, grow = _inproj(x2d, norm_mix_g[0], w_in[0], conv_w[0], conv_b[0], i_bias[0],
                                         f_bias[0], seq_len)
    out_a = _mlstm(q, k, v, og, gcol, grow, mlstm_norm_g[0], batch, seq_len)
    tables = _s5_tables(s5_lambda_re[0], s5_lambda_im[0], s5_log_dt[0], s5_b_re[0], s5_b_im[0], s5_c_re[0],
                        s5_c_im[0], s5_d[0], s5_glu_w[0], s5_glu_b[0])
    out_b = _s5(u, tables, batch, seq_len)
    x1, h2, meta, meta_t, counts = _outproj(out_a, out_b, x2d, w_out[0], norm_ffn_g[0], router_group_w[0],
                                            router_group_b[0], router_expert_w[0], router_expert_b[0])
    n_rows = 2 * T + N_EXPERTS * FFN_TILE
    base, pad_start, first_tile, n_tile, n_used = _run_layout(counts)
    pos = _plan(meta_t, base)
    xs = _dispatch(h2, pos[0], pos[1], pad_start, n_rows)
    ys = _ffn(xs, first_tile, n_tile, n_used, n_rows, expert_w_gate[0], expert_w_up[0], expert_w_down[0])
    y = _combine(ys, pos[0], pos[1], meta, x1, norm_final_g)
    return y.reshape(batch, seq_len, D_MODEL)
```

```python
import functools

import jax
import jax.numpy as jnp
from jax import lax
from jax.experimental import pallas as pl
from jax.experimental.pallas import tpu as pltpu

F32 = jnp.float32
BF16 = jnp.bfloat16
HIGHEST = lax.Precision.HIGHEST

D_MODEL = 1024
D_MLSTM = 512
D_S5 = 512
HEADS = 4
HEAD_DIM = 128
CONV_WIDTH = 4
S5_GROUP_CH = 16
S5_GROUPS = 32
S5_STATE = 64
N_GROUPS = 4
PER_GROUP = 8
N_EXPERTS = 32
D_EXPERT = 512
EPS = 1e-6
LAMBDA_RE_MAX = -1e-4

VMEM_LIMIT_BYTES = 56 * 1024 * 1024
LANES = 128
SUBLANES = 8
ROW_SLABS = D_MODEL // LANES
assert ROW_SLABS == SUBLANES

INPROJ_TILE = 1024
MLSTM_CHUNK = 256
S5_BLOCK = 16
S5_PAIR = 2
S5_STEP_GROUPS = LANES // S5_GROUP_CH
S5_STEP_PAIRS = S5_STEP_GROUPS // S5_PAIR
S5_SCAN_ROWS = 16
OUTPROJ_TILE = 512
META_COLS = 8
PLAN_TILE = 2048
DISPATCH_TILE = 512
FFN_TILE = 256
COMBINE_TILE = 1024
ROW_DMA_UNROLL = 8
FFN_SLOTS = 3
FFN_LOOKAHEAD = FFN_SLOTS - 1


def _nt_dims():
    return (((1,), (1,)), ((), ()))


def _tn_dims():
    return (((0,), (0,)), ((), ()))


def _rmsnorm(x, g):
    return x * lax.rsqrt(jnp.mean(x * x, axis=-1, keepdims=True) + EPS) * g


def _log_sigmoid(x):
    return jnp.minimum(x, 0.0) - jnp.log1p(jnp.exp(-jnp.abs(x)))


def _split_weight(w):
    w = jnp.pad(w.astype(F32), ((0, 0), (0, LANES - w.shape[1])))
    hi = w.astype(BF16)
    lo = (w - hi.astype(F32)).astype(BF16)
    return jnp.concatenate([hi, lo], axis=1), hi


def _dot_split(x, x_hi, w_hilo_ref, w_hi_ref):
    x_lo = (x - x_hi.astype(F32)).astype(BF16)
    two = jnp.dot(x_hi, w_hilo_ref[...], preferred_element_type=F32)
    return two[:, :LANES] + two[:, LANES:] + jnp.dot(x_lo, w_hi_ref[...], preferred_element_type=F32)


def _store_slabs(ref, value, lead=()):
    m = value.shape[0]
    for s in range(ROW_SLABS):
        ref[lead + (pl.ds(s, m, stride=ROW_SLABS), slice(None))] = value[:, s * LANES:(s + 1) * LANES]


def _load_slabs(ref, m, lead=()):
    return jnp.concatenate([ref[lead + (pl.ds(s, m, stride=ROW_SLABS), slice(None))] for s in range(ROW_SLABS)],
                           axis=1)


def _transpose_small(a):
    n = a.shape[1]
    eye = (lax.broadcasted_iota(jnp.int32, (n, n), 0) == lax.broadcasted_iota(jnp.int32, (n, n), 1)).astype(F32)
    return lax.dot_general(eye, a, _nt_dims(), precision=HIGHEST, preferred_element_type=F32)


def _inproj_kernel(tiles_per_seq, x_ref, g_ref, wqk_ref, wv_ref, wo_ref, wu_ref, wg2_ref, wg1_ref,
                   gb_ref, cw_ref, cb_ref,
                   q_ref, k_ref, v_ref, og_ref, u_ref, gcol_ref, grow_ref, ext_ref, us_ref):
    i = pl.program_id(0)
    tm = x_ref.shape[0]
    x = x_ref[...]
    h = x * lax.rsqrt(jnp.mean(x * x, axis=-1, keepdims=True) + EPS) * g_ref[...]
    hb = h.astype(BF16)

    qk_raw = jnp.dot(hb, wqk_ref[...], preferred_element_type=F32)

    @pl.when(i % tiles_per_seq == 0)
    def _():
        ext_ref[0:8, :] = jnp.zeros((8, ext_ref.shape[1]), F32)

    ext_ref[8:, :] = qk_raw
    cw = cw_ref[...]
    acc = cb_ref[...] + cw[CONV_WIDTH - 1:CONV_WIDTH, :] * qk_raw
    for s in range(1, CONV_WIDTH):
        acc = acc + cw[CONV_WIDTH - 1 - s:CONV_WIDTH - s, :] * ext_ref[pl.ds(8 - s, tm), :]
    ext_ref[0:8, :] = qk_raw[tm - 8:, :]
    qk = acc * jax.nn.sigmoid(acc)
    q_ref[...] = qk[:, :D_MLSTM].astype(BF16)
    k_ref[...] = (qk[:, D_MLSTM:] * (HEAD_DIM ** -0.5)).astype(BF16)

    v_ref[...] = jnp.dot(hb, wv_ref[...], preferred_element_type=F32).astype(BF16)
    o_pre = jnp.dot(hb, wo_ref[...], preferred_element_type=F32)
    og_ref[...] = jax.nn.sigmoid(o_pre).astype(BF16)

    u = jnp.dot(hb, wu_ref[...], preferred_element_type=F32)
    n_slabs = us_ref.shape[0]
    for j in range(n_slabs):
        us_ref[j] = u[:, j * LANES:(j + 1) * LANES]
    for p in range(S5_BLOCK):
        u_ref[p] = jnp.concatenate([us_ref[j, pl.ds(p, tm // S5_BLOCK, stride=S5_BLOCK), :]
                                    for j in range(n_slabs)], axis=1).astype(BF16)

    gcol = _dot_split(h, hb, wg2_ref, wg1_ref)[:, :2 * HEADS] + gb_ref[...]
    gcol_ref[...] = gcol
    grow_ref[...] = _transpose_small(gcol)


def _inproj(x2d, norm_g, w_in, conv_w, conv_b, i_bias, f_bias, seq_len):
    T = x2d.shape[0]
    tm = INPROJ_TILE
    dm = D_MLSTM
    wqk = w_in[:, :2 * dm].astype(BF16)
    wv = w_in[:, 2 * dm:3 * dm].astype(BF16)
    wo = w_in[:, 3 * dm:4 * dm].astype(BF16)
    wg2, wg1 = _split_weight(w_in[:, 4 * dm:4 * dm + 2 * HEADS])
    wu = w_in[:, 4 * dm + 2 * HEADS:].astype(BF16)
    gb = jnp.concatenate([i_bias, f_bias]).astype(F32)
    full = lambda shape: pl.BlockSpec(shape, lambda i: (0,) * len(shape))
    row = lambda n: pl.BlockSpec((tm, n), lambda i: (i, 0))
    return pl.pallas_call(
        functools.partial(_inproj_kernel, seq_len // tm),
        grid=(T // tm,),
        in_specs=[row(D_MODEL), full((1, D_MODEL)), full((D_MODEL, 2 * dm)), full((D_MODEL, dm)),
                  full((D_MODEL, dm)), full((D_MODEL, D_S5)), full((D_MODEL, 2 * LANES)),
                  full((D_MODEL, LANES)), full((1, 2 * HEADS)),
                  full((CONV_WIDTH, 2 * dm)), full((1, 2 * dm))],
        out_specs=[row(dm), row(dm), row(dm), row(dm),
                   pl.BlockSpec((S5_BLOCK, tm // S5_BLOCK, D_S5), lambda i: (0, i, 0)), row(2 * HEADS),
                   pl.BlockSpec((2 * HEADS, tm), lambda i: (0, i))],
        out_shape=[jax.ShapeDtypeStruct((T, dm), BF16)] * 4
        + [jax.ShapeDtypeStruct((S5_BLOCK, T // S5_BLOCK, D_S5), BF16),
           jax.ShapeDtypeStruct((T, 2 * HEADS), F32),
           jax.ShapeDtypeStruct((2 * HEADS, T), F32)],
        scratch_shapes=[pltpu.VMEM((tm + 8, 2 * dm), F32), pltpu.VMEM((D_S5 // LANES, tm, LANES), F32)],
        compiler_params=pltpu.CompilerParams(dimension_semantics=("arbitrary",),
                                             vmem_limit_bytes=VMEM_LIMIT_BYTES),
        name="inproj",
    )(x2d, norm_g.reshape(1, D_MODEL), wqk, wv, wo, wu, wg2, wg1, gb.reshape(1, -1),
      conv_w, conv_b.reshape(1, -1))


def _mlstm_kernel(batch, q_ref, k_ref, v_ref, og_ref, gcol_ref, ng_ref, *refs):
    grow_refs = refs[:batch]
    out_ref, c_ref, m_ref = refs[batch:]
    tc = q_ref.shape[1]
    d = HEAD_DIM

    @pl.when(pl.program_id(0) == 0)
    def _():
        c_ref[...] = jnp.zeros(c_ref.shape, F32)
        m_ref[...] = jnp.zeros(m_ref.shape, F32)

    causal = lax.broadcasted_iota(jnp.int32, (tc, tc), 0) >= lax.broadcasted_iota(jnp.int32, (tc, tc), 1)
    ones_col = (lax.broadcasted_iota(jnp.int32, (tc, d), 1) == 0).astype(BF16)
    for b in range(batch):
        gcol = gcol_ref[b]
        grow = grow_refs[b][...]
        b_col_all = jnp.dot(causal.astype(F32), _log_sigmoid(gcol), precision=HIGHEST,
                            preferred_element_type=F32)
        b_row_all = _transpose_small(b_col_all)
        for h in range(HEADS):
            _mlstm_head(b, h, causal, ones_col, gcol, grow, b_col_all, b_row_all,
                        q_ref, k_ref, v_ref, og_ref, ng_ref, out_ref, c_ref, m_ref)


def _mlstm_head(b, h, causal, ones_col, gcol, grow, b_col_all, b_row_all,
                q_ref, k_ref, v_ref, og_ref, ng_ref, out_ref, c_ref, m_ref):
    tc, d = q_ref.shape[1], HEAD_DIM
    hs = slice(h * d, (h + 1) * d)
    st = b * HEADS + h
    li_col = gcol[:, h:h + 1]
    li_row = grow[h:h + 1, :]
    b_col = b_col_all[:, HEADS + h:HEADS + h + 1]
    b_row = b_row_all[HEADS + h:HEADS + h + 1, :]
    m_prev = m_ref[st:st + 1, 0:1]

    dmat = jnp.where(causal, b_col - b_row + li_row, -jnp.inf)
    inter = b_col + m_prev
    m_row = jnp.maximum(inter, jnp.max(dmat, axis=1, keepdims=True))
    w_intra = jnp.exp(dmat - m_row)
    w_inter = jnp.exp(inter - m_row)

    qh = q_ref[b, :, hs]
    kh = k_ref[b, :, hs]
    v_aug = jnp.concatenate([v_ref[b, :, hs], ones_col], axis=1)
    s = lax.dot_general(qh, kh, _nt_dims(), preferred_element_type=F32) * w_intra
    c_aug = c_ref[st]
    tot = (jnp.dot(s.astype(BF16), v_aug, preferred_element_type=F32)
           + w_inter * lax.dot_general(qh, c_aug.astype(BF16), _nt_dims(), preferred_element_type=F32))
    num = tot[:, :d]
    den = tot[:, d:d + 1]
    hh = num / jnp.maximum(jnp.abs(den), jnp.exp(-m_row))
    hh = hh * lax.rsqrt(jnp.mean(hh * hh, axis=-1, keepdims=True) + EPS)
    out_ref[b, :, hs] = (og_ref[b, :, hs].astype(F32) * (hh * ng_ref[:, hs])).astype(BF16)

    b_last = b_col[tc - 1:tc, :]
    d_state = b_last - b_col + li_col
    m_new = jnp.maximum(b_last + m_prev, jnp.max(d_state, axis=0, keepdims=True))
    w_s = jnp.exp(d_state - m_new)
    w_c = jnp.exp(b_last + m_prev - m_new)
    upd = lax.dot_general((v_aug.astype(F32) * w_s).astype(BF16), kh, _tn_dims(),
                          preferred_element_type=F32)
    c_ref[st] = w_c * c_aug + upd
    m_ref[st:st + 1, :] = jnp.broadcast_to(m_new, (1, m_ref.shape[1]))


def _mlstm(q, k, v, og, gcol, grow, norm_g, batch, seq_len):
    tc = MLSTM_CHUNK
    nc = seq_len // tc
    T = q.shape[0]
    n_state = batch * HEADS
    seq = lambda a: a.reshape(batch, seq_len, a.shape[1])
    rows = lambda n: pl.BlockSpec((batch, tc, n), lambda c: (0, c, 0))
    grow_specs = [pl.BlockSpec((2 * HEADS, tc), lambda c, b=b: (0, b * nc + c)) for b in range(batch)]
    out = pl.pallas_call(
        functools.partial(_mlstm_kernel, batch),
        grid=(nc,),
        in_specs=[rows(D_MLSTM), rows(D_MLSTM), rows(D_MLSTM), rows(D_MLSTM), rows(2 * HEADS),
                  pl.BlockSpec((1, D_MLSTM), lambda c: (0, 0))] + grow_specs,
        out_specs=rows(D_MLSTM),
        out_shape=jax.ShapeDtypeStruct((batch, seq_len, D_MLSTM), BF16),
        scratch_shapes=[pltpu.VMEM((n_state, 2 * HEAD_DIM, HEAD_DIM), F32),
                        pltpu.VMEM((-(-n_state // SUBLANES) * SUBLANES, LANES), F32)],
        compiler_params=pltpu.CompilerParams(dimension_semantics=("arbitrary",),
                                             vmem_limit_bytes=VMEM_LIMIT_BYTES),
        name="mlstm",
    )(seq(q), seq(k), seq(v), seq(og), seq(gcol), norm_g.reshape(1, D_MLSTM), *([grow] * batch))
    return out.reshape(T, D_MLSTM)


def _cplx_pow(exponent, theta_re, theta_im):
    mag = jnp.exp(exponent * theta_re)
    return mag * jnp.cos(exponent * theta_im), mag * jnp.sin(exponent * theta_im)


def _s5_tables_kernel(lrr_ref, lir_ref, dtr_ref, lrc_ref, lic_ref, dtc_ref, btr_ref, bti_ref, ctr_ref, cti_ref,
                      dcol_ref, gw1_ref, gw2_ref, gbcol_ref,
                      we_ref, kt_ref, wc_ref, glu_ref, glub_ref, dsk_ref, sr_ref, si_ref):
    P, C, N = S5_BLOCK, S5_GROUP_CH, S5_STATE
    PC, NN = P * C, S5_PAIR * S5_STATE
    shift_c, shift_n = C.bit_length() - 1, N.bit_length() - 1

    lr = jnp.minimum(lrr_ref[0], LAMBDA_RE_MAX)
    li = lir_ref[0]
    dt = jnp.exp(dtr_ref[0])
    th_r, th_i = lr * dt, li * dt
    ar, ai = _cplx_pow(1.0, th_r, th_i)
    den = lr * lr + li * li
    cr = ((ar - 1.0) * lr + ai * li) / den
    ci = (ai * lr - (ar - 1.0) * li) / den
    btr, bti = btr_ref[0], bti_ref[0]
    bbr = cr * btr - ci * bti
    bbi = cr * bti + ci * btr

    back = (P - 1 - lax.broadcasted_iota(jnp.int32, (P, 1), 0)).astype(F32)
    pwr, pwi = _cplx_pow(back, th_r, th_i)
    val_r = jnp.concatenate([pwr[p:p + 1] * bbr - pwi[p:p + 1] * bbi for p in range(P)], axis=0)
    val_i = jnp.concatenate([pwr[p:p + 1] * bbi + pwi[p:p + 1] * bbr for p in range(P)], axis=0)
    lane_grp = lax.broadcasted_iota(jnp.int32, (PC, NN), 1) >> shift_n
    we_r = jnp.concatenate([jnp.where(lane_grp == g, val_r, 0.0) for g in range(S5_PAIR)], axis=0)
    we_i = jnp.concatenate([jnp.where(lane_grp == g, val_i, 0.0) for g in range(S5_PAIR)], axis=0)
    we_ref[0] = jnp.concatenate([we_r, we_i], axis=1).astype(BF16)

    steps = (P << lax.broadcasted_iota(jnp.int32, (S5_SCAN_ROWS, 1), 0)).astype(F32)
    sr_ref[0], si_ref[0] = _cplx_pow(steps, th_r, th_i)

    lrc = jnp.minimum(lrc_ref[0], LAMBDA_RE_MAX)
    dtc = jnp.exp(dtc_ref[0])
    thc_r, thc_i = lrc * dtc, lic_ref[0] * dtc
    lane = lax.broadcasted_iota(jnp.int32, (NN, PC), 1)
    p0r, p0i = _cplx_pow((lane >> shift_c).astype(F32), thc_r, thc_i)
    acr, aci = _cplx_pow(1.0, thc_r, thc_i)
    p1r, p1i = p0r * acr - p0i * aci, p0r * aci + p0i * acr
    tile_c = ((lax.broadcasted_iota(jnp.int32, (C, PC), 1) & (C - 1))
              == lax.broadcasted_iota(jnp.int32, (C, PC), 0)).astype(F32)
    ctr = jnp.dot(ctr_ref[0], tile_c, precision=HIGHEST, preferred_element_type=F32)
    cti = jnp.dot(cti_ref[0], tile_c, precision=HIGHEST, preferred_element_type=F32)
    w0r, w0i = ctr * p0r - cti * p0i, ctr * p0i + cti * p0r
    vr, vi = ctr * p1r - cti * p1i, -(ctr * p1i + cti * p1r)
    row_grp = lax.broadcasted_iota(jnp.int32, (NN, PC), 0) >> shift_n
    wc_r = jnp.concatenate([jnp.where(row_grp == g, vr, 0.0) for g in range(S5_PAIR)], axis=1)
    wc_i = jnp.concatenate([jnp.where(row_grp == g, vi, 0.0) for g in range(S5_PAIR)], axis=1)
    wc_ref[0] = jnp.concatenate([wc_r, wc_i], axis=0).astype(BF16)

    b_grp = lax.broadcasted_iota(jnp.int32, (C, NN), 1) >> shift_n
    lane_k = lax.broadcasted_iota(jnp.int32, (C, PC), 1)
    same_pos = ((lax.broadcasted_iota(jnp.int32, (PC, PC), 0) >> shift_c)
                == (lax.broadcasted_iota(jnp.int32, (PC, PC), 1) >> shift_c))
    zero_blk = jnp.zeros((PC, PC), F32)

    def on_diagonal(blk, g):
        return jnp.concatenate([blk if j == g else zero_blk for j in range(S5_PAIR)], axis=1)

    kt_rows, glu_v, glu_g, bias_v, bias_g, skip = [], [], [], [], [], []
    for g in range(S5_PAIR):
        k0 = (jnp.dot(jnp.where(b_grp == g, bbr, 0.0), w0r, precision=HIGHEST, preferred_element_type=F32)
              - jnp.dot(jnp.where(b_grp == g, bbi, 0.0), w0i, precision=HIGHEST, preferred_element_type=F32))
        toep = jnp.concatenate(
            [k0 if p == 0 else jnp.where(lane_k >= C * p, pltpu.roll(k0, C * p, axis=1), 0.0) for p in range(P)],
            axis=0)
        kt_rows.append(on_diagonal(toep, g))
        for w_ref, dst in ((gw1_ref, glu_v), (gw2_ref, glu_g)):
            t = jnp.dot(w_ref[0, g], tile_c, precision=HIGHEST, preferred_element_type=F32)
            dst.append(on_diagonal(jnp.where(same_pos, jnp.concatenate([t] * P, axis=0), 0.0), g))
        gb = gbcol_ref[0, g]
        bias_v.append(jnp.sum(gb[:C] * tile_c, axis=0, keepdims=True))
        bias_g.append(jnp.sum(gb[C:] * tile_c, axis=0, keepdims=True))
        skip.append(jnp.sum(dcol_ref[0, g] * tile_c, axis=0, keepdims=True))
    kt_ref[0] = jnp.concatenate(kt_rows, axis=0).astype(BF16)
    glu_ref[0] = jnp.concatenate([jnp.concatenate(glu_v, axis=0), jnp.concatenate(glu_g, axis=0)],
                                 axis=1).astype(BF16)
    glub_ref[0] = jnp.concatenate(bias_v + bias_g, axis=1)
    dsk_ref[0] = jnp.concatenate(skip, axis=1)


def _s5_tables(lam_re, lam_im, log_dt, b_re, b_im, c_re, c_im, d_skip, glu_w, glu_b):
    G, N, C, P = S5_GROUPS, S5_STATE, S5_GROUP_CH, S5_BLOCK
    J, NN, PC = G // S5_PAIR, S5_PAIR * S5_STATE, S5_BLOCK * S5_GROUP_CH
    f = lambda a: a.astype(F32)
    dt_full = jnp.broadcast_to(f(log_dt)[:, None], (G, N))
    rows = [a.reshape(J, 1, NN) for a in (f(lam_re), f(lam_im), dt_full)]
    cols = [a.reshape(J, NN, 1) for a in (f(lam_re), f(lam_im), dt_full)]
    bt = [f(a).reshape(J, S5_PAIR, N, C).transpose(0, 3, 1, 2).reshape(J, C, NN) for a in (b_re, b_im)]
    ct = [f(a).transpose(0, 2, 1).reshape(J, NN, C) for a in (c_re, c_im)]
    dcol = f(d_skip).reshape(J, S5_PAIR, C, 1)
    gw1 = f(glu_w)[..., :C].reshape(J, S5_PAIR, C, C)
    gw2 = f(glu_w)[..., C:].reshape(J, S5_PAIR, C, C)
    gbcol = f(glu_b).reshape(J, S5_PAIR, 2 * C, 1)
    ins = rows + cols + bt + ct + [dcol, gw1, gw2, gbcol]
    per_pair = lambda shape: pl.BlockSpec((1,) + shape, lambda j: (j,) + (0,) * len(shape))
    out_shapes = [((S5_PAIR * PC, 2 * NN), BF16), ((S5_PAIR * PC, S5_PAIR * PC), BF16),
                  ((2 * NN, S5_PAIR * PC), BF16), ((S5_PAIR * PC, 2 * S5_PAIR * PC), BF16),
                  ((1, 2 * S5_PAIR * PC), F32), ((1, S5_PAIR * PC), F32),
                  ((S5_SCAN_ROWS, NN), F32), ((S5_SCAN_ROWS, NN), F32)]
    return pl.pallas_call(
        _s5_tables_kernel,
        grid=(J,),
        in_specs=[per_pair(a.shape[1:]) for a in ins],
        out_specs=[per_pair(s) for s, _ in out_shapes],
        out_shape=[jax.ShapeDtypeStruct((J,) + s, d) for s, d in out_shapes],
        compiler_params=pltpu.CompilerParams(dimension_semantics=("parallel",),
                                             vmem_limit_bytes=VMEM_LIMIT_BYTES),
        name="s5_tables",
    )(*ins)


def _s5_kernel(n_steps, u_ref, we_ref, kt_ref, wc_ref, glu_ref, glub_ref, dsk_ref, sr_ref, si_ref, out_ref):
    P, C = S5_BLOCK, S5_GROUP_CH
    nb, slab = u_ref.shape[1], u_ref.shape[2]
    wide = slab * P
    pair_w = S5_PAIR * P * C
    half = S5_PAIR * S5_STATE
    shift_c = C.bit_length() - 1

    lane = lax.broadcasted_iota(jnp.int32, (slab, wide), 1)
    src = lax.broadcasted_iota(jnp.int32, (slab, wide), 0)
    target0 = ((src >> shift_c) << (shift_c + P.bit_length() - 1)) + (src & (C - 1))

    def place(p0):
        return jnp.concatenate([jnp.where(lane == target0 + p * C, 1.0, 0.0).astype(BF16)
                                for p in (p0, p0 + 1)], axis=0)

    u_blk = jnp.zeros((nb, wide), F32)
    for p0 in range(0, P, 2):
        lhs = jnp.concatenate([u_ref[p0], u_ref[p0 + 1]], axis=1)
        u_blk = u_blk + jnp.dot(lhs, place(p0), preferred_element_type=F32)
    u_blk = u_blk.astype(BF16)

    row = lax.broadcasted_iota(jnp.int32, (nb, half), 0)
    outs = []
    for q in range(S5_STEP_PAIRS):
        u = u_blk[:, q * pair_w:(q + 1) * pair_w]
        e = jnp.dot(u, we_ref[q], preferred_element_type=F32)
        xr, xi = e[:, :half], e[:, half:]
        for k in range(n_steps):
            sh = 1 << k
            ar = sr_ref[q, k:k + 1, :]
            ai = si_ref[q, k:k + 1, :]
            valid = row >= sh
            yr = jnp.where(valid, pltpu.roll(xr, sh, axis=0), 0.0)
            yi = jnp.where(valid, pltpu.roll(xi, sh, axis=0), 0.0)
            xr, xi = xr + ar * yr - ai * yi, xi + ar * yi + ai * yr
        first = row >= 1
        pr = jnp.where(first, pltpu.roll(xr, 1, axis=0), 0.0)
        pi = jnp.where(first, pltpu.roll(xi, 1, axis=0), 0.0)
        sprev = jnp.concatenate([pr, pi], axis=1).astype(BF16)
        y = (jnp.dot(u, kt_ref[q], preferred_element_type=F32)
             + jnp.dot(sprev, wc_ref[q], preferred_element_type=F32)
             + dsk_ref[q] * u.astype(F32))
        y = jax.nn.gelu(y)
        z = jnp.dot(y.astype(BF16), glu_ref[q], preferred_element_type=F32) + glub_ref[q]
        outs.append(z[:, :pair_w] * jax.nn.sigmoid(z[:, pair_w:]))
    out_blk = jnp.concatenate(outs, axis=1).astype(BF16)
    for p0 in range(0, P, 2):
        o = lax.dot_general(out_blk, place(p0), _nt_dims(), preferred_element_type=F32)
        out_ref[p0] = o[:, :slab].astype(BF16)
        out_ref[p0 + 1] = o[:, slab:].astype(BF16)


def _s5(u, tables, batch, seq_len):
    nb = seq_len // S5_BLOCK
    n_steps = max(1, (nb - 1).bit_length())
    assert n_steps <= S5_SCAN_ROWS
    slab = pl.BlockSpec((S5_BLOCK, nb, LANES), lambda b, o: (0, b, o))
    per_step = lambda a: pl.BlockSpec((S5_STEP_PAIRS,) + a.shape[1:], lambda b, o: (o, 0, 0))
    return pl.pallas_call(
        functools.partial(_s5_kernel, n_steps),
        grid=(batch, S5_GROUPS // S5_STEP_GROUPS),
        in_specs=[slab] + [per_step(a) for a in tables],
        out_specs=slab,
        out_shape=jax.ShapeDtypeStruct(u.shape, BF16),
        compiler_params=pltpu.CompilerParams(dimension_semantics=("parallel", "parallel"),
                                             vmem_limit_bytes=VMEM_LIMIT_BYTES),
        name="s5",
    )(u, *tables)


def _outproj_kernel(a_ref, b_ref, x_ref, wa_ref, wb_ref, g_ref, rw2_ref, rw1_ref, rgb_ref, reb_ref,
                    x1_ref, meta_ref, metat_ref, cnt_ref, bs_ref):
    tm = x_ref.shape[0]

    @pl.when(pl.program_id(0) == 0)
    def _():
        cnt_ref[...] = jnp.zeros(cnt_ref.shape, F32)

    n_slabs = bs_ref.shape[0]
    for p in range(S5_BLOCK):
        bp = b_ref[p].astype(F32)
        for j in range(n_slabs):
            bs_ref[j, pl.ds(p, tm // S5_BLOCK, stride=S5_BLOCK), :] = bp[:, j * LANES:(j + 1) * LANES]
    out_b = jnp.concatenate([bs_ref[j] for j in range(n_slabs)], axis=1).astype(BF16)
    x1 = (x_ref[...]
          + jnp.dot(a_ref[...], wa_ref[...], preferred_element_type=F32)
          + jnp.dot(out_b, wb_ref[...], preferred_element_type=F32))
    x1_ref[...] = x1
    h2 = _rmsnorm(x1, g_ref[...])

    logits = _dot_split(h2, h2.astype(BF16), rw2_ref, rw1_ref)
    el = logits[:, :N_EXPERTS] + reb_ref[...]
    gl = logits[:, N_EXPERTS:N_EXPERTS + N_GROUPS] + rgb_ref[...]
    gmax = jnp.max(gl, axis=1, keepdims=True)
    g_w = 1.0 / jnp.sum(jnp.exp(gl - gmax), axis=1, keepdims=True)
    lane_g = lax.broadcasted_iota(jnp.int32, (tm, N_GROUPS), 1)
    g_idx = jnp.min(jnp.where(gl == gmax, lane_g, N_GROUPS), axis=1, keepdims=True)
    lane_e = lax.broadcasted_iota(jnp.int32, (tm, N_EXPERTS), 1)
    elm = jnp.where(lane_e // PER_GROUP == g_idx, el, -jnp.inf)
    m1 = jnp.max(elm, axis=1, keepdims=True)
    i1 = jnp.min(jnp.where(elm == m1, lane_e, N_EXPERTS), axis=1, keepdims=True)
    elm2 = jnp.where(lane_e == i1, -jnp.inf, elm)
    m2 = jnp.max(elm2, axis=1, keepdims=True)
    i2 = jnp.min(jnp.where(elm2 == m2, lane_e, N_EXPERTS), axis=1, keepdims=True)
    e2 = jnp.exp(m2 - m1)
    w1 = g_w / (1.0 + e2)
    w2 = g_w * e2 / (1.0 + e2)

    sel1 = lane_e == i1
    sel2 = lane_e == i2
    picked = (sel1 | sel2).astype(BF16)
    rix = lax.broadcasted_iota(jnp.int32, (tm, tm), 0)
    cix = lax.broadcasted_iota(jnp.int32, (tm, tm), 1)
    before = jnp.dot((rix > cix).astype(BF16), picked, preferred_element_type=F32) + cnt_ref[...]
    r1 = jnp.sum(jnp.where(sel1, before, 0.0), axis=1, keepdims=True)
    r2 = jnp.sum(jnp.where(sel2, before, 0.0), axis=1, keepdims=True)
    cnt_ref[...] += jnp.sum(picked.astype(F32), axis=0, keepdims=True)
    col = lax.broadcasted_iota(jnp.int32, (tm, META_COLS), 1)
    meta = jnp.zeros((tm, META_COLS), F32)
    for c, val in enumerate((i1.astype(F32), i2.astype(F32), r1, r2, w1, w2)):
        meta = jnp.where(col == c, val, meta)
    meta_ref[...] = meta
    metat_ref[...] = _transpose_small(meta)


def _outproj(out_a, out_b, x2d, w_out, norm_g, rg_w, rg_b, re_w, re_b):
    T = x2d.shape[0]
    tm = OUTPROJ_TILE
    wa = w_out[:D_MLSTM].astype(BF16)
    wb = w_out[D_MLSTM:].astype(BF16)
    rw2, rw1 = _split_weight(jnp.concatenate([re_w, rg_w], axis=1))
    full = lambda shape: pl.BlockSpec(shape, lambda i: (0,) * len(shape))
    row = lambda n: pl.BlockSpec((tm, n), lambda i: (i, 0))
    return pl.pallas_call(
        _outproj_kernel,
        grid=(T // tm,),
        in_specs=[row(D_MLSTM), pl.BlockSpec((S5_BLOCK, tm // S5_BLOCK, D_S5), lambda i: (0, i, 0)), row(D_MODEL),
                  full((D_MLSTM, D_MODEL)), full((D_S5, D_MODEL)),
                  full((1, D_MODEL)), full((D_MODEL, 2 * LANES)), full((D_MODEL, LANES)), full((1, N_GROUPS)),
                  full((1, N_EXPERTS))],
        out_specs=[row(D_MODEL), row(META_COLS), pl.BlockSpec((META_COLS, tm), lambda i: (0, i)),
                   full((1, N_EXPERTS))],
        out_shape=[jax.ShapeDtypeStruct((T, D_MODEL), F32),
                   jax.ShapeDtypeStruct((T, META_COLS), F32), jax.ShapeDtypeStruct((META_COLS, T), F32),
                   jax.ShapeDtypeStruct((1, N_EXPERTS), F32)],
        scratch_shapes=[pltpu.VMEM((D_S5 // LANES, tm, LANES), F32)],
        compiler_params=pltpu.CompilerParams(dimension_semantics=("arbitrary",),
                                             vmem_limit_bytes=VMEM_LIMIT_BYTES),
        name="outproj",
    )(out_a, out_b, x2d, wa, wb, norm_g.reshape(1, D_MODEL), rw2, rw1, rg_b.reshape(1, -1), re_b.reshape(1, -1))


def _plan_kernel(metat_ref, base_ref, pos_ref):
    mt = metat_ref[...]
    tt = mt.shape[1]
    expert = lax.broadcasted_iota(jnp.int32, (N_EXPERTS, tt), 0)
    base = base_ref[...]
    row = lax.broadcasted_iota(jnp.int32, (META_COLS, tt), 0)
    pos = jnp.zeros((META_COLS, tt), F32)
    for k in range(2):
        first = jnp.sum(jnp.where(expert == mt[k:k + 1, :].astype(jnp.int32), base, 0.0), axis=0, keepdims=True)
        pos = jnp.where(row == k, first + mt[2 + k:3 + k, :], pos)
    pos_ref[...] = pos.astype(jnp.int32)


def _plan(meta_t, base):
    T = meta_t.shape[1]
    tt = PLAN_TILE
    return pl.pallas_call(
        _plan_kernel,
        grid=(T // tt,),
        in_specs=[pl.BlockSpec((META_COLS, tt), lambda i: (0, i)), pl.BlockSpec((N_EXPERTS, 1), lambda i: (0, 0))],
        out_specs=pl.BlockSpec((META_COLS, tt), lambda i: (0, i)),
        out_shape=jax.ShapeDtypeStruct((META_COLS, T), jnp.int32),
        compiler_params=pltpu.CompilerParams(dimension_semantics=("parallel",),
                                             vmem_limit_bytes=VMEM_LIMIT_BYTES),
        name="plan",
    )(meta_t, base.astype(F32).reshape(N_EXPERTS, 1))


def _slab_rows(ref, row, n_rows):
    return ref.at[pl.ds(pl.multiple_of(row * ROW_SLABS, ROW_SLABS), n_rows * ROW_SLABS), :]


def _row_copy(src_ref, src_row, dst_ref, dst_row, sem):
    return pltpu.make_async_copy(_slab_rows(src_ref, src_row, 1), _slab_rows(dst_ref, dst_row, 1), sem)


def _pad_copy(zero_ref, xs_ref, row, sem):
    return pltpu.make_async_copy(zero_ref, _slab_rows(xs_ref, row, FFN_TILE), sem)


def _dispatch_kernel(pad_ref, p1_ref, p2_ref, x1_ref, g_ref, xs_ref, h_ref, zero_ref, sem, zsem):
    tm = x1_ref.shape[0]
    n_tiles = xs_ref.shape[0] // (FFN_TILE * ROW_SLABS)
    _store_slabs(h_ref, _rmsnorm(x1_ref[...], g_ref[...]))

    @pl.when(pl.program_id(0) == 0)
    def _():
        zero_ref[...] = jnp.zeros(zero_ref.shape, F32)
        for e in range(N_EXPERTS):
            _pad_copy(zero_ref, xs_ref, pad_ref[e], zsem).start()
        for e in range(N_EXPERTS):
            _pad_copy(zero_ref, xs_ref, 0, zsem).wait()
        first_free = pad_ref[N_EXPERTS]

        def start_tail(k, carry):
            _pad_copy(zero_ref, xs_ref, k * FFN_TILE, zsem).start()
            return carry

        def wait_tail(k, carry):
            _pad_copy(zero_ref, xs_ref, 0, zsem).wait()
            return carry

        lax.fori_loop(first_free, n_tiles, start_tail, 0)
        lax.fori_loop(first_free, n_tiles, wait_tail, 0)

    def issue(t, carry):
        _row_copy(h_ref, t, xs_ref, p1_ref[0, 0, t], sem).start(priority=0)
        _row_copy(h_ref, t, xs_ref, p2_ref[0, 0, t], sem).start(priority=1)
        return carry

    lax.fori_loop(0, tm, issue, 0, unroll=ROW_DMA_UNROLL)

    def drain(t, carry):
        _row_copy(h_ref, 0, xs_ref, 0, sem).wait()
        _row_copy(h_ref, 0, xs_ref, 0, sem).wait()
        return carry

    lax.fori_loop(0, tm, drain, 0, unroll=ROW_DMA_UNROLL)


def _dispatch(x1, norm_g, pos1, pos2, pad_start, n_rows):
    T = x1.shape[0]
    tm = DISPATCH_TILE
    smem_row = pl.BlockSpec((1, 1, tm), lambda i, pad: (i, 0, 0), memory_space=pltpu.SMEM)
    return pl.pallas_call(
        _dispatch_kernel,
        grid_spec=pltpu.PrefetchScalarGridSpec(
            num_scalar_prefetch=1,
            grid=(T // tm,),
            in_specs=[smem_row, smem_row, pl.BlockSpec((tm, D_MODEL), lambda i, pad: (i, 0)),
                      pl.BlockSpec((1, D_MODEL), lambda i, pad: (0, 0))],
            out_specs=pl.BlockSpec(memory_space=pl.ANY),
            scratch_shapes=[pltpu.VMEM((tm * ROW_SLABS, LANES), F32),
                            pltpu.VMEM((FFN_TILE * ROW_SLABS, LANES), F32), pltpu.SemaphoreType.DMA,
                            pltpu.SemaphoreType.DMA]),
        out_shape=jax.ShapeDtypeStruct(((n_rows + FFN_TILE) * ROW_SLABS, LANES), F32),
        compiler_params=pltpu.CompilerParams(dimension_semantics=("arbitrary",),
                                             vmem_limit_bytes=VMEM_LIMIT_BYTES),
        name="dispatch",
    )(pad_start, pos1.reshape(T // tm, 1, tm), pos2.reshape(T // tm, 1, tm), x1, norm_g.reshape(1, D_MODEL))


def _tile_in(xs_ref, tile, xbuf_ref, slot, sem_ref):
    return pltpu.make_async_copy(_slab_rows(xs_ref, tile * FFN_TILE, FFN_TILE), xbuf_ref.at[slot], sem_ref.at[slot])


def _tile_out(ybuf_ref, slot, ys_ref, tile, sem_ref):
    return pltpu.make_async_copy(ybuf_ref.at[slot], _slab_rows(ys_ref, tile * FFN_TILE, FFN_TILE), sem_ref.at[slot])


def _ffn_kernel(first_ref, ntile_ref, nused_ref, xs_ref, wg_ref, wu_ref, wd_ref, ys_ref,
                xbuf_ref, ybuf_ref, wgb_ref, wub_ref, wdb_ref, busy_ref, isem, osem):
    e = pl.program_id(0)
    t0 = first_ref[e]
    n = ntile_ref[e]
    n_tiles = ys_ref.shape[0] // (FFN_TILE * ROW_SLABS)
    n_experts = pl.num_programs(0)

    def fetch_first_tiles(expert):
        for k in range(FFN_LOOKAHEAD):
            @pl.when(k < ntile_ref[expert])
            def _():
                _tile_in(xs_ref, first_ref[expert] + k, xbuf_ref, k, isem).start()

    @pl.when(e == 0)
    def _():
        for slot in range(FFN_SLOTS):
            busy_ref[slot] = 0
        fetch_first_tiles(0)

    @pl.when(n > 0)
    def _():
        wgb_ref[...] = wg_ref[0].astype(BF16)
        wub_ref[...] = wu_ref[0].astype(BF16)
        wdb_ref[...] = wd_ref[0].astype(BF16)

        def ring_pass(kr, carry):
            for slot in range(FFN_SLOTS):
                k = FFN_SLOTS * kr + slot

                @pl.when(k < n)
                def _():
                    _tile_in(xs_ref, t0 + k, xbuf_ref, slot, isem).wait()

                    @pl.when(k + FFN_LOOKAHEAD < n)
                    def _():
                        _tile_in(xs_ref, t0 + k + FFN_LOOKAHEAD, xbuf_ref, (slot + FFN_LOOKAHEAD) % FFN_SLOTS,
                                 isem).start()

                    @pl.when(busy_ref[slot] == 1)
                    def _():
                        _tile_out(ybuf_ref, slot, ys_ref, t0, osem).wait()

                    x = _load_slabs(xbuf_ref, FFN_TILE, (slot,)).astype(BF16)
                    a = jnp.dot(x, wgb_ref[...], preferred_element_type=F32)
                    b = jnp.dot(x, wub_ref[...], preferred_element_type=F32)
                    act = (a * jax.nn.sigmoid(a) * b).astype(BF16)
                    _store_slabs(ybuf_ref, jnp.dot(act, wdb_ref[...], preferred_element_type=F32), (slot,))
                    _tile_out(ybuf_ref, slot, ys_ref, t0 + k, osem).start()
                    busy_ref[slot] = 1
            return carry

        lax.fori_loop(0, (n + FFN_SLOTS - 1) // FFN_SLOTS, ring_pass, 0)

        @pl.when(e + 1 < n_experts)
        def _():
            fetch_first_tiles(e + 1)


    @pl.when((n == 0) & (e + 1 < n_experts))
    def _():
        fetch_first_tiles(e + 1)

    @pl.when(e == n_experts - 1)
    def _():
        for slot in range(FFN_SLOTS):
            @pl.when(busy_ref[slot] == 1)
            def _():
                _tile_out(ybuf_ref, slot, ys_ref, 0, osem).wait()
        ybuf_ref[0] = jnp.zeros(ybuf_ref.shape[1:], F32)

        def start_tail(k, carry):
            _tile_out(ybuf_ref, 0, ys_ref, k, osem).start()
            return carry

        def wait_tail(k, carry):
            _tile_out(ybuf_ref, 0, ys_ref, k, osem).wait()
            return carry

        lax.fori_loop(nused_ref[0], n_tiles, start_tail, 0)
        lax.fori_loop(nused_ref[0], n_tiles, wait_tail, 0)


def _ffn(xs, first_tile, n_tile, n_used, n_rows, w_gate, w_up, w_down):
    wspec = lambda shape: pl.BlockSpec((1,) + shape, lambda e, *_: (e, 0, 0))
    tile_buf = pltpu.VMEM((FFN_SLOTS, FFN_TILE * ROW_SLABS, LANES), F32)
    return pl.pallas_call(
        _ffn_kernel,
        grid_spec=pltpu.PrefetchScalarGridSpec(
            num_scalar_prefetch=3,
            grid=(N_EXPERTS,),
            in_specs=[pl.BlockSpec(memory_space=pl.ANY),
                      wspec((D_MODEL, D_EXPERT)), wspec((D_MODEL, D_EXPERT)), wspec((D_EXPERT, D_MODEL))],
            out_specs=pl.BlockSpec(memory_space=pl.ANY),
            scratch_shapes=[tile_buf, tile_buf,
                            pltpu.VMEM((D_MODEL, D_EXPERT), BF16), pltpu.VMEM((D_MODEL, D_EXPERT), BF16),
                            pltpu.VMEM((D_EXPERT, D_MODEL), BF16),
                            pltpu.SMEM((FFN_SLOTS,), jnp.int32),
                            pltpu.SemaphoreType.DMA((FFN_SLOTS,)), pltpu.SemaphoreType.DMA((FFN_SLOTS,))]),
        out_shape=jax.ShapeDtypeStruct((n_rows * ROW_SLABS, LANES), F32),
        compiler_params=pltpu.CompilerParams(dimension_semantics=("arbitrary",),
                                             vmem_limit_bytes=VMEM_LIMIT_BYTES),
        name="ffn",
    )(first_tile, n_tile, n_used, xs, w_gate, w_up, w_down)


def _combine_kernel(p1_ref, p2_ref, ys_ref, meta_ref, x1_ref, gf_ref, out_ref, buf_ref, sem):
    tm = x1_ref.shape[0]

    def issue(t, carry):
        _row_copy(ys_ref, p1_ref[0, 0, t], buf_ref.at[0], t, sem).start(priority=0)
        _row_copy(ys_ref, p2_ref[0, 0, t], buf_ref.at[1], t, sem).start(priority=1)
        return carry

    lax.fori_loop(0, tm, issue, 0, unroll=ROW_DMA_UNROLL)

    def drain(t, carry):
        _row_copy(ys_ref, 0, buf_ref.at[0], 0, sem).wait()
        _row_copy(ys_ref, 0, buf_ref.at[1], 0, sem).wait()
        return carry

    lax.fori_loop(0, tm, drain, 0, unroll=ROW_DMA_UNROLL)

    meta = meta_ref[...]
    y = x1_ref[...] + meta[:, 4:5] * _load_slabs(buf_ref, tm, (0,)) + meta[:, 5:6] * _load_slabs(buf_ref, tm, (1,))
    out_ref[...] = y * lax.rsqrt(jnp.mean(y * y, axis=-1, keepdims=True) + EPS) * gf_ref[...]


def _combine(ys, pos1, pos2, meta, x1, norm_final_g):
    T = x1.shape[0]
    tm = COMBINE_TILE
    smem_row = pl.BlockSpec((1, 1, tm), lambda i: (i, 0, 0), memory_space=pltpu.SMEM)
    row = lambda n: pl.BlockSpec((tm, n), lambda i: (i, 0))
    return pl.pallas_call(
        _combine_kernel,
        grid=(T // tm,),
        in_specs=[smem_row, smem_row, pl.BlockSpec(memory_space=pl.ANY), row(META_COLS), row(D_MODEL),
                  pl.BlockSpec((1, D_MODEL), lambda i: (0, 0))],
        out_specs=row(D_MODEL),
        out_shape=jax.ShapeDtypeStruct((T, D_MODEL), F32),
        scratch_shapes=[pltpu.VMEM((2, tm * ROW_SLABS, LANES), F32), pltpu.SemaphoreType.DMA],
        compiler_params=pltpu.CompilerParams(dimension_semantics=("arbitrary",),
                                             vmem_limit_bytes=VMEM_LIMIT_BYTES),
        name="combine",
    )(pos1.reshape(T // tm, 1, tm), pos2.reshape(T // tm, 1, tm), ys, meta, x1,
      norm_final_g.reshape(1, D_MODEL))


def _run_layout(counts):
    cnt = counts.reshape(-1).astype(jnp.int32)
    padded = (cnt + FFN_TILE - 1) // FFN_TILE * FFN_TILE
    ends = jnp.cumsum(padded)
    base = ends - padded
    n_used = (ends[-1:] // FFN_TILE).astype(jnp.int32)
    pad_start = jnp.concatenate([base + cnt, n_used])
    return base, pad_start, base // FFN_TILE, padded // FFN_TILE, n_used


def kernel(x, norm_mix_g, w_in, conv_w, conv_b, i_bias, f_bias, mlstm_norm_g, s5_lambda_re, s5_lambda_im,
           s5_log_dt, s5_b_re, s5_b_im, s5_c_re, s5_c_im, s5_d, s5_glu_w, s5_glu_b, w_out, norm_ffn_g,
           router_group_w, router_group_b, router_expert_w, router_expert_b, expert_w_gate, expert_w_up,
           expert_w_down, norm_final_g):
    batch, seq_len, d_model = x.shape
    assert d_model == D_MODEL and norm_mix_g.shape[0] == 1
    assert seq_len % MLSTM_CHUNK == 0 and seq_len % INPROJ_TILE == 0 and seq_len % S5_BLOCK == 0
    T = batch * seq_len
    x2d = x.reshape(T, D_MODEL)

    q, k, v, og, u, gcol, grow = _inproj(x2d, norm_mix_g[0], w_in[0], conv_w[0], conv_b[0], i_bias[0],
                                         f_bias[0], seq_len)
    out_a = _mlstm(q, k, v, og, gcol, grow, mlstm_norm_g[0], batch, seq_len)
    tables = _s5_tables(s5_lambda_re[0], s5_lambda_im[0], s5_log_dt[0], s5_b_re[0], s5_b_im[0], s5_c_re[0],
                        s5_c_im[0], s5_d[0], s5_glu_w[0], s5_glu_b[0])
    out_b = _s5(u, tables, batch, seq_len)
    x1, meta, meta_t, counts = _outproj(out_a, out_b, x2d, w_out[0], norm_ffn_g[0], router_group_w[0],
                                            router_group_b[0], router_expert_w[0], router_expert_b[0])
    n_rows = 2 * T + N_EXPERTS * FFN_TILE
    base, pad_start, first_tile, n_tile, n_used = _run_layout(counts)
    pos = _plan(meta_t, base)
    xs = _dispatch(x1, norm_ffn_g[0], pos[0], pos[1], pad_start, n_rows)
    ys = _ffn(xs, first_tile, n_tile, n_used, n_rows, expert_w_gate[0], expert_w_up[0], expert_w_down[0])
    y = _combine(ys, pos[0], pos[1], meta, x1, norm_final_g)
    return y.reshape(batch, seq_len, D_MODEL)
```

```python
import functools

import jax
import jax.numpy as jnp
from jax import lax
from jax.experimental import pallas as pl
from jax.experimental.pallas import tpu as pltpu

F32 = jnp.float32
BF16 = jnp.bfloat16
HIGHEST = lax.Precision.HIGHEST

D_MODEL = 1024
D_MLSTM = 512
D_S5 = 512
HEADS = 4
HEAD_DIM = 128
CONV_WIDTH = 4
S5_GROUP_CH = 16
S5_GROUPS = 32
S5_STATE = 64
N_GROUPS = 4
PER_GROUP = 8
N_EXPERTS = 32
D_EXPERT = 512
EPS = 1e-6
LAMBDA_RE_MAX = -1e-4

VMEM_LIMIT_BYTES = 56 * 1024 * 1024
LANES = 128
SUBLANES = 8
ROW_SLABS = D_MODEL // LANES
assert ROW_SLABS == SUBLANES

INPROJ_TILE = 1024
MLSTM_CHUNK = 256
S5_BLOCK = 16
S5_PAIR = 2
S5_STEP_GROUPS = LANES // S5_GROUP_CH
S5_STEP_PAIRS = S5_STEP_GROUPS // S5_PAIR
S5_SCAN_ROWS = 16
OUTPROJ_TILE = 512
META_COLS = 8
PLAN_TILE = 2048
DISPATCH_TILE = 512
FFN_TILE = 256
COMBINE_TILE = 1024
ROW_DMA_UNROLL = 8
FFN_SLOTS = 3
FFN_LOOKAHEAD = FFN_SLOTS - 1


def _nt_dims():
    return (((1,), (1,)), ((), ()))


def _tn_dims():
    return (((0,), (0,)), ((), ()))


def _rmsnorm(x, g):
    return x * lax.rsqrt(jnp.mean(x * x, axis=-1, keepdims=True) + EPS) * g


def _log_sigmoid(x):
    return jnp.minimum(x, 0.0) - jnp.log1p(jnp.exp(-jnp.abs(x)))


def _split_weight(w):
    w = jnp.pad(w.astype(F32), ((0, 0), (0, LANES - w.shape[1])))
    hi = w.astype(BF16)
    lo = (w - hi.astype(F32)).astype(BF16)
    return jnp.concatenate([hi, lo], axis=1), hi


def _dot_split(x, x_hi, w_hilo_ref, w_hi_ref):
    x_lo = (x - x_hi.astype(F32)).astype(BF16)
    two = jnp.dot(x_hi, w_hilo_ref[...], preferred_element_type=F32)
    return two[:, :LANES] + two[:, LANES:] + jnp.dot(x_lo, w_hi_ref[...], preferred_element_type=F32)


def _store_slabs(ref, value, lead=()):
    m = value.shape[0]
    for s in range(ROW_SLABS):
        ref[lead + (pl.ds(s, m, stride=ROW_SLABS), slice(None))] = value[:, s * LANES:(s + 1) * LANES]


def _load_slabs(ref, m, lead=()):
    return jnp.concatenate([ref[lead + (pl.ds(s, m, stride=ROW_SLABS), slice(None))] for s in range(ROW_SLABS)],
                           axis=1)


def _transpose_small(a):
    n = a.shape[1]
    eye = (lax.broadcasted_iota(jnp.int32, (n, n), 0) == lax.broadcasted_iota(jnp.int32, (n, n), 1)).astype(F32)
    return lax.dot_general(eye, a, _nt_dims(), precision=HIGHEST, preferred_element_type=F32)


def _inproj_kernel(tiles_per_seq, x_ref, g_ref, wqk_ref, wv_ref, wo_ref, wu_ref, wg2_ref, wg1_ref,
                   gb_ref, cw_ref, cb_ref,
                   q_ref, k_ref, v_ref, og_ref, u_ref, gcol_ref, grow_ref, ext_ref, us_ref):
    i = pl.program_id(0)
    tm = x_ref.shape[0]
    x = x_ref[...]
    h = x * lax.rsqrt(jnp.mean(x * x, axis=-1, keepdims=True) + EPS) * g_ref[...]
    hb = h.astype(BF16)

    qk_raw = jnp.dot(hb, wqk_ref[...], preferred_element_type=F32)

    @pl.when(i % tiles_per_seq == 0)
    def _():
        ext_ref[0:8, :] = jnp.zeros((8, ext_ref.shape[1]), F32)

    ext_ref[8:, :] = qk_raw
    cw = cw_ref[...]
    acc = cb_ref[...] + cw[CONV_WIDTH - 1:CONV_WIDTH, :] * qk_raw
    for s in range(1, CONV_WIDTH):
        acc = acc + cw[CONV_WIDTH - 1 - s:CONV_WIDTH - s, :] * ext_ref[pl.ds(8 - s, tm), :]
    ext_ref[0:8, :] = qk_raw[tm - 8:, :]
    qk = acc * jax.nn.sigmoid(acc)
    q_ref[...] = qk[:, :D_MLSTM].astype(BF16)
    k_ref[...] = (qk[:, D_MLSTM:] * (HEAD_DIM ** -0.5)).astype(BF16)

    v_ref[...] = jnp.dot(hb, wv_ref[...], preferred_element_type=F32).astype(BF16)
    o_pre = jnp.dot(hb, wo_ref[...], preferred_element_type=F32)
    og_ref[...] = jax.nn.sigmoid(o_pre).astype(BF16)

    u = jnp.dot(hb, wu_ref[...], preferred_element_type=F32)
    n_slabs = us_ref.shape[0]
    for j in range(n_slabs):
        us_ref[j] = u[:, j * LANES:(j + 1) * LANES]
    for p in range(S5_BLOCK):
        u_ref[p] = jnp.concatenate([us_ref[j, pl.ds(p, tm // S5_BLOCK, stride=S5_BLOCK), :]
                                    for j in range(n_slabs)], axis=1).astype(BF16)

    gcol = _dot_split(h, hb, wg2_ref, wg1_ref)[:, :2 * HEADS] + gb_ref[...]
    gcol_ref[...] = gcol
    grow_ref[...] = _transpose_small(gcol)


def _inproj(x2d, norm_g, w_in, conv_w, conv_b, i_bias, f_bias, seq_len):
    T = x2d.shape[0]
    tm = INPROJ_TILE
    dm = D_MLSTM
    wqk = w_in[:, :2 * dm].astype(BF16)
    wv = w_in[:, 2 * dm:3 * dm].astype(BF16)
    wo = w_in[:, 3 * dm:4 * dm].astype(BF16)
    wg2, wg1 = _split_weight(w_in[:, 4 * dm:4 * dm + 2 * HEADS])
    wu = w_in[:, 4 * dm + 2 * HEADS:].astype(BF16)
    gb = jnp.concatenate([i_bias, f_bias]).astype(F32)
    full = lambda shape: pl.BlockSpec(shape, lambda i: (0,) * len(shape))
    row = lambda n: pl.BlockSpec((tm, n), lambda i: (i, 0))
    return pl.pallas_call(
        functools.partial(_inproj_kernel, seq_len // tm),
        grid=(T // tm,),
        in_specs=[row(D_MODEL), full((1, D_MODEL)), full((D_MODEL, 2 * dm)), full((D_MODEL, dm)),
                  full((D_MODEL, dm)), full((D_MODEL, D_S5)), full((D_MODEL, 2 * LANES)),
                  full((D_MODEL, LANES)), full((1, 2 * HEADS)),
                  full((CONV_WIDTH, 2 * dm)), full((1, 2 * dm))],
        out_specs=[row(dm), row(dm), row(dm), row(dm),
                   pl.BlockSpec((S5_BLOCK, tm // S5_BLOCK, D_S5), lambda i: (0, i, 0)), row(2 * HEADS),
                   pl.BlockSpec((2 * HEADS, tm), lambda i: (0, i))],
        out_shape=[jax.ShapeDtypeStruct((T, dm), BF16)] * 4
        + [jax.ShapeDtypeStruct((S5_BLOCK, T // S5_BLOCK, D_S5), BF16),
           jax.ShapeDtypeStruct((T, 2 * HEADS), F32),
           jax.ShapeDtypeStruct((2 * HEADS, T), F32)],
        scratch_shapes=[pltpu.VMEM((tm + 8, 2 * dm), F32), pltpu.VMEM((D_S5 // LANES, tm, LANES), F32)],
        compiler_params=pltpu.CompilerParams(dimension_semantics=("arbitrary",),
                                             vmem_limit_bytes=VMEM_LIMIT_BYTES),
        name="inproj",
    )(x2d, norm_g.reshape(1, D_MODEL), wqk, wv, wo, wu, wg2, wg1, gb.reshape(1, -1),
      conv_w, conv_b.reshape(1, -1))


def _mlstm_kernel(batch, q_ref, k_ref, v_ref, og_ref, gcol_ref, ng_ref, *refs):
    grow_refs = refs[:batch]
    out_ref, c_ref, m_ref = refs[batch:]
    tc = q_ref.shape[1]
    d = HEAD_DIM

    @pl.when(pl.program_id(0) == 0)
    def _():
        c_ref[...] = jnp.zeros(c_ref.shape, F32)
        m_ref[...] = jnp.zeros(m_ref.shape, F32)

    causal = lax.broadcasted_iota(jnp.int32, (tc, tc), 0) >= lax.broadcasted_iota(jnp.int32, (tc, tc), 1)
    ones_col = (lax.broadcasted_iota(jnp.int32, (tc, d), 1) == 0).astype(BF16)
    for b in range(batch):
        gcol = gcol_ref[b]
        grow = grow_refs[b][...]
        b_col_all = jnp.dot(causal.astype(F32), _log_sigmoid(gcol), precision=HIGHEST,
                            preferred_element_type=F32)
        b_row_all = _transpose_small(b_col_all)
        for h in range(HEADS):
            _mlstm_head(b, h, causal, ones_col, gcol, grow, b_col_all, b_row_all,
                        q_ref, k_ref, v_ref, og_ref, ng_ref, out_ref, c_ref, m_ref)


def _mlstm_head(b, h, causal, ones_col, gcol, grow, b_col_all, b_row_all,
                q_ref, k_ref, v_ref, og_ref, ng_ref, out_ref, c_ref, m_ref):
    tc, d = q_ref.shape[1], HEAD_DIM
    hs = slice(h * d, (h + 1) * d)
    st = b * HEADS + h
    li_col = gcol[:, h:h + 1]
    li_row = grow[h:h + 1, :]
    b_col = b_col_all[:, HEADS + h:HEADS + h + 1]
    b_row = b_row_all[HEADS + h:HEADS + h + 1, :]
    m_prev = m_ref[st:st + 1, 0:1]

    dmat = jnp.where(causal, b_col - b_row + li_row, -jnp.inf)
    inter = b_col + m_prev
    m_row = jnp.maximum(inter, jnp.max(dmat, axis=1, keepdims=True))
    w_intra = jnp.exp(dmat - m_row)
    w_inter = jnp.exp(inter - m_row)

    qh = q_ref[b, :, hs]
    kh = k_ref[b, :, hs]
    v_aug = jnp.concatenate([v_ref[b, :, hs], ones_col], axis=1)
    s = lax.dot_general(qh, kh, _nt_dims(), preferred_element_type=F32) * w_intra
    c_aug = c_ref[st]
    tot = (jnp.dot(s.astype(BF16), v_aug, preferred_element_type=F32)
           + w_inter * lax.dot_general(qh, c_aug.astype(BF16), _nt_dims(), preferred_element_type=F32))
    num = tot[:, :d]
    den = tot[:, d:d + 1]
    hh = num / jnp.maximum(jnp.abs(den), jnp.exp(-m_row))
    hh = hh * lax.rsqrt(jnp.mean(hh * hh, axis=-1, keepdims=True) + EPS)
    out_ref[b, :, hs] = (og_ref[b, :, hs].astype(F32) * (hh * ng_ref[:, hs])).astype(BF16)

    b_last = b_col[tc - 1:tc, :]
    d_state = b_last - b_col + li_col
    m_new = jnp.maximum(b_last + m_prev, jnp.max(d_state, axis=0, keepdims=True))
    w_s = jnp.exp(d_state - m_new)
    w_c = jnp.exp(b_last + m_prev - m_new)
    upd = lax.dot_general((v_aug.astype(F32) * w_s).astype(BF16), kh, _tn_dims(),
                          preferred_element_type=F32)
    c_ref[st] = w_c * c_aug + upd
    m_ref[st:st + 1, :] = jnp.broadcast_to(m_new, (1, m_ref.shape[1]))


def _mlstm(q, k, v, og, gcol, grow, norm_g, batch, seq_len):
    tc = MLSTM_CHUNK
    nc = seq_len // tc
    T = q.shape[0]
    n_state = batch * HEADS
    seq = lambda a: a.reshape(batch, seq_len, a.shape[1])
    rows = lambda n: pl.BlockSpec((batch, tc, n), lambda c: (0, c, 0))
    grow_specs = [pl.BlockSpec((2 * HEADS, tc), lambda c, b=b: (0, b * nc + c)) for b in range(batch)]
    out = pl.pallas_call(
        functools.partial(_mlstm_kernel, batch),
        grid=(nc,),
        in_specs=[rows(D_MLSTM), rows(D_MLSTM), rows(D_MLSTM), rows(D_MLSTM), rows(2 * HEADS),
                  pl.BlockSpec((1, D_MLSTM), lambda c: (0, 0))] + grow_specs,
        out_specs=rows(D_MLSTM),
        out_shape=jax.ShapeDtypeStruct((batch, seq_len, D_MLSTM), BF16),
        scratch_shapes=[pltpu.VMEM((n_state, 2 * HEAD_DIM, HEAD_DIM), F32),
                        pltpu.VMEM((-(-n_state // SUBLANES) * SUBLANES, LANES), F32)],
        compiler_params=pltpu.CompilerParams(dimension_semantics=("arbitrary",),
                                             vmem_limit_bytes=VMEM_LIMIT_BYTES),
        name="mlstm",
    )(seq(q), seq(k), seq(v), seq(og), seq(gcol), norm_g.reshape(1, D_MLSTM), *([grow] * batch))
    return out.reshape(T, D_MLSTM)


def _cplx_pow(exponent, theta_re, theta_im):
    mag = jnp.exp(exponent * theta_re)
    return mag * jnp.cos(exponent * theta_im), mag * jnp.sin(exponent * theta_im)


def _s5_tables_kernel(lrr_ref, lir_ref, dtr_ref, lrc_ref, lic_ref, dtc_ref, btr_ref, bti_ref, ctr_ref, cti_ref,
                      dcol_ref, gw1_ref, gw2_ref, gbcol_ref,
                      we_ref, kt_ref, wc_ref, glu_ref, glub_ref, dsk_ref, sr_ref, si_ref):
    P, C, N = S5_BLOCK, S5_GROUP_CH, S5_STATE
    PC, NN = P * C, S5_PAIR * S5_STATE
    shift_c, shift_n = C.bit_length() - 1, N.bit_length() - 1

    lr = jnp.minimum(lrr_ref[0], LAMBDA_RE_MAX)
    li = lir_ref[0]
    dt = jnp.exp(dtr_ref[0])
    th_r, th_i = lr * dt, li * dt
    ar, ai = _cplx_pow(1.0, th_r, th_i)
    den = lr * lr + li * li
    cr = ((ar - 1.0) * lr + ai * li) / den
    ci = (ai * lr - (ar - 1.0) * li) / den
    btr, bti = btr_ref[0], bti_ref[0]
    bbr = cr * btr - ci * bti
    bbi = cr * bti + ci * btr

    back = (P - 1 - lax.broadcasted_iota(jnp.int32, (P, 1), 0)).astype(F32)
    pwr, pwi = _cplx_pow(back, th_r, th_i)
    val_r = jnp.concatenate([pwr[p:p + 1] * bbr - pwi[p:p + 1] * bbi for p in range(P)], axis=0)
    val_i = jnp.concatenate([pwr[p:p + 1] * bbi + pwi[p:p + 1] * bbr for p in range(P)], axis=0)
    lane_grp = lax.broadcasted_iota(jnp.int32, (PC, NN), 1) >> shift_n
    we_r = jnp.concatenate([jnp.where(lane_grp == g, val_r, 0.0) for g in range(S5_PAIR)], axis=0)
    we_i = jnp.concatenate([jnp.where(lane_grp == g, val_i, 0.0) for g in range(S5_PAIR)], axis=0)
    we_ref[0] = jnp.concatenate([we_r, we_i], axis=1).astype(BF16)

    steps = (P << lax.broadcasted_iota(jnp.int32, (S5_SCAN_ROWS, 1), 0)).astype(F32)
    sr_ref[0], si_ref[0] = _cplx_pow(steps, th_r, th_i)

    lrc = jnp.minimum(lrc_ref[0], LAMBDA_RE_MAX)
    dtc = jnp.exp(dtc_ref[0])
    thc_r, thc_i = lrc * dtc, lic_ref[0] * dtc
    lane = lax.broadcasted_iota(jnp.int32, (NN, PC), 1)
    p0r, p0i = _cplx_pow((lane >> shift_c).astype(F32), thc_r, thc_i)
    acr, aci = _cplx_pow(1.0, thc_r, thc_i)
    p1r, p1i = p0r * acr - p0i * aci, p0r * aci + p0i * acr
    tile_c = ((lax.broadcasted_iota(jnp.int32, (C, PC), 1) & (C - 1))
              == lax.broadcasted_iota(jnp.int32, (C, PC), 0)).astype(F32)
    ctr = jnp.dot(ctr_ref[0], tile_c, precision=HIGHEST, preferred_element_type=F32)
    cti = jnp.dot(cti_ref[0], tile_c, precision=HIGHEST, preferred_element_type=F32)
    w0r, w0i = ctr * p0r - cti * p0i, ctr * p0i + cti * p0r
    vr, vi = ctr * p1r - cti * p1i, -(ctr * p1i + cti * p1r)
    row_grp = lax.broadcasted_iota(jnp.int32, (NN, PC), 0) >> shift_n
    wc_r = jnp.concatenate([jnp.where(row_grp == g, vr, 0.0) for g in range(S5_PAIR)], axis=1)
    wc_i = jnp.concatenate([jnp.where(row_grp == g, vi, 0.0) for g in range(S5_PAIR)], axis=1)
    wc_ref[0] = jnp.concatenate([wc_r, wc_i], axis=0).astype(BF16)

    b_grp = lax.broadcasted_iota(jnp.int32, (C, NN), 1) >> shift_n
    lane_k = lax.broadcasted_iota(jnp.int32, (C, PC), 1)
    same_pos = ((lax.broadcasted_iota(jnp.int32, (PC, PC), 0) >> shift_c)
                == (lax.broadcasted_iota(jnp.int32, (PC, PC), 1) >> shift_c))
    zero_blk = jnp.zeros((PC, PC), F32)

    def on_diagonal(blk, g):
        return jnp.concatenate([blk if j == g else zero_blk for j in range(S5_PAIR)], axis=1)

    kt_rows, glu_v, glu_g, bias_v, bias_g, skip = [], [], [], [], [], []
    for g in range(S5_PAIR):
        k0 = (jnp.dot(jnp.where(b_grp == g, bbr, 0.0), w0r, precision=HIGHEST, preferred_element_type=F32)
              - jnp.dot(jnp.where(b_grp == g, bbi, 0.0), w0i, precision=HIGHEST, preferred_element_type=F32))
        toep = jnp.concatenate(
            [k0 if p == 0 else jnp.where(lane_k >= C * p, pltpu.roll(k0, C * p, axis=1), 0.0) for p in range(P)],
            axis=0)
        kt_rows.append(on_diagonal(toep, g))
        for w_ref, dst in ((gw1_ref, glu_v), (gw2_ref, glu_g)):
            t = jnp.dot(w_ref[0, g], tile_c, precision=HIGHEST, preferred_element_type=F32)
            dst.append(on_diagonal(jnp.where(same_pos, jnp.concatenate([t] * P, axis=0), 0.0), g))
        gb = gbcol_ref[0, g]
        bias_v.append(jnp.sum(gb[:C] * tile_c, axis=0, keepdims=True))
        bias_g.append(jnp.sum(gb[C:] * tile_c, axis=0, keepdims=True))
        skip.append(jnp.sum(dcol_ref[0, g] * tile_c, axis=0, keepdims=True))
    kt_ref[0] = jnp.concatenate(kt_rows, axis=0).astype(BF16)
    glu_ref[0] = jnp.concatenate([jnp.concatenate(glu_v, axis=0), jnp.concatenate(glu_g, axis=0)],
                                 axis=1).astype(BF16)
    glub_ref[0] = jnp.concatenate(bias_v + bias_g, axis=1)
    dsk_ref[0] = jnp.concatenate(skip, axis=1)


def _s5_tables(lam_re, lam_im, log_dt, b_re, b_im, c_re, c_im, d_skip, glu_w, glu_b):
    G, N, C, P = S5_GROUPS, S5_STATE, S5_GROUP_CH, S5_BLOCK
    J, NN, PC = G // S5_PAIR, S5_PAIR * S5_STATE, S5_BLOCK * S5_GROUP_CH
    f = lambda a: a.astype(F32)
    dt_full = jnp.broadcast_to(f(log_dt)[:, None], (G, N))
    rows = [a.reshape(J, 1, NN) for a in (f(lam_re), f(lam_im), dt_full)]
    cols = [a.reshape(J, NN, 1) for a in (f(lam_re), f(lam_im), dt_full)]
    bt = [f(a).reshape(J, S5_PAIR, N, C).transpose(0, 3, 1, 2).reshape(J, C, NN) for a in (b_re, b_im)]
    ct = [f(a).transpose(0, 2, 1).reshape(J, NN, C) for a in (c_re, c_im)]
    dcol = f(d_skip).reshape(J, S5_PAIR, C, 1)
    gw1 = f(glu_w)[..., :C].reshape(J, S5_PAIR, C, C)
    gw2 = f(glu_w)[..., C:].reshape(J, S5_PAIR, C, C)
    gbcol = f(glu_b).reshape(J, S5_PAIR, 2 * C, 1)
    ins = rows + cols + bt + ct + [dcol, gw1, gw2, gbcol]
    per_pair = lambda shape: pl.BlockSpec((1,) + shape, lambda j: (j,) + (0,) * len(shape))
    out_shapes = [((S5_PAIR * PC, 2 * NN), BF16), ((S5_PAIR * PC, S5_PAIR * PC), BF16),
                  ((2 * NN, S5_PAIR * PC), BF16), ((S5_PAIR * PC, 2 * S5_PAIR * PC), BF16),
                  ((1, 2 * S5_PAIR * PC), F32), ((1, S5_PAIR * PC), F32),
                  ((S5_SCAN_ROWS, NN), F32), ((S5_SCAN_ROWS, NN), F32)]
    return pl.pallas_call(
        _s5_tables_kernel,
        grid=(J,),
        in_specs=[per_pair(a.shape[1:]) for a in ins],
        out_specs=[per_pair(s) for s, _ in out_shapes],
        out_shape=[jax.ShapeDtypeStruct((J,) + s, d) for s, d in out_shapes],
        compiler_params=pltpu.CompilerParams(dimension_semantics=("parallel",),
                                             vmem_limit_bytes=VMEM_LIMIT_BYTES),
        name="s5_tables",
    )(*ins)


def _s5_kernel(n_steps, u_ref, we_ref, kt_ref, wc_ref, glu_ref, glub_ref, dsk_ref, sr_ref, si_ref, out_ref):
    P, C = S5_BLOCK, S5_GROUP_CH
    nb, slab = u_ref.shape[1], u_ref.shape[2]
    wide = slab * P
    pair_w = S5_PAIR * P * C
    half = S5_PAIR * S5_STATE
    shift_c = C.bit_length() - 1

    lane = lax.broadcasted_iota(jnp.int32, (slab, wide), 1)
    src = lax.broadcasted_iota(jnp.int32, (slab, wide), 0)
    target0 = ((src >> shift_c) << (shift_c + P.bit_length() - 1)) + (src & (C - 1))

    def place(p0):
        return jnp.concatenate([jnp.where(lane == target0 + p * C, 1.0, 0.0).astype(BF16)
                                for p in (p0, p0 + 1)], axis=0)

    u_blk = jnp.zeros((nb, wide), F32)
    for p0 in range(0, P, 2):
        lhs = jnp.concatenate([u_ref[p0], u_ref[p0 + 1]], axis=1)
        u_blk = u_blk + jnp.dot(lhs, place(p0), preferred_element_type=F32)
    u_blk = u_blk.astype(BF16)

    row = lax.broadcasted_iota(jnp.int32, (nb, half), 0)
    outs = []
    for q in range(S5_STEP_PAIRS):
        u = u_blk[:, q * pair_w:(q + 1) * pair_w]
        e = jnp.dot(u, we_ref[q], preferred_element_type=F32)
        xr, xi = e[:, :half], e[:, half:]
        for k in range(n_steps):
            sh = 1 << k
            ar = sr_ref[q, k:k + 1, :]
            ai = si_ref[q, k:k + 1, :]
            valid = row >= sh
            yr = jnp.where(valid, pltpu.roll(xr, sh, axis=0), 0.0)
            yi = jnp.where(valid, pltpu.roll(xi, sh, axis=0), 0.0)
            xr, xi = xr + ar * yr - ai * yi, xi + ar * yi + ai * yr
        first = row >= 1
        pr = jnp.where(first, pltpu.roll(xr, 1, axis=0), 0.0)
        pi = jnp.where(first, pltpu.roll(xi, 1, axis=0), 0.0)
        sprev = jnp.concatenate([pr, pi], axis=1).astype(BF16)
        y = (jnp.dot(u, kt_ref[q], preferred_element_type=F32)
             + jnp.dot(sprev, wc_ref[q], preferred_element_type=F32)
             + dsk_ref[q] * u.astype(F32))
        y = jax.nn.gelu(y)
        z = jnp.dot(y.astype(BF16), glu_ref[q], preferred_element_type=F32) + glub_ref[q]
        outs.append(z[:, :pair_w] * jax.nn.sigmoid(z[:, pair_w:]))
    out_blk = jnp.concatenate(outs, axis=1).astype(BF16)
    for p0 in range(0, P, 2):
        o = lax.dot_general(out_blk, place(p0), _nt_dims(), preferred_element_type=F32)
        out_ref[p0] = o[:, :slab].astype(BF16)
        out_ref[p0 + 1] = o[:, slab:].astype(BF16)


def _s5(u, tables, batch, seq_len):
    nb = seq_len // S5_BLOCK
    n_steps = max(1, (nb - 1).bit_length())
    assert n_steps <= S5_SCAN_ROWS
    slab = pl.BlockSpec((S5_BLOCK, nb, LANES), lambda b, o: (0, b, o))
    per_step = lambda a: pl.BlockSpec((S5_STEP_PAIRS,) + a.shape[1:], lambda b, o: (o, 0, 0))
    return pl.pallas_call(
        functools.partial(_s5_kernel, n_steps),
        grid=(batch, S5_GROUPS // S5_STEP_GROUPS),
        in_specs=[slab] + [per_step(a) for a in tables],
        out_specs=slab,
        out_shape=jax.ShapeDtypeStruct(u.shape, BF16),
        compiler_params=pltpu.CompilerParams(dimension_semantics=("parallel", "parallel"),
                                             vmem_limit_bytes=VMEM_LIMIT_BYTES),
        name="s5",
    )(u, *tables)


def _outproj_kernel(a_ref, b_ref, x_ref, wa_ref, wb_ref, g_ref, rw2_ref, rw1_ref, rgb_ref, reb_ref,
                    x1_ref, meta_ref, metat_ref, cnt_ref, bs_ref):
    tm = x_ref.shape[0]

    @pl.when(pl.program_id(0) == 0)
    def _():
        cnt_ref[...] = jnp.zeros(cnt_ref.shape, F32)

    n_slabs = bs_ref.shape[0]
    for p in range(S5_BLOCK):
        bp = b_ref[p].astype(F32)
        for j in range(n_slabs):
            bs_ref[j, pl.ds(p, tm // S5_BLOCK, stride=S5_BLOCK), :] = bp[:, j * LANES:(j + 1) * LANES]
    out_b = jnp.concatenate([bs_ref[j] for j in range(n_slabs)], axis=1).astype(BF16)
    x1 = (x_ref[...]
          + jnp.dot(a_ref[...], wa_ref[...], preferred_element_type=F32)
          + jnp.dot(out_b, wb_ref[...], preferred_element_type=F32))
    x1_ref[...] = x1
    h2 = _rmsnorm(x1, g_ref[...])

    logits = _dot_split(h2, h2.astype(BF16), rw2_ref, rw1_ref)
    el = logits[:, :N_EXPERTS] + reb_ref[...]
    gl = logits[:, N_EXPERTS:N_EXPERTS + N_GROUPS] + rgb_ref[...]
    gmax = jnp.max(gl, axis=1, keepdims=True)
    g_w = 1.0 / jnp.sum(jnp.exp(gl - gmax), axis=1, keepdims=True)
    lane_g = lax.broadcasted_iota(jnp.int32, (tm, N_GROUPS), 1)
    g_idx = jnp.min(jnp.where(gl == gmax, lane_g, N_GROUPS), axis=1, keepdims=True)
    lane_e = lax.broadcasted_iota(jnp.int32, (tm, N_EXPERTS), 1)
    elm = jnp.where(lane_e // PER_GROUP == g_idx, el, -jnp.inf)
    m1 = jnp.max(elm, axis=1, keepdims=True)
    i1 = jnp.min(jnp.where(elm == m1, lane_e, N_EXPERTS), axis=1, keepdims=True)
    elm2 = jnp.where(lane_e == i1, -jnp.inf, elm)
    m2 = jnp.max(elm2, axis=1, keepdims=True)
    i2 = jnp.min(jnp.where(elm2 == m2, lane_e, N_EXPERTS), axis=1, keepdims=True)
    e2 = jnp.exp(m2 - m1)
    w1 = g_w / (1.0 + e2)
    w2 = g_w * e2 / (1.0 + e2)

    sel1 = lane_e == i1
    sel2 = lane_e == i2
    picked = (sel1 | sel2).astype(BF16)
    rix = lax.broadcasted_iota(jnp.int32, (tm, tm), 0)
    cix = lax.broadcasted_iota(jnp.int32, (tm, tm), 1)
    before = jnp.dot((rix > cix).astype(BF16), picked, preferred_element_type=F32) + cnt_ref[...]
    r1 = jnp.sum(jnp.where(sel1, before, 0.0), axis=1, keepdims=True)
    r2 = jnp.sum(jnp.where(sel2, before, 0.0), axis=1, keepdims=True)
    cnt_ref[...] += jnp.sum(picked.astype(F32), axis=0, keepdims=True)
    col = lax.broadcasted_iota(jnp.int32, (tm, META_COLS), 1)
    meta = jnp.zeros((tm, META_COLS), F32)
    for c, val in enumerate((i1.astype(F32), i2.astype(F32), r1, r2, w1, w2)):
        meta = jnp.where(col == c, val, meta)
    meta_ref[...] = meta
    metat_ref[...] = _transpose_small(meta)


def _outproj(out_a, out_b, x2d, w_out, norm_g, rg_w, rg_b, re_w, re_b):
    T = x2d.shape[0]
    tm = OUTPROJ_TILE
    wa = w_out[:D_MLSTM].astype(BF16)
    wb = w_out[D_MLSTM:].astype(BF16)
    rw2, rw1 = _split_weight(jnp.concatenate([re_w, rg_w], axis=1))
    full = lambda shape: pl.BlockSpec(shape, lambda i: (0,) * len(shape))
    row = lambda n: pl.BlockSpec((tm, n), lambda i: (i, 0))
    return pl.pallas_call(
        _outproj_kernel,
        grid=(T // tm,),
        in_specs=[row(D_MLSTM), pl.BlockSpec((S5_BLOCK, tm // S5_BLOCK, D_S5), lambda i: (0, i, 0)), row(D_MODEL),
                  full((D_MLSTM, D_MODEL)), full((D_S5, D_MODEL)),
                  full((1, D_MODEL)), full((D_MODEL, 2 * LANES)), full((D_MODEL, LANES)), full((1, N_GROUPS)),
                  full((1, N_EXPERTS))],
        out_specs=[row(D_MODEL), row(META_COLS), pl.BlockSpec((META_COLS, tm), lambda i: (0, i)),
                   full((1, N_EXPERTS))],
        out_shape=[jax.ShapeDtypeStruct((T, D_MODEL), F32),
                   jax.ShapeDtypeStruct((T, META_COLS), F32), jax.ShapeDtypeStruct((META_COLS, T), F32),
                   jax.ShapeDtypeStruct((1, N_EXPERTS), F32)],
        scratch_shapes=[pltpu.VMEM((D_S5 // LANES, tm, LANES), F32)],
        compiler_params=pltpu.CompilerParams(dimension_semantics=("arbitrary",),
                                             vmem_limit_bytes=VMEM_LIMIT_BYTES),
        name="outproj",
    )(out_a, out_b, x2d, wa, wb, norm_g.reshape(1, D_MODEL), rw2, rw1, rg_b.reshape(1, -1), re_b.reshape(1, -1))


def _plan_kernel(metat_ref, base_ref, pos_ref):
    mt = metat_ref[...]
    tt = mt.shape[1]
    expert = lax.broadcasted_iota(jnp.int32, (N_EXPERTS, tt), 0)
    base = base_ref[...]
    row = lax.broadcasted_iota(jnp.int32, (META_COLS, tt), 0)
    pos = jnp.zeros((META_COLS, tt), F32)
    for k in range(2):
        first = jnp.sum(jnp.where(expert == mt[k:k + 1, :].astype(jnp.int32), base, 0.0), axis=0, keepdims=True)
        pos = jnp.where(row == k, first + mt[2 + k:3 + k, :], pos)
    pos_ref[...] = pos.astype(jnp.int32)


def _plan(meta_t, base):
    T = meta_t.shape[1]
    tt = PLAN_TILE
    return pl.pallas_call(
        _plan_kernel,
        grid=(T // tt,),
        in_specs=[pl.BlockSpec((META_COLS, tt), lambda i: (0, i)), pl.BlockSpec((N_EXPERTS, 1), lambda i: (0, 0))],
        out_specs=pl.BlockSpec((META_COLS, tt), lambda i: (0, i)),
        out_shape=jax.ShapeDtypeStruct((META_COLS, T), jnp.int32),
        compiler_params=pltpu.CompilerParams(dimension_semantics=("parallel",),
                                             vmem_limit_bytes=VMEM_LIMIT_BYTES),
        name="plan",
    )(meta_t, base.astype(F32).reshape(N_EXPERTS, 1))


def _slab_rows(ref, row, n_rows):
    return ref.at[pl.ds(pl.multiple_of(row * ROW_SLABS, ROW_SLABS), n_rows * ROW_SLABS), :]


def _row_copy(src_ref, src_row, dst_ref, dst_row, sem):
    return pltpu.make_async_copy(_slab_rows(src_ref, src_row, 1), _slab_rows(dst_ref, dst_row, 1), sem)


def _pad_copy(zero_ref, xs_ref, row, sem):
    return pltpu.make_async_copy(zero_ref, _slab_rows(xs_ref, row, FFN_TILE), sem)


def _dispatch_kernel(pad_ref, p1_ref, p2_ref, x1_ref, g_ref, xs_ref, h_ref, zero_ref, sem, zsem):
    tm = x1_ref.shape[0]
    n_tiles = xs_ref.shape[0] // (FFN_TILE * ROW_SLABS)
    i = pl.program_id(0)

    @pl.when(i == 0)
    def _():
        zero_ref[...] = jnp.zeros(zero_ref.shape, F32)
        for e in range(N_EXPERTS):
            _pad_copy(zero_ref, xs_ref, pad_ref[e], zsem).start()
        for e in range(N_EXPERTS):
            _pad_copy(zero_ref, xs_ref, 0, zsem).wait()
        first_free = pad_ref[N_EXPERTS]

        def start_tail(k, carry):
            _pad_copy(zero_ref, xs_ref, k * FFN_TILE, zsem).start()
            return carry

        def wait_tail(k, carry):
            _pad_copy(zero_ref, xs_ref, 0, zsem).wait()
            return carry

        lax.fori_loop(first_free, n_tiles, start_tail, 0)
        lax.fori_loop(first_free, n_tiles, wait_tail, 0)

    def drain(half):
        def wait_two(t, carry):
            _row_copy(h_ref.at[half], 0, xs_ref, 0, sem.at[half]).wait()
            _row_copy(h_ref.at[half], 0, xs_ref, 0, sem.at[half]).wait()
            return carry

        lax.fori_loop(0, tm, wait_two, 0, unroll=ROW_DMA_UNROLL)

    for half in (0, 1):
        @pl.when(i % 2 == half)
        def _():
            @pl.when(i >= 2)
            def _():
                drain(half)

            _store_slabs(h_ref, _rmsnorm(x1_ref[...], g_ref[...]), (half,))

            def issue(t, carry):
                _row_copy(h_ref.at[half], t, xs_ref, p1_ref[0, 0, t], sem.at[half]).start(priority=0)
                _row_copy(h_ref.at[half], t, xs_ref, p2_ref[0, 0, t], sem.at[half]).start(priority=1)
                return carry

            lax.fori_loop(0, tm, issue, 0, unroll=ROW_DMA_UNROLL)

            @pl.when(i == pl.num_programs(0) - 1)
            def _():
                drain(half)

                @pl.when(i >= 1)
                def _():
                    drain(1 - half)


def _dispatch(x1, norm_g, pos1, pos2, pad_start, n_rows):
    T = x1.shape[0]
    tm = DISPATCH_TILE
    smem_row = pl.BlockSpec((1, 1, tm), lambda i, pad: (i, 0, 0), memory_space=pltpu.SMEM)
    return pl.pallas_call(
        _dispatch_kernel,
        grid_spec=pltpu.PrefetchScalarGridSpec(
            num_scalar_prefetch=1,
            grid=(T // tm,),
            in_specs=[smem_row, smem_row, pl.BlockSpec((tm, D_MODEL), lambda i, pad: (i, 0)),
                      pl.BlockSpec((1, D_MODEL), lambda i, pad: (0, 0))],
            out_specs=pl.BlockSpec(memory_space=pl.ANY),
            scratch_shapes=[pltpu.VMEM((2, tm * ROW_SLABS, LANES), F32),
                            pltpu.VMEM((FFN_TILE * ROW_SLABS, LANES), F32), pltpu.SemaphoreType.DMA((2,)),
                            pltpu.SemaphoreType.DMA]),
        out_shape=jax.ShapeDtypeStruct(((n_rows + FFN_TILE) * ROW_SLABS, LANES), F32),
        compiler_params=pltpu.CompilerParams(dimension_semantics=("arbitrary",),
                                             vmem_limit_bytes=VMEM_LIMIT_BYTES),
        name="dispatch",
    )(pad_start, pos1.reshape(T // tm, 1, tm), pos2.reshape(T // tm, 1, tm), x1, norm_g.reshape(1, D_MODEL))


def _tile_in(xs_ref, tile, xbuf_ref, slot, sem_ref):
    return pltpu.make_async_copy(_slab_rows(xs_ref, tile * FFN_TILE, FFN_TILE), xbuf_ref.at[slot], sem_ref.at[slot])


def _tile_out(ybuf_ref, slot, ys_ref, tile, sem_ref):
    return pltpu.make_async_copy(ybuf_ref.at[slot], _slab_rows(ys_ref, tile * FFN_TILE, FFN_TILE), sem_ref.at[slot])


def _ffn_kernel(first_ref, ntile_ref, nused_ref, xs_ref, wg_ref, wu_ref, wd_ref, ys_ref,
                xbuf_ref, ybuf_ref, wgb_ref, wub_ref, wdb_ref, busy_ref, isem, osem):
    e = pl.program_id(0)
    t0 = first_ref[e]
    n = ntile_ref[e]
    n_tiles = ys_ref.shape[0] // (FFN_TILE * ROW_SLABS)
    n_experts = pl.num_programs(0)

    def fetch_first_tiles(expert):
        for k in range(FFN_LOOKAHEAD):
            @pl.when(k < ntile_ref[expert])
            def _():
                _tile_in(xs_ref, first_ref[expert] + k, xbuf_ref, k, isem).start()

    @pl.when(e == 0)
    def _():
        for slot in range(FFN_SLOTS):
            busy_ref[slot] = 0
        fetch_first_tiles(0)

    @pl.when(n > 0)
    def _():
        wgb_ref[...] = wg_ref[0].astype(BF16)
        wub_ref[...] = wu_ref[0].astype(BF16)
        wdb_ref[...] = wd_ref[0].astype(BF16)

        def ring_pass(kr, carry):
            for slot in range(FFN_SLOTS):
                k = FFN_SLOTS * kr + slot

                @pl.when(k < n)
                def _():
                    _tile_in(xs_ref, t0 + k, xbuf_ref, slot, isem).wait()

                    @pl.when(k + FFN_LOOKAHEAD < n)
                    def _():
                        _tile_in(xs_ref, t0 + k + FFN_LOOKAHEAD, xbuf_ref, (slot + FFN_LOOKAHEAD) % FFN_SLOTS,
                                 isem).start()

                    @pl.when(busy_ref[slot] == 1)
                    def _():
                        _tile_out(ybuf_ref, slot, ys_ref, t0, osem).wait()

                    x = _load_slabs(xbuf_ref, FFN_TILE, (slot,)).astype(BF16)
                    a = jnp.dot(x, wgb_ref[...], preferred_element_type=F32)
                    b = jnp.dot(x, wub_ref[...], preferred_element_type=F32)
                    act = (a * jax.nn.sigmoid(a) * b).astype(BF16)
                    _store_slabs(ybuf_ref, jnp.dot(act, wdb_ref[...], preferred_element_type=F32), (slot,))
                    _tile_out(ybuf_ref, slot, ys_ref, t0 + k, osem).start()
                    busy_ref[slot] = 1
            return carry

        lax.fori_loop(0, (n + FFN_SLOTS - 1) // FFN_SLOTS, ring_pass, 0)

        @pl.when(e + 1 < n_experts)
        def _():
            fetch_first_tiles(e + 1)


    @pl.when((n == 0) & (e + 1 < n_experts))
    def _():
        fetch_first_tiles(e + 1)

    @pl.when(e == n_experts - 1)
    def _():
        for slot in range(FFN_SLOTS):
            @pl.when(busy_ref[slot] == 1)
            def _():
                _tile_out(ybuf_ref, slot, ys_ref, 0, osem).wait()
        ybuf_ref[0] = jnp.zeros(ybuf_ref.shape[1:], F32)

        def start_tail(k, carry):
            _tile_out(ybuf_ref, 0, ys_ref, k, osem).start()
            return carry

        def wait_tail(k, carry):
            _tile_out(ybuf_ref, 0, ys_ref, k, osem).wait()
            return carry

        lax.fori_loop(nused_ref[0], n_tiles, start_tail, 0)
        lax.fori_loop(nused_ref[0], n_tiles, wait_tail, 0)


def _ffn(xs, first_tile, n_tile, n_used, n_rows, w_gate, w_up, w_down):
    wspec = lambda shape: pl.BlockSpec((1,) + shape, lambda e, *_: (e, 0, 0))
    tile_buf = pltpu.VMEM((FFN_SLOTS, FFN_TILE * ROW_SLABS, LANES), F32)
    return pl.pallas_call(
        _ffn_kernel,
        grid_spec=pltpu.PrefetchScalarGridSpec(
            num_scalar_prefetch=3,
            grid=(N_EXPERTS,),
            in_specs=[pl.BlockSpec(memory_space=pl.ANY),
                      wspec((D_MODEL, D_EXPERT)), wspec((D_MODEL, D_EXPERT)), wspec((D_EXPERT, D_MODEL))],
            out_specs=pl.BlockSpec(memory_space=pl.ANY),
            scratch_shapes=[tile_buf, tile_buf,
                            pltpu.VMEM((D_MODEL, D_EXPERT), BF16), pltpu.VMEM((D_MODEL, D_EXPERT), BF16),
                            pltpu.VMEM((D_EXPERT, D_MODEL), BF16),
                            pltpu.SMEM((FFN_SLOTS,), jnp.int32),
                            pltpu.SemaphoreType.DMA((FFN_SLOTS,)), pltpu.SemaphoreType.DMA((FFN_SLOTS,))]),
        out_shape=jax.ShapeDtypeStruct((n_rows * ROW_SLABS, LANES), F32),
        compiler_params=pltpu.CompilerParams(dimension_semantics=("arbitrary",),
                                             vmem_limit_bytes=VMEM_LIMIT_BYTES),
        name="ffn",
    )(first_tile, n_tile, n_used, xs, w_gate, w_up, w_down)


def _combine_kernel(p1_ref, p2_ref, p1n_ref, p2n_ref, ys_ref, meta_ref, x1_ref, gf_ref, out_ref, buf_ref, sem):
    tm = x1_ref.shape[0]
    i = pl.program_id(0)

    def gather(pa_ref, pb_ref, half):
        def issue(t, carry):
            _row_copy(ys_ref, pa_ref[0, 0, t], buf_ref.at[half, 0], t, sem.at[half]).start(priority=0)
            _row_copy(ys_ref, pb_ref[0, 0, t], buf_ref.at[half, 1], t, sem.at[half]).start(priority=1)
            return carry

        lax.fori_loop(0, tm, issue, 0, unroll=ROW_DMA_UNROLL)

    def finish(half):
        def drain(t, carry):
            _row_copy(ys_ref, 0, buf_ref.at[half, 0], 0, sem.at[half]).wait()
            _row_copy(ys_ref, 0, buf_ref.at[half, 1], 0, sem.at[half]).wait()
            return carry

        lax.fori_loop(0, tm, drain, 0, unroll=ROW_DMA_UNROLL)
        meta = meta_ref[...]
        y = (x1_ref[...] + meta[:, 4:5] * _load_slabs(buf_ref, tm, (half, 0))
             + meta[:, 5:6] * _load_slabs(buf_ref, tm, (half, 1)))
        out_ref[...] = _rmsnorm(y, gf_ref[...])

    @pl.when(i == 0)
    def _():
        gather(p1_ref, p2_ref, 0)

    for half in (0, 1):
        @pl.when(i % 2 == half)
        def _():
            @pl.when(i + 1 < pl.num_programs(0))
            def _():
                gather(p1n_ref, p2n_ref, 1 - half)

            finish(half)


def _combine(ys, pos1, pos2, meta, x1, norm_final_g):
    T = x1.shape[0]
    tm = COMBINE_TILE
    n_steps = T // tm
    smem_row = pl.BlockSpec((1, 1, tm), lambda i: (i, 0, 0), memory_space=pltpu.SMEM)
    smem_next = pl.BlockSpec((1, 1, tm), lambda i: (jnp.minimum(i + 1, n_steps - 1), 0, 0),
                             memory_space=pltpu.SMEM)
    row = lambda n: pl.BlockSpec((tm, n), lambda i: (i, 0))
    p1 = pos1.reshape(n_steps, 1, tm)
    p2 = pos2.reshape(n_steps, 1, tm)
    return pl.pallas_call(
        _combine_kernel,
        grid=(n_steps,),
        in_specs=[smem_row, smem_row, smem_next, smem_next, pl.BlockSpec(memory_space=pl.ANY), row(META_COLS),
                  row(D_MODEL), pl.BlockSpec((1, D_MODEL), lambda i: (0, 0))],
        out_specs=row(D_MODEL),
        out_shape=jax.ShapeDtypeStruct((T, D_MODEL), F32),
        scratch_shapes=[pltpu.VMEM((2, 2, tm * ROW_SLABS, LANES), F32), pltpu.SemaphoreType.DMA((2,))],
        compiler_params=pltpu.CompilerParams(dimension_semantics=("arbitrary",),
                                             vmem_limit_bytes=VMEM_LIMIT_BYTES),
        name="combine",
    )(p1, p2, p1, p2, ys, meta, x1, norm_final_g.reshape(1, D_MODEL))


def _run_layout(counts):
    cnt = counts.reshape(-1).astype(jnp.int32)
    padded = (cnt + FFN_TILE - 1) // FFN_TILE * FFN_TILE
    ends = jnp.cumsum(padded)
    base = ends - padded
    n_used = (ends[-1:] // FFN_TILE).astype(jnp.int32)
    pad_start = jnp.concatenate([base + cnt, n_used])
    return base, pad_start, base // FFN_TILE, padded // FFN_TILE, n_used


def kernel(x, norm_mix_g, w_in, conv_w, conv_b, i_bias, f_bias, mlstm_norm_g, s5_lambda_re, s5_lambda_im,
           s5_log_dt, s5_b_re, s5_b_im, s5_c_re, s5_c_im, s5_d, s5_glu_w, s5_glu_b, w_out, norm_ffn_g,
           router_group_w, router_group_b, router_expert_w, router_expert_b, expert_w_gate, expert_w_up,
           expert_w_down, norm_final_g):
    batch, seq_len, d_model = x.shape
    assert d_model == D_MODEL and norm_mix_g.shape[0] == 1
    assert seq_len % MLSTM_CHUNK == 0 and seq_len % INPROJ_TILE == 0 and seq_len % S5_BLOCK == 0
    T = batch * seq_len
    x2d = x.reshape(T, D_MODEL)

    q, k, v, og, u, gcol, grow = _inproj(x2d, norm_mix_g[0], w_in[0], conv_w[0], conv_b[0], i_bias[0],
                                         f_bias[0], seq_len)
    out_a = _mlstm(q, k, v, og, gcol, grow, mlstm_norm_g[0], batch, seq_len)
    tables = _s5_tables(s5_lambda_re[0], s5_lambda_im[0], s5_log_dt[0], s5_b_re[0], s5_b_im[0], s5_c_re[0],
                        s5_c_im[0], s5_d[0], s5_glu_w[0], s5_glu_b[0])
    out_b = _s5(u, tables, batch, seq_len)
    x1, meta, meta_t, counts = _outproj(out_a, out_b, x2d, w_out[0], norm_ffn_g[0], router_group_w[0],
                                            router_group_b[0], router_expert_w[0], router_expert_b[0])
    n_rows = 2 * T + N_EXPERTS * FFN_TILE
    base, pad_start, first_tile, n_tile, n_used = _run_layout(counts)
    pos = _plan(meta_t, base)
    xs = _dispatch(x1, norm_ffn_g[0], pos[0], pos[1], pad_start, n_rows)
    ys = _ffn(xs, first_tile, n_tile, n_used, n_rows, expert_w_gate[0], expert_w_up[0], expert_w_down[0])
    y = _combine(ys, pos[0], pos[1], meta, x1, norm_final_g)
    return y.reshape(batch, seq_len, D_MODEL)
```

```python
import functools

import jax
import jax.numpy as jnp
from jax import lax
from jax.experimental import pallas as pl
from jax.experimental.pallas import tpu as pltpu

F32 = jnp.float32
BF16 = jnp.bfloat16
HIGHEST = lax.Precision.HIGHEST

D_MODEL = 1024
D_MLSTM = 512
D_S5 = 512
HEADS = 4
HEAD_DIM = 128
CONV_WIDTH = 4
S5_GROUP_CH = 16
S5_GROUPS = 32
S5_STATE = 64
N_GROUPS = 4
PER_GROUP = 8
N_EXPERTS = 32
D_EXPERT = 512
EPS = 1e-6
LAMBDA_RE_MAX = -1e-4

VMEM_LIMIT_BYTES = 56 * 1024 * 1024
LANES = 128
SUBLANES = 8
ROW_SLABS = D_MODEL // LANES
assert ROW_SLABS == SUBLANES

INPROJ_TILE = 1024
MLSTM_CHUNK = 256
S5_BLOCK = 16
S5_PAIR = 2
S5_STEP_GROUPS = LANES // S5_GROUP_CH
S5_STEP_PAIRS = S5_STEP_GROUPS // S5_PAIR
S5_SCAN_ROWS = 16
OUTPROJ_TILE = 512
META_COLS = 8
PLAN_TILE = 2048
DISPATCH_TILE = 512
FFN_TILE = 256
COMBINE_TILE = 1024
ROW_DMA_UNROLL = 16
FFN_SLOTS = 3
FFN_LOOKAHEAD = FFN_SLOTS - 1


def _nt_dims():
    return (((1,), (1,)), ((), ()))


def _tn_dims():
    return (((0,), (0,)), ((), ()))


def _rmsnorm(x, g):
    return x * lax.rsqrt(jnp.mean(x * x, axis=-1, keepdims=True) + EPS) * g


def _log_sigmoid(x):
    return jnp.minimum(x, 0.0) - jnp.log1p(jnp.exp(-jnp.abs(x)))


def _split_weight(w):
    w = jnp.pad(w.astype(F32), ((0, 0), (0, LANES - w.shape[1])))
    hi = w.astype(BF16)
    lo = (w - hi.astype(F32)).astype(BF16)
    return jnp.concatenate([hi, lo], axis=1), hi


def _dot_split(x, x_hi, w_hilo_ref, w_hi_ref):
    x_lo = (x - x_hi.astype(F32)).astype(BF16)
    two = jnp.dot(x_hi, w_hilo_ref[...], preferred_element_type=F32)
    return two[:, :LANES] + two[:, LANES:] + jnp.dot(x_lo, w_hi_ref[...], preferred_element_type=F32)


def _store_slabs(ref, value, lead=()):
    m = value.shape[0]
    for s in range(ROW_SLABS):
        ref[lead + (pl.ds(s, m, stride=ROW_SLABS), slice(None))] = value[:, s * LANES:(s + 1) * LANES]


def _load_slabs(ref, m, lead=()):
    return jnp.concatenate([ref[lead + (pl.ds(s, m, stride=ROW_SLABS), slice(None))] for s in range(ROW_SLABS)],
                           axis=1)


def _transpose_small(a):
    n = a.shape[1]
    eye = (lax.broadcasted_iota(jnp.int32, (n, n), 0) == lax.broadcasted_iota(jnp.int32, (n, n), 1)).astype(F32)
    return lax.dot_general(eye, a, _nt_dims(), precision=HIGHEST, preferred_element_type=F32)


def _inproj_kernel(tiles_per_seq, x_ref, g_ref, wqk_ref, wv_ref, wo_ref, wu_ref, wg2_ref, wg1_ref,
                   gb_ref, cw_ref, cb_ref,
                   q_ref, k_ref, v_ref, og_ref, u_ref, gcol_ref, grow_ref, ext_ref, us_ref):
    i = pl.program_id(0)
    tm = x_ref.shape[0]
    x = x_ref[...]
    h = x * lax.rsqrt(jnp.mean(x * x, axis=-1, keepdims=True) + EPS) * g_ref[...]
    hb = h.astype(BF16)

    qk_raw = jnp.dot(hb, wqk_ref[...], preferred_element_type=F32)

    @pl.when(i % tiles_per_seq == 0)
    def _():
        ext_ref[0:8, :] = jnp.zeros((8, ext_ref.shape[1]), F32)

    ext_ref[8:, :] = qk_raw
    cw = cw_ref[...]
    acc = cb_ref[...] + cw[CONV_WIDTH - 1:CONV_WIDTH, :] * qk_raw
    for s in range(1, CONV_WIDTH):
        acc = acc + cw[CONV_WIDTH - 1 - s:CONV_WIDTH - s, :] * ext_ref[pl.ds(8 - s, tm), :]
    ext_ref[0:8, :] = qk_raw[tm - 8:, :]
    qk = acc * jax.nn.sigmoid(acc)
    q_ref[...] = qk[:, :D_MLSTM].astype(BF16)
    k_ref[...] = (qk[:, D_MLSTM:] * (HEAD_DIM ** -0.5)).astype(BF16)

    v_ref[...] = jnp.dot(hb, wv_ref[...], preferred_element_type=F32).astype(BF16)
    o_pre = jnp.dot(hb, wo_ref[...], preferred_element_type=F32)
    og_ref[...] = jax.nn.sigmoid(o_pre).astype(BF16)

    u = jnp.dot(hb, wu_ref[...], preferred_element_type=F32)
    n_slabs = us_ref.shape[0]
    for j in range(n_slabs):
        us_ref[j] = u[:, j * LANES:(j + 1) * LANES]
    for p in range(S5_BLOCK):
        u_ref[p] = jnp.concatenate([us_ref[j, pl.ds(p, tm // S5_BLOCK, stride=S5_BLOCK), :]
                                    for j in range(n_slabs)], axis=1).astype(BF16)

    gcol = _dot_split(h, hb, wg2_ref, wg1_ref)[:, :2 * HEADS] + gb_ref[...]
    gcol_ref[...] = gcol
    grow_ref[...] = _transpose_small(gcol)


def _inproj(x2d, norm_g, w_in, conv_w, conv_b, i_bias, f_bias, seq_len):
    T = x2d.shape[0]
    tm = INPROJ_TILE
    dm = D_MLSTM
    wqk = w_in[:, :2 * dm].astype(BF16)
    wv = w_in[:, 2 * dm:3 * dm].astype(BF16)
    wo = w_in[:, 3 * dm:4 * dm].astype(BF16)
    wg2, wg1 = _split_weight(w_in[:, 4 * dm:4 * dm + 2 * HEADS])
    wu = w_in[:, 4 * dm + 2 * HEADS:].astype(BF16)
    gb = jnp.concatenate([i_bias, f_bias]).astype(F32)
    full = lambda shape: pl.BlockSpec(shape, lambda i: (0,) * len(shape))
    row = lambda n: pl.BlockSpec((tm, n), lambda i: (i, 0))
    return pl.pallas_call(
        functools.partial(_inproj_kernel, seq_len // tm),
        grid=(T // tm,),
        in_specs=[row(D_MODEL), full((1, D_MODEL)), full((D_MODEL, 2 * dm)), full((D_MODEL, dm)),
                  full((D_MODEL, dm)), full((D_MODEL, D_S5)), full((D_MODEL, 2 * LANES)),
                  full((D_MODEL, LANES)), full((1, 2 * HEADS)),
                  full((CONV_WIDTH, 2 * dm)), full((1, 2 * dm))],
        out_specs=[row(dm), row(dm), row(dm), row(dm),
                   pl.BlockSpec((S5_BLOCK, tm // S5_BLOCK, D_S5), lambda i: (0, i, 0)), row(2 * HEADS),
                   pl.BlockSpec((2 * HEADS, tm), lambda i: (0, i))],
        out_shape=[jax.ShapeDtypeStruct((T, dm), BF16)] * 4
        + [jax.ShapeDtypeStruct((S5_BLOCK, T // S5_BLOCK, D_S5), BF16),
           jax.ShapeDtypeStruct((T, 2 * HEADS), F32),
           jax.ShapeDtypeStruct((2 * HEADS, T), F32)],
        scratch_shapes=[pltpu.VMEM((tm + 8, 2 * dm), F32), pltpu.VMEM((D_S5 // LANES, tm, LANES), F32)],
        compiler_params=pltpu.CompilerParams(dimension_semantics=("arbitrary",),
                                             vmem_limit_bytes=VMEM_LIMIT_BYTES),
        name="inproj",
    )(x2d, norm_g.reshape(1, D_MODEL), wqk, wv, wo, wu, wg2, wg1, gb.reshape(1, -1),
      conv_w, conv_b.reshape(1, -1))


def _mlstm_kernel(batch, q_ref, k_ref, v_ref, og_ref, gcol_ref, ng_ref, *refs):
    grow_refs = refs[:batch]
    out_ref, c_ref, m_ref = refs[batch:]
    tc = q_ref.shape[1]
    d = HEAD_DIM

    @pl.when(pl.program_id(0) == 0)
    def _():
        c_ref[...] = jnp.zeros(c_ref.shape, F32)
        m_ref[...] = jnp.zeros(m_ref.shape, F32)

    causal = lax.broadcasted_iota(jnp.int32, (tc, tc), 0) >= lax.broadcasted_iota(jnp.int32, (tc, tc), 1)
    ones_col = (lax.broadcasted_iota(jnp.int32, (tc, d), 1) == 0).astype(BF16)
    for b in range(batch):
        gcol = gcol_ref[b]
        grow = grow_refs[b][...]
        b_col_all = jnp.dot(causal.astype(F32), _log_sigmoid(gcol), precision=HIGHEST,
                            preferred_element_type=F32)
        b_row_all = _transpose_small(b_col_all)
        for h in range(HEADS):
            _mlstm_head(b, h, causal, ones_col, gcol, grow, b_col_all, b_row_all,
                        q_ref, k_ref, v_ref, og_ref, ng_ref, out_ref, c_ref, m_ref)


def _mlstm_head(b, h, causal, ones_col, gcol, grow, b_col_all, b_row_all,
                q_ref, k_ref, v_ref, og_ref, ng_ref, out_ref, c_ref, m_ref):
    tc, d = q_ref.shape[1], HEAD_DIM
    hs = slice(h * d, (h + 1) * d)
    st = b * HEADS + h
    li_col = gcol[:, h:h + 1]
    li_row = grow[h:h + 1, :]
    b_col = b_col_all[:, HEADS + h:HEADS + h + 1]
    b_row = b_row_all[HEADS + h:HEADS + h + 1, :]
    m_prev = m_ref[st:st + 1, 0:1]

    dmat = jnp.where(causal, b_col - b_row + li_row, -jnp.inf)
    inter = b_col + m_prev
    m_row = jnp.maximum(inter, jnp.max(dmat, axis=1, keepdims=True))
    w_intra = jnp.exp(dmat - m_row)
    w_inter = jnp.exp(inter - m_row)

    qh = q_ref[b, :, hs]
    kh = k_ref[b, :, hs]
    v_aug = jnp.concatenate([v_ref[b, :, hs], ones_col], axis=1)
    s = lax.dot_general(qh, kh, _nt_dims(), preferred_element_type=F32) * w_intra
    c_aug = c_ref[st]
    tot = (jnp.dot(s.astype(BF16), v_aug, preferred_element_type=F32)
           + w_inter * lax.dot_general(qh, c_aug.astype(BF16), _nt_dims(), preferred_element_type=F32))
    num = tot[:, :d]
    den = tot[:, d:d + 1]
    hh = num / jnp.maximum(jnp.abs(den), jnp.exp(-m_row))
    hh = hh * lax.rsqrt(jnp.mean(hh * hh, axis=-1, keepdims=True) + EPS)
    out_ref[b, :, hs] = (og_ref[b, :, hs].astype(F32) * (hh * ng_ref[:, hs])).astype(BF16)

    b_last = b_col[tc - 1:tc, :]
    d_state = b_last - b_col + li_col
    m_new = jnp.maximum(b_last + m_prev, jnp.max(d_state, axis=0, keepdims=True))
    w_s = jnp.exp(d_state - m_new)
    w_c = jnp.exp(b_last + m_prev - m_new)
    upd = lax.dot_general((v_aug.astype(F32) * w_s).astype(BF16), kh, _tn_dims(),
                          preferred_element_type=F32)
    c_ref[st] = w_c * c_aug + upd
    m_ref[st:st + 1, :] = jnp.broadcast_to(m_new, (1, m_ref.shape[1]))


def _mlstm(q, k, v, og, gcol, grow, norm_g, batch, seq_len):
    tc = MLSTM_CHUNK
    nc = seq_len // tc
    T = q.shape[0]
    n_state = batch * HEADS
    seq = lambda a: a.reshape(batch, seq_len, a.shape[1])
    rows = lambda n: pl.BlockSpec((batch, tc, n), lambda c: (0, c, 0))
    grow_specs = [pl.BlockSpec((2 * HEADS, tc), lambda c, b=b: (0, b * nc + c)) for b in range(batch)]
    out = pl.pallas_call(
        functools.partial(_mlstm_kernel, batch),
        grid=(nc,),
        in_specs=[rows(D_MLSTM), rows(D_MLSTM), rows(D_MLSTM), rows(D_MLSTM), rows(2 * HEADS),
                  pl.BlockSpec((1, D_MLSTM), lambda c: (0, 0))] + grow_specs,
        out_specs=rows(D_MLSTM),
        out_shape=jax.ShapeDtypeStruct((batch, seq_len, D_MLSTM), BF16),
        scratch_shapes=[pltpu.VMEM((n_state, 2 * HEAD_DIM, HEAD_DIM), F32),
                        pltpu.VMEM((-(-n_state // SUBLANES) * SUBLANES, LANES), F32)],
        compiler_params=pltpu.CompilerParams(dimension_semantics=("arbitrary",),
                                             vmem_limit_bytes=VMEM_LIMIT_BYTES),
        name="mlstm",
    )(seq(q), seq(k), seq(v), seq(og), seq(gcol), norm_g.reshape(1, D_MLSTM), *([grow] * batch))
    return out.reshape(T, D_MLSTM)


def _cplx_pow(exponent, theta_re, theta_im):
    mag = jnp.exp(exponent * theta_re)
    return mag * jnp.cos(exponent * theta_im), mag * jnp.sin(exponent * theta_im)


def _s5_tables_kernel(lrr_ref, lir_ref, dtr_ref, lrc_ref, lic_ref, dtc_ref, btr_ref, bti_ref, ctr_ref, cti_ref,
                      dcol_ref, gw1_ref, gw2_ref, gbcol_ref,
                      we_ref, kt_ref, wc_ref, glu_ref, glub_ref, dsk_ref, sr_ref, si_ref):
    P, C, N = S5_BLOCK, S5_GROUP_CH, S5_STATE
    PC, NN = P * C, S5_PAIR * S5_STATE
    shift_c, shift_n = C.bit_length() - 1, N.bit_length() - 1

    lr = jnp.minimum(lrr_ref[0], LAMBDA_RE_MAX)
    li = lir_ref[0]
    dt = jnp.exp(dtr_ref[0])
    th_r, th_i = lr * dt, li * dt
    ar, ai = _cplx_pow(1.0, th_r, th_i)
    den = lr * lr + li * li
    cr = ((ar - 1.0) * lr + ai * li) / den
    ci = (ai * lr - (ar - 1.0) * li) / den
    btr, bti = btr_ref[0], bti_ref[0]
    bbr = cr * btr - ci * bti
    bbi = cr * bti + ci * btr

    back = (P - 1 - lax.broadcasted_iota(jnp.int32, (P, 1), 0)).astype(F32)
    pwr, pwi = _cplx_pow(back, th_r, th_i)
    val_r = jnp.concatenate([pwr[p:p + 1] * bbr - pwi[p:p + 1] * bbi for p in range(P)], axis=0)
    val_i = jnp.concatenate([pwr[p:p + 1] * bbi + pwi[p:p + 1] * bbr for p in range(P)], axis=0)
    lane_grp = lax.broadcasted_iota(jnp.int32, (PC, NN), 1) >> shift_n
    we_r = jnp.concatenate([jnp.where(lane_grp == g, val_r, 0.0) for g in range(S5_PAIR)], axis=0)
    we_i = jnp.concatenate([jnp.where(lane_grp == g, val_i, 0.0) for g in range(S5_PAIR)], axis=0)
    we_ref[0] = jnp.concatenate([we_r, we_i], axis=1).astype(BF16)

    steps = (P << lax.broadcasted_iota(jnp.int32, (S5_SCAN_ROWS, 1), 0)).astype(F32)
    sr_ref[0], si_ref[0] = _cplx_pow(steps, th_r, th_i)

    lrc = jnp.minimum(lrc_ref[0], LAMBDA_RE_MAX)
    dtc = jnp.exp(dtc_ref[0])
    thc_r, thc_i = lrc * dtc, lic_ref[0] * dtc
    lane = lax.broadcasted_iota(jnp.int32, (NN, PC), 1)
    p0r, p0i = _cplx_pow((lane >> shift_c).astype(F32), thc_r, thc_i)
    acr, aci = _cplx_pow(1.0, thc_r, thc_i)
    p1r, p1i = p0r * acr - p0i * aci, p0r * aci + p0i * acr
    tile_c = ((lax.broadcasted_iota(jnp.int32, (C, PC), 1) & (C - 1))
              == lax.broadcasted_iota(jnp.int32, (C, PC), 0)).astype(F32)
    ctr = jnp.dot(ctr_ref[0], tile_c, precision=HIGHEST, preferred_element_type=F32)
    cti = jnp.dot(cti_ref[0], tile_c, precision=HIGHEST, preferred_element_type=F32)
    w0r, w0i = ctr * p0r - cti * p0i, ctr * p0i + cti * p0r
    vr, vi = ctr * p1r - cti * p1i, -(ctr * p1i + cti * p1r)
    row_grp = lax.broadcasted_iota(jnp.int32, (NN, PC), 0) >> shift_n
    wc_r = jnp.concatenate([jnp.where(row_grp == g, vr, 0.0) for g in range(S5_PAIR)], axis=1)
    wc_i = jnp.concatenate([jnp.where(row_grp == g, vi, 0.0) for g in range(S5_PAIR)], axis=1)
    wc_ref[0] = jnp.concatenate([wc_r, wc_i], axis=0).astype(BF16)

    b_grp = lax.broadcasted_iota(jnp.int32, (C, NN), 1) >> shift_n
    lane_k = lax.broadcasted_iota(jnp.int32, (C, PC), 1)
    same_pos = ((lax.broadcasted_iota(jnp.int32, (PC, PC), 0) >> shift_c)
                == (lax.broadcasted_iota(jnp.int32, (PC, PC), 1) >> shift_c))
    zero_blk = jnp.zeros((PC, PC), F32)

    def on_diagonal(blk, g):
        return jnp.concatenate([blk if j == g else zero_blk for j in range(S5_PAIR)], axis=1)

    kt_rows, glu_v, glu_g, bias_v, bias_g, skip = [], [], [], [], [], []
    for g in range(S5_PAIR):
        k0 = (jnp.dot(jnp.where(b_grp == g, bbr, 0.0), w0r, precision=HIGHEST, preferred_element_type=F32)
              - jnp.dot(jnp.where(b_grp == g, bbi, 0.0), w0i, precision=HIGHEST, preferred_element_type=F32))
        toep = jnp.concatenate(
            [k0 if p == 0 else jnp.where(lane_k >= C * p, pltpu.roll(k0, C * p, axis=1), 0.0) for p in range(P)],
            axis=0)
        kt_rows.append(on_diagonal(toep, g))
        for w_ref, dst in ((gw1_ref, glu_v), (gw2_ref, glu_g)):
            t = jnp.dot(w_ref[0, g], tile_c, precision=HIGHEST, preferred_element_type=F32)
            dst.append(on_diagonal(jnp.where(same_pos, jnp.concatenate([t] * P, axis=0), 0.0), g))
        gb = gbcol_ref[0, g]
        bias_v.append(jnp.sum(gb[:C] * tile_c, axis=0, keepdims=True))
        bias_g.append(jnp.sum(gb[C:] * tile_c, axis=0, keepdims=True))
        skip.append(jnp.sum(dcol_ref[0, g] * tile_c, axis=0, keepdims=True))
    kt_ref[0] = jnp.concatenate(kt_rows, axis=0).astype(BF16)
    glu_ref[0] = jnp.concatenate([jnp.concatenate(glu_v, axis=0), jnp.concatenate(glu_g, axis=0)],
                                 axis=1).astype(BF16)
    glub_ref[0] = jnp.concatenate(bias_v + bias_g, axis=1)
    dsk_ref[0] = jnp.concatenate(skip, axis=1)


def _s5_tables(lam_re, lam_im, log_dt, b_re, b_im, c_re, c_im, d_skip, glu_w, glu_b):
    G, N, C, P = S5_GROUPS, S5_STATE, S5_GROUP_CH, S5_BLOCK
    J, NN, PC = G // S5_PAIR, S5_PAIR * S5_STATE, S5_BLOCK * S5_GROUP_CH
    f = lambda a: a.astype(F32)
    dt_full = jnp.broadcast_to(f(log_dt)[:, None], (G, N))
    rows = [a.reshape(J, 1, NN) for a in (f(lam_re), f(lam_im), dt_full)]
    cols = [a.reshape(J, NN, 1) for a in (f(lam_re), f(lam_im), dt_full)]
    bt = [f(a).reshape(J, S5_PAIR, N, C).transpose(0, 3, 1, 2).reshape(J, C, NN) for a in (b_re, b_im)]
    ct = [f(a).transpose(0, 2, 1).reshape(J, NN, C) for a in (c_re, c_im)]
    dcol = f(d_skip).reshape(J, S5_PAIR, C, 1)
    gw1 = f(glu_w)[..., :C].reshape(J, S5_PAIR, C, C)
    gw2 = f(glu_w)[..., C:].reshape(J, S5_PAIR, C, C)
    gbcol = f(glu_b).reshape(J, S5_PAIR, 2 * C, 1)
    ins = rows + cols + bt + ct + [dcol, gw1, gw2, gbcol]
    per_pair = lambda shape: pl.BlockSpec((1,) + shape, lambda j: (j,) + (0,) * len(shape))
    out_shapes = [((S5_PAIR * PC, 2 * NN), BF16), ((S5_PAIR * PC, S5_PAIR * PC), BF16),
                  ((2 * NN, S5_PAIR * PC), BF16), ((S5_PAIR * PC, 2 * S5_PAIR * PC), BF16),
                  ((1, 2 * S5_PAIR * PC), F32), ((1, S5_PAIR * PC), F32),
                  ((S5_SCAN_ROWS, NN), F32), ((S5_SCAN_ROWS, NN), F32)]
    return pl.pallas_call(
        _s5_tables_kernel,
        grid=(J,),
        in_specs=[per_pair(a.shape[1:]) for a in ins],
        out_specs=[per_pair(s) for s, _ in out_shapes],
        out_shape=[jax.ShapeDtypeStruct((J,) + s, d) for s, d in out_shapes],
        compiler_params=pltpu.CompilerParams(dimension_semantics=("parallel",),
                                             vmem_limit_bytes=VMEM_LIMIT_BYTES),
        name="s5_tables",
    )(*ins)


def _s5_kernel(n_steps, u_ref, we_ref, kt_ref, wc_ref, glu_ref, glub_ref, dsk_ref, sr_ref, si_ref, out_ref):
    P, C = S5_BLOCK, S5_GROUP_CH
    nb, slab = u_ref.shape[1], u_ref.shape[2]
    wide = slab * P
    pair_w = S5_PAIR * P * C
    half = S5_PAIR * S5_STATE
    shift_c = C.bit_length() - 1

    lane = lax.broadcasted_iota(jnp.int32, (slab, wide), 1)
    src = lax.broadcasted_iota(jnp.int32, (slab, wide), 0)
    target0 = ((src >> shift_c) << (shift_c + P.bit_length() - 1)) + (src & (C - 1))

    def place(p0):
        return jnp.concatenate([jnp.where(lane == target0 + p * C, 1.0, 0.0).astype(BF16)
                                for p in (p0, p0 + 1)], axis=0)

    u_blk = jnp.zeros((nb, wide), F32)
    for p0 in range(0, P, 2):
        lhs = jnp.concatenate([u_ref[p0], u_ref[p0 + 1]], axis=1)
        u_blk = u_blk + jnp.dot(lhs, place(p0), preferred_element_type=F32)
    u_blk = u_blk.astype(BF16)

    row = lax.broadcasted_iota(jnp.int32, (nb, half), 0)
    outs = []
    for q in range(S5_STEP_PAIRS):
        u = u_blk[:, q * pair_w:(q + 1) * pair_w]
        e = jnp.dot(u, we_ref[q], preferred_element_type=F32)
        xr, xi = e[:, :half], e[:, half:]
        for k in range(n_steps):
            sh = 1 << k
            ar = sr_ref[q, k:k + 1, :]
            ai = si_ref[q, k:k + 1, :]
            valid = row >= sh
            yr = jnp.where(valid, pltpu.roll(xr, sh, axis=0), 0.0)
            yi = jnp.where(valid, pltpu.roll(xi, sh, axis=0), 0.0)
            xr, xi = xr + ar * yr - ai * yi, xi + ar * yi + ai * yr
        first = row >= 1
        pr = jnp.where(first, pltpu.roll(xr, 1, axis=0), 0.0)
        pi = jnp.where(first, pltpu.roll(xi, 1, axis=0), 0.0)
        sprev = jnp.concatenate([pr, pi], axis=1).astype(BF16)
        y = (jnp.dot(u, kt_ref[q], preferred_element_type=F32)
             + jnp.dot(sprev, wc_ref[q], preferred_element_type=F32)
             + dsk_ref[q] * u.astype(F32))
        y = jax.nn.gelu(y)
        z = jnp.dot(y.astype(BF16), glu_ref[q], preferred_element_type=F32) + glub_ref[q]
        outs.append(z[:, :pair_w] * jax.nn.sigmoid(z[:, pair_w:]))
    out_blk = jnp.concatenate(outs, axis=1).astype(BF16)
    for p0 in range(0, P, 2):
        o = lax.dot_general(out_blk, place(p0), _nt_dims(), preferred_element_type=F32)
        out_ref[p0] = o[:, :slab].astype(BF16)
        out_ref[p0 + 1] = o[:, slab:].astype(BF16)


def _s5(u, tables, batch, seq_len):
    nb = seq_len // S5_BLOCK
    n_steps = max(1, (nb - 1).bit_length())
    assert n_steps <= S5_SCAN_ROWS
    slab = pl.BlockSpec((S5_BLOCK, nb, LANES), lambda b, o: (0, b, o))
    per_step = lambda a: pl.BlockSpec((S5_STEP_PAIRS,) + a.shape[1:], lambda b, o: (o, 0, 0))
    return pl.pallas_call(
        functools.partial(_s5_kernel, n_steps),
        grid=(batch, S5_GROUPS // S5_STEP_GROUPS),
        in_specs=[slab] + [per_step(a) for a in tables],
        out_specs=slab,
        out_shape=jax.ShapeDtypeStruct(u.shape, BF16),
        compiler_params=pltpu.CompilerParams(dimension_semantics=("parallel", "parallel"),
                                             vmem_limit_bytes=VMEM_LIMIT_BYTES),
        name="s5",
    )(u, *tables)


def _outproj_kernel(a_ref, b_ref, x_ref, wa_ref, wb_ref, g_ref, rw2_ref, rw1_ref, rgb_ref, reb_ref,
                    x1_ref, meta_ref, metat_ref, cnt_ref, bs_ref):
    tm = x_ref.shape[0]

    @pl.when(pl.program_id(0) == 0)
    def _():
        cnt_ref[...] = jnp.zeros(cnt_ref.shape, F32)

    n_slabs = bs_ref.shape[0]
    for p in range(S5_BLOCK):
        bp = b_ref[p].astype(F32)
        for j in range(n_slabs):
            bs_ref[j, pl.ds(p, tm // S5_BLOCK, stride=S5_BLOCK), :] = bp[:, j * LANES:(j + 1) * LANES]
    out_b = jnp.concatenate([bs_ref[j] for j in range(n_slabs)], axis=1).astype(BF16)
    x1 = (x_ref[...]
          + jnp.dot(a_ref[...], wa_ref[...], preferred_element_type=F32)
          + jnp.dot(out_b, wb_ref[...], preferred_element_type=F32))
    x1_ref[...] = x1
    h2 = _rmsnorm(x1, g_ref[...])

    logits = _dot_split(h2, h2.astype(BF16), rw2_ref, rw1_ref)
    el = logits[:, :N_EXPERTS] + reb_ref[...]
    gl = logits[:, N_EXPERTS:N_EXPERTS + N_GROUPS] + rgb_ref[...]
    gmax = jnp.max(gl, axis=1, keepdims=True)
    g_w = 1.0 / jnp.sum(jnp.exp(gl - gmax), axis=1, keepdims=True)
    lane_g = lax.broadcasted_iota(jnp.int32, (tm, N_GROUPS), 1)
    g_idx = jnp.min(jnp.where(gl == gmax, lane_g, N_GROUPS), axis=1, keepdims=True)
    lane_e = lax.broadcasted_iota(jnp.int32, (tm, N_EXPERTS), 1)
    elm = jnp.where(lane_e // PER_GROUP == g_idx, el, -jnp.inf)
    m1 = jnp.max(elm, axis=1, keepdims=True)
    i1 = jnp.min(jnp.where(elm == m1, lane_e, N_EXPERTS), axis=1, keepdims=True)
    elm2 = jnp.where(lane_e == i1, -jnp.inf, elm)
    m2 = jnp.max(elm2, axis=1, keepdims=True)
    i2 = jnp.min(jnp.where(elm2 == m2, lane_e, N_EXPERTS), axis=1, keepdims=True)
    e2 = jnp.exp(m2 - m1)
    w1 = g_w / (1.0 + e2)
    w2 = g_w * e2 / (1.0 + e2)

    sel1 = lane_e == i1
    sel2 = lane_e == i2
    picked = (sel1 | sel2).astype(BF16)
    rix = lax.broadcasted_iota(jnp.int32, (tm, tm), 0)
    cix = lax.broadcasted_iota(jnp.int32, (tm, tm), 1)
    before = jnp.dot((rix > cix).astype(BF16), picked, preferred_element_type=F32) + cnt_ref[...]
    r1 = jnp.sum(jnp.where(sel1, before, 0.0), axis=1, keepdims=True)
    r2 = jnp.sum(jnp.where(sel2, before, 0.0), axis=1, keepdims=True)
    cnt_ref[...] += jnp.sum(picked.astype(F32), axis=0, keepdims=True)
    col = lax.broadcasted_iota(jnp.int32, (tm, META_COLS), 1)
    meta = jnp.zeros((tm, META_COLS), F32)
    for c, val in enumerate((i1.astype(F32), i2.astype(F32), r1, r2, w1, w2)):
        meta = jnp.where(col == c, val, meta)
    meta_ref[...] = meta
    metat_ref[...] = _transpose_small(meta)


def _outproj(out_a, out_b, x2d, w_out, norm_g, rg_w, rg_b, re_w, re_b):
    T = x2d.shape[0]
    tm = OUTPROJ_TILE
    wa = w_out[:D_MLSTM].astype(BF16)
    wb = w_out[D_MLSTM:].astype(BF16)
    rw2, rw1 = _split_weight(jnp.concatenate([re_w, rg_w], axis=1))
    full = lambda shape: pl.BlockSpec(shape, lambda i: (0,) * len(shape))
    row = lambda n: pl.BlockSpec((tm, n), lambda i: (i, 0))
    return pl.pallas_call(
        _outproj_kernel,
        grid=(T // tm,),
        in_specs=[row(D_MLSTM), pl.BlockSpec((S5_BLOCK, tm // S5_BLOCK, D_S5), lambda i: (0, i, 0)), row(D_MODEL),
                  full((D_MLSTM, D_MODEL)), full((D_S5, D_MODEL)),
                  full((1, D_MODEL)), full((D_MODEL, 2 * LANES)), full((D_MODEL, LANES)), full((1, N_GROUPS)),
                  full((1, N_EXPERTS))],
        out_specs=[row(D_MODEL), row(META_COLS), pl.BlockSpec((META_COLS, tm), lambda i: (0, i)),
                   full((1, N_EXPERTS))],
        out_shape=[jax.ShapeDtypeStruct((T, D_MODEL), F32),
                   jax.ShapeDtypeStruct((T, META_COLS), F32), jax.ShapeDtypeStruct((META_COLS, T), F32),
                   jax.ShapeDtypeStruct((1, N_EXPERTS), F32)],
        scratch_shapes=[pltpu.VMEM((D_S5 // LANES, tm, LANES), F32)],
        compiler_params=pltpu.CompilerParams(dimension_semantics=("arbitrary",),
                                             vmem_limit_bytes=VMEM_LIMIT_BYTES),
        name="outproj",
    )(out_a, out_b, x2d, wa, wb, norm_g.reshape(1, D_MODEL), rw2, rw1, rg_b.reshape(1, -1), re_b.reshape(1, -1))


def _plan_kernel(metat_ref, base_ref, pos_ref):
    mt = metat_ref[...]
    tt = mt.shape[1]
    expert = lax.broadcasted_iota(jnp.int32, (N_EXPERTS, tt), 0)
    base = base_ref[...]
    row = lax.broadcasted_iota(jnp.int32, (META_COLS, tt), 0)
    pos = jnp.zeros((META_COLS, tt), F32)
    for k in range(2):
        first = jnp.sum(jnp.where(expert == mt[k:k + 1, :].astype(jnp.int32), base, 0.0), axis=0, keepdims=True)
        pos = jnp.where(row == k, first + mt[2 + k:3 + k, :], pos)
    pos_ref[...] = pos.astype(jnp.int32)


def _plan(meta_t, base):
    T = meta_t.shape[1]
    tt = PLAN_TILE
    return pl.pallas_call(
        _plan_kernel,
        grid=(T // tt,),
        in_specs=[pl.BlockSpec((META_COLS, tt), lambda i: (0, i)), pl.BlockSpec((N_EXPERTS, 1), lambda i: (0, 0))],
        out_specs=pl.BlockSpec((META_COLS, tt), lambda i: (0, i)),
        out_shape=jax.ShapeDtypeStruct((META_COLS, T), jnp.int32),
        compiler_params=pltpu.CompilerParams(dimension_semantics=("parallel",),
                                             vmem_limit_bytes=VMEM_LIMIT_BYTES),
        name="plan",
    )(meta_t, base.astype(F32).reshape(N_EXPERTS, 1))


def _slab_rows(ref, row, n_rows):
    return ref.at[pl.ds(pl.multiple_of(row * ROW_SLABS, ROW_SLABS), n_rows * ROW_SLABS), :]


def _row_copy(src_ref, src_row, dst_ref, dst_row, sem):
    return pltpu.make_async_copy(_slab_rows(src_ref, src_row, 1), _slab_rows(dst_ref, dst_row, 1), sem)


def _pad_copy(zero_ref, xs_ref, row, sem):
    return pltpu.make_async_copy(zero_ref, _slab_rows(xs_ref, row, FFN_TILE), sem)


def _dispatch_kernel(pad_ref, p1_ref, p2_ref, x1_ref, g_ref, xs_ref, h_ref, zero_ref, sem, zsem):
    tm = x1_ref.shape[0]
    n_tiles = xs_ref.shape[0] // (FFN_TILE * ROW_SLABS)
    i = pl.program_id(0)

    @pl.when(i == 0)
    def _():
        zero_ref[...] = jnp.zeros(zero_ref.shape, F32)
        for e in range(N_EXPERTS):
            _pad_copy(zero_ref, xs_ref, pad_ref[e], zsem).start()
        for e in range(N_EXPERTS):
            _pad_copy(zero_ref, xs_ref, 0, zsem).wait()
        first_free = pad_ref[N_EXPERTS]

        def start_tail(k, carry):
            _pad_copy(zero_ref, xs_ref, k * FFN_TILE, zsem).start()
            return carry

        def wait_tail(k, carry):
            _pad_copy(zero_ref, xs_ref, 0, zsem).wait()
            return carry

        lax.fori_loop(first_free, n_tiles, start_tail, 0)
        lax.fori_loop(first_free, n_tiles, wait_tail, 0)

    def drain(half):
        def wait_two(t, carry):
            _row_copy(h_ref.at[half], 0, xs_ref, 0, sem.at[half]).wait()
            _row_copy(h_ref.at[half], 0, xs_ref, 0, sem.at[half]).wait()
            return carry

        lax.fori_loop(0, tm, wait_two, 0, unroll=ROW_DMA_UNROLL)

    for half in (0, 1):
        @pl.when(i % 2 == half)
        def _():
            @pl.when(i >= 2)
            def _():
                drain(half)

            _store_slabs(h_ref, _rmsnorm(x1_ref[...], g_ref[...]), (half,))

            def issue(t, carry):
                _row_copy(h_ref.at[half], t, xs_ref, p1_ref[0, 0, t], sem.at[half]).start(priority=0)
                _row_copy(h_ref.at[half], t, xs_ref, p2_ref[0, 0, t], sem.at[half]).start(priority=1)
                return carry

            lax.fori_loop(0, tm, issue, 0, unroll=ROW_DMA_UNROLL)

            @pl.when(i == pl.num_programs(0) - 1)
            def _():
                drain(half)

                @pl.when(i >= 1)
                def _():
                    drain(1 - half)


def _dispatch(x1, norm_g, pos1, pos2, pad_start, n_rows):
    T = x1.shape[0]
    tm = DISPATCH_TILE
    smem_row = pl.BlockSpec((1, 1, tm), lambda i, pad: (i, 0, 0), memory_space=pltpu.SMEM)
    return pl.pallas_call(
        _dispatch_kernel,
        grid_spec=pltpu.PrefetchScalarGridSpec(
            num_scalar_prefetch=1,
            grid=(T // tm,),
            in_specs=[smem_row, smem_row, pl.BlockSpec((tm, D_MODEL), lambda i, pad: (i, 0)),
                      pl.BlockSpec((1, D_MODEL), lambda i, pad: (0, 0))],
            out_specs=pl.BlockSpec(memory_space=pl.ANY),
            scratch_shapes=[pltpu.VMEM((2, tm * ROW_SLABS, LANES), F32),
                            pltpu.VMEM((FFN_TILE * ROW_SLABS, LANES), F32), pltpu.SemaphoreType.DMA((2,)),
                            pltpu.SemaphoreType.DMA]),
        out_shape=jax.ShapeDtypeStruct(((n_rows + FFN_TILE) * ROW_SLABS, LANES), F32),
        compiler_params=pltpu.CompilerParams(dimension_semantics=("arbitrary",),
                                             vmem_limit_bytes=VMEM_LIMIT_BYTES),
        name="dispatch",
    )(pad_start, pos1.reshape(T // tm, 1, tm), pos2.reshape(T // tm, 1, tm), x1, norm_g.reshape(1, D_MODEL))


def _tile_in(xs_ref, tile, xbuf_ref, slot, sem_ref):
    return pltpu.make_async_copy(_slab_rows(xs_ref, tile * FFN_TILE, FFN_TILE), xbuf_ref.at[slot], sem_ref.at[slot])


def _tile_out(ybuf_ref, slot, ys_ref, tile, sem_ref):
    return pltpu.make_async_copy(ybuf_ref.at[slot], _slab_rows(ys_ref, tile * FFN_TILE, FFN_TILE), sem_ref.at[slot])


def _ffn_kernel(first_ref, ntile_ref, nused_ref, xs_ref, wg_ref, wu_ref, wd_ref, ys_ref,
                xbuf_ref, ybuf_ref, wgb_ref, wub_ref, wdb_ref, busy_ref, isem, osem):
    e = pl.program_id(0)
    t0 = first_ref[e]
    n = ntile_ref[e]
    n_tiles = ys_ref.shape[0] // (FFN_TILE * ROW_SLABS)
    n_experts = pl.num_programs(0)

    def fetch_first_tiles(expert):
        for k in range(FFN_LOOKAHEAD):
            @pl.when(k < ntile_ref[expert])
            def _():
                _tile_in(xs_ref, first_ref[expert] + k, xbuf_ref, k, isem).start()

    @pl.when(e == 0)
    def _():
        for slot in range(FFN_SLOTS):
            busy_ref[slot] = 0
        fetch_first_tiles(0)

    @pl.when(n > 0)
    def _():
        wgb_ref[...] = wg_ref[0].astype(BF16)
        wub_ref[...] = wu_ref[0].astype(BF16)
        wdb_ref[...] = wd_ref[0].astype(BF16)

        def ring_pass(kr, carry):
            for slot in range(FFN_SLOTS):
                k = FFN_SLOTS * kr + slot

                @pl.when(k < n)
                def _():
                    _tile_in(xs_ref, t0 + k, xbuf_ref, slot, isem).wait()

                    @pl.when(k + FFN_LOOKAHEAD < n)
                    def _():
                        _tile_in(xs_ref, t0 + k + FFN_LOOKAHEAD, xbuf_ref, (slot + FFN_LOOKAHEAD) % FFN_SLOTS,
                                 isem).start()

                    @pl.when(busy_ref[slot] == 1)
                    def _():
                        _tile_out(ybuf_ref, slot, ys_ref, t0, osem).wait()

                    x = _load_slabs(xbuf_ref, FFN_TILE, (slot,)).astype(BF16)
                    a = jnp.dot(x, wgb_ref[...], preferred_element_type=F32)
                    b = jnp.dot(x, wub_ref[...], preferred_element_type=F32)
                    act = (a * jax.nn.sigmoid(a) * b).astype(BF16)
                    _store_slabs(ybuf_ref, jnp.dot(act, wdb_ref[...], preferred_element_type=F32), (slot,))
                    _tile_out(ybuf_ref, slot, ys_ref, t0 + k, osem).start()
                    busy_ref[slot] = 1
            return carry

        lax.fori_loop(0, (n + FFN_SLOTS - 1) // FFN_SLOTS, ring_pass, 0)

        @pl.when(e + 1 < n_experts)
        def _():
            fetch_first_tiles(e + 1)


    @pl.when((n == 0) & (e + 1 < n_experts))
    def _():
        fetch_first_tiles(e + 1)

    @pl.when(e == n_experts - 1)
    def _():
        for slot in range(FFN_SLOTS):
            @pl.when(busy_ref[slot] == 1)
            def _():
                _tile_out(ybuf_ref, slot, ys_ref, 0, osem).wait()
        ybuf_ref[0] = jnp.zeros(ybuf_ref.shape[1:], F32)

        def start_tail(k, carry):
            _tile_out(ybuf_ref, 0, ys_ref, k, osem).start()
            return carry

        def wait_tail(k, carry):
            _tile_out(ybuf_ref, 0, ys_ref, k, osem).wait()
            return carry

        lax.fori_loop(nused_ref[0], n_tiles, start_tail, 0)
        lax.fori_loop(nused_ref[0], n_tiles, wait_tail, 0)


def _ffn(xs, first_tile, n_tile, n_used, n_rows, w_gate, w_up, w_down):
    wspec = lambda shape: pl.BlockSpec((1,) + shape, lambda e, *_: (e, 0, 0))
    tile_buf = pltpu.VMEM((FFN_SLOTS, FFN_TILE * ROW_SLABS, LANES), F32)
    return pl.pallas_call(
        _ffn_kernel,
        grid_spec=pltpu.PrefetchScalarGridSpec(
            num_scalar_prefetch=3,
            grid=(N_EXPERTS,),
            in_specs=[pl.BlockSpec(memory_space=pl.ANY),
                      wspec((D_MODEL, D_EXPERT)), wspec((D_MODEL, D_EXPERT)), wspec((D_EXPERT, D_MODEL))],
            out_specs=pl.BlockSpec(memory_space=pl.ANY),
            scratch_shapes=[tile_buf, tile_buf,
                            pltpu.VMEM((D_MODEL, D_EXPERT), BF16), pltpu.VMEM((D_MODEL, D_EXPERT), BF16),
                            pltpu.VMEM((D_EXPERT, D_MODEL), BF16),
                            pltpu.SMEM((FFN_SLOTS,), jnp.int32),
                            pltpu.SemaphoreType.DMA((FFN_SLOTS,)), pltpu.SemaphoreType.DMA((FFN_SLOTS,))]),
        out_shape=jax.ShapeDtypeStruct((n_rows * ROW_SLABS, LANES), F32),
        compiler_params=pltpu.CompilerParams(dimension_semantics=("arbitrary",),
                                             vmem_limit_bytes=VMEM_LIMIT_BYTES),
        name="ffn",
    )(first_tile, n_tile, n_used, xs, w_gate, w_up, w_down)


def _combine_kernel(p1_ref, p2_ref, p1n_ref, p2n_ref, ys_ref, meta_ref, x1_ref, gf_ref, out_ref, buf_ref, sem):
    tm = x1_ref.shape[0]
    i = pl.program_id(0)

    def gather(pa_ref, pb_ref, half):
        def issue(t, carry):
            _row_copy(ys_ref, pa_ref[0, 0, t], buf_ref.at[half, 0], t, sem.at[half]).start(priority=0)
            _row_copy(ys_ref, pb_ref[0, 0, t], buf_ref.at[half, 1], t, sem.at[half]).start(priority=1)
            return carry

        lax.fori_loop(0, tm, issue, 0, unroll=ROW_DMA_UNROLL)

    def finish(half):
        def drain(t, carry):
            _row_copy(ys_ref, 0, buf_ref.at[half, 0], 0, sem.at[half]).wait()
            _row_copy(ys_ref, 0, buf_ref.at[half, 1], 0, sem.at[half]).wait()
            return carry

        lax.fori_loop(0, tm, drain, 0, unroll=ROW_DMA_UNROLL)
        meta = meta_ref[...]
        y = (x1_ref[...] + meta[:, 4:5] * _load_slabs(buf_ref, tm, (half, 0))
             + meta[:, 5:6] * _load_slabs(buf_ref, tm, (half, 1)))
        out_ref[...] = _rmsnorm(y, gf_ref[...])

    @pl.when(i == 0)
    def _():
        gather(p1_ref, p2_ref, 0)

    for half in (0, 1):
        @pl.when(i % 2 == half)
        def _():
            @pl.when(i + 1 < pl.num_programs(0))
            def _():
                gather(p1n_ref, p2n_ref, 1 - half)

            finish(half)


def _combine(ys, pos1, pos2, meta, x1, norm_final_g):
    T = x1.shape[0]
    tm = COMBINE_TILE
    n_steps = T // tm
    smem_row = pl.BlockSpec((1, 1, tm), lambda i: (i, 0, 0), memory_space=pltpu.SMEM)
    smem_next = pl.BlockSpec((1, 1, tm), lambda i: (jnp.minimum(i + 1, n_steps - 1), 0, 0),
                             memory_space=pltpu.SMEM)
    row = lambda n: pl.BlockSpec((tm, n), lambda i: (i, 0))
    p1 = pos1.reshape(n_steps, 1, tm)
    p2 = pos2.reshape(n_steps, 1, tm)
    return pl.pallas_call(
        _combine_kernel,
        grid=(n_steps,),
        in_specs=[smem_row, smem_row, smem_next, smem_next, pl.BlockSpec(memory_space=pl.ANY), row(META_COLS),
                  row(D_MODEL), pl.BlockSpec((1, D_MODEL), lambda i: (0, 0))],
        out_specs=row(D_MODEL),
        out_shape=jax.ShapeDtypeStruct((T, D_MODEL), F32),
        scratch_shapes=[pltpu.VMEM((2, 2, tm * ROW_SLABS, LANES), F32), pltpu.SemaphoreType.DMA((2,))],
        compiler_params=pltpu.CompilerParams(dimension_semantics=("arbitrary",),
                                             vmem_limit_bytes=VMEM_LIMIT_BYTES),
        name="combine",
    )(p1, p2, p1, p2, ys, meta, x1, norm_final_g.reshape(1, D_MODEL))


def _run_layout(counts):
    cnt = counts.reshape(-1).astype(jnp.int32)
    padded = (cnt + FFN_TILE - 1) // FFN_TILE * FFN_TILE
    ends = jnp.cumsum(padded)
    base = ends - padded
    n_used = (ends[-1:] // FFN_TILE).astype(jnp.int32)
    pad_start = jnp.concatenate([base + cnt, n_used])
    return base, pad_start, base // FFN_TILE, padded // FFN_TILE, n_used


def kernel(x, norm_mix_g, w_in, conv_w, conv_b, i_bias, f_bias, mlstm_norm_g, s5_lambda_re, s5_lambda_im,
           s5_log_dt, s5_b_re, s5_b_im, s5_c_re, s5_c_im, s5_d, s5_glu_w, s5_glu_b, w_out, norm_ffn_g,
           router_group_w, router_group_b, router_expert_w, router_expert_b, expert_w_gate, expert_w_up,
           expert_w_down, norm_final_g):
    batch, seq_len, d_model = x.shape
    assert d_model == D_MODEL and norm_mix_g.shape[0] == 1
    assert seq_len % MLSTM_CHUNK == 0 and seq_len % INPROJ_TILE == 0 and seq_len % S5_BLOCK == 0
    T = batch * seq_len
    x2d = x.reshape(T, D_MODEL)

    q, k, v, og, u, gcol, grow = _inproj(x2d, norm_mix_g[0], w_in[0], conv_w[0], conv_b[0], i_bias[0],
                                         f_bias[0], seq_len)
    out_a = _mlstm(q, k, v, og, gcol, grow, mlstm_norm_g[0], batch, seq_len)
    tables = _s5_tables(s5_lambda_re[0], s5_lambda_im[0], s5_log_dt[0], s5_b_re[0], s5_b_im[0], s5_c_re[0],
                        s5_c_im[0], s5_d[0], s5_glu_w[0], s5_glu_b[0])
    out_b = _s5(u, tables, batch, seq_len)
    x1, meta, meta_t, counts = _outproj(out_a, out_b, x2d, w_out[0], norm_ffn_g[0], router_group_w[0],
                                            router_group_b[0], router_expert_w[0], router_expert_b[0])
    n_rows = 2 * T + N_EXPERTS * FFN_TILE
    base, pad_start, first_tile, n_tile, n_used = _run_layout(counts)
    pos = _plan(meta_t, base)
    xs = _dispatch(x1, norm_ffn_g[0], pos[0], pos[1], pad_start, n_rows)
    ys = _ffn(xs, first_tile, n_tile, n_used, n_rows, expert_w_gate[0], expert_w_up[0], expert_w_down[0])
    y = _combine(ys, pos[0], pos[1], meta, x1, norm_final_g)
    return y.reshape(batch, seq_len, D_MODEL)
```
